```python
import math
import jax
import jax.numpy as jnp
from jax import lax
import numpy as np

D_MODEL = 1024
BATCH = 32
SEQ = 2048
DEPTH = 4

CTX_LEN = 256
GRID_W = 64
RMS_EPS = 1e-6
L2_EPS = 1e-6
ROPE_THETA = 10000.0
Q_BLOCK = 128

HEAD_DIM = 64
ATTN_SCALE = HEAD_DIM ** -0.5
DIFF_HEADS = 4
DIFF_V_DIM = 2 * HEAD_DIM
GQA_Q_HEADS = 8
GQA_KV_HEADS = 2
GQA_REP = GQA_Q_HEADS // GQA_KV_HEADS
DIFF_QK_W = DIFF_HEADS * 2 * HEAD_DIM
DIFF_V_W = DIFF_HEADS * DIFF_V_DIM
GQA_Q_W = GQA_Q_HEADS * HEAD_DIM
GQA_KV_W = GQA_KV_HEADS * HEAD_DIM
ATTN_SPLIT_IDX = (DIFF_QK_W, 2 * DIFF_QK_W, 2 * DIFF_QK_W + DIFF_V_W,
                  2 * DIFF_QK_W + DIFF_V_W + GQA_Q_W,
                  2 * DIFF_QK_W + DIFF_V_W + GQA_Q_W + GQA_KV_W)
ATTN_IN_W = 2 * DIFF_QK_W + DIFF_V_W + GQA_Q_W + 2 * GQA_KV_W
ATTN_OUT_W = DIFF_V_W + GQA_Q_W

GDN_HEADS = 8
GDN_HEAD_DIM = 128
GDN_W = GDN_HEADS * GDN_HEAD_DIM
GDN_CONV_K = 4
GDN_CONV_LEFT = 2
GDN_CHUNK = 64
GDN_IN_W = 4 * GDN_W + 4 * GDN_HEADS

FFN_HIDDEN = -(-8 * D_MODEL // (3 * 256)) * 256

kernel_name = "hybrid_diffgqa_gdeltanet_dit_prefix"


def rms_norm(x, g):
    xf = x.astype(jnp.float32)
    y = xf * lax.rsqrt(jnp.mean(xf * xf, axis=-1, keepdims=True) + RMS_EPS)
    return (y * g.astype(jnp.float32)).astype(x.dtype)


def l2_normalize(x):
    xf = x.astype(jnp.float32)
    return xf * lax.rsqrt(jnp.sum(xf * xf, axis=-1, keepdims=True) + L2_EPS)


def modulate(h, shift, scale):
    return h * (1.0 + scale) + shift


def axial_rope_tables(n_tokens, rows):
    row_ids = jnp.repeat(jnp.arange(rows, dtype=jnp.float32), GRID_W)[:n_tokens]
    col_ids = jnp.tile(jnp.arange(GRID_W, dtype=jnp.float32), rows)[:n_tokens]
    axis_dim = HEAD_DIM // 2
    inv_freq = ROPE_THETA ** (-jnp.arange(0, axis_dim, 2, dtype=jnp.float32) / axis_dim)
    ang_r = row_ids[:, None] * inv_freq
    ang_c = col_ids[:, None] * inv_freq
    ang = jnp.concatenate([ang_r, ang_r, ang_c, ang_c], axis=-1)
    return jnp.cos(ang), jnp.sin(ang)


def apply_axial_rope(x, cos, sin):
    shape = (1, cos.shape[0]) + (1,) * (x.ndim - 3) + (cos.shape[1],)
    cos = cos.reshape(shape).astype(x.dtype)
    sin = sin.reshape(shape).astype(x.dtype)
    x1, x2, x3, x4 = jnp.split(x, 4, axis=-1)
    rot = jnp.concatenate([-x2, x1, -x4, x3], axis=-1)
    return x * cos + rot * sin


def sweep_query_blocks(fn, *qs):
    B, L = qs[0].shape[:2]
    n_blk = L // Q_BLOCK
    blocks = tuple(jnp.swapaxes(q.reshape((B, n_blk, Q_BLOCK) + q.shape[2:]), 0, 1) for q in qs)
    out = lax.map(lambda blk: fn(*blk), blocks)
    return jnp.swapaxes(out, 0, 1).reshape((B, L) + out.shape[3:])


def attn_project(h, w_in, qk_g, rope):
    B, L, _ = h.shape
    p = h @ w_in
    dq, dk, dv, gq, gk, gv = jnp.split(p, ATTN_SPLIT_IDX, axis=-1)
    dq = dq.reshape(B, L, DIFF_HEADS, 2, HEAD_DIM)
    dk = dk.reshape(B, L, DIFF_HEADS, 2, HEAD_DIM)
    dv = dv.reshape(B, L, DIFF_HEADS, DIFF_V_DIM)
    gq = rms_norm(gq.reshape(B, L, GQA_KV_HEADS, GQA_REP, HEAD_DIM), qk_g[0])
    gk = rms_norm(gk.reshape(B, L, GQA_KV_HEADS, HEAD_DIM), qk_g[1])
    gv = gv.reshape(B, L, GQA_KV_HEADS, HEAD_DIM)
    if rope is not None:
        cos, sin = rope
        dq = apply_axial_rope(dq, cos, sin)
        dk = apply_axial_rope(dk, cos, sin)
        gq = apply_axial_rope(gq, cos, sin)
        gk = apply_axial_rope(gk, cos, sin)
    return dq, dk, dv, gq, gk, gv


def attend_heads(dq, gq, keys, lam, lam_init, subln_g):
    dk, dv, gk, gv = keys
    B, Q = dq.shape[:2]
    s = jnp.einsum('bqhmd,bthmd->bhmqt', dq, dk) * ATTN_SCALE
    p = jax.nn.softmax(s.astype(jnp.float32), axis=-1)
    a = (p[:, :, 0] - lam * p[:, :, 1]).astype(dv.dtype)
    d_out = jnp.einsum('bhqt,bthe->bqhe', a, dv)
    d_out = rms_norm(d_out, subln_g) * (1.0 - lam_init)
    s = jnp.einsum('bqgrd,btgd->bgrqt', gq, gk) * ATTN_SCALE
    p = jax.nn.softmax(s.astype(jnp.float32), axis=-1).astype(gv.dtype)
    g_out = jnp.einsum('bgrqt,btgd->bqgrd', p, gv)
    return jnp.concatenate([d_out.reshape(B, Q, DIFF_V_W), g_out.reshape(B, Q, GQA_Q_W)], axis=-1)


def even_mixer(h_lat, h_ctx, w_in, w_out, lam_vec, subln_g, qk_g, lam_init, rope, need_ctx):
    lv = lam_vec.astype(jnp.float32)
    lam = jnp.exp(jnp.sum(lv[0] * lv[1])) - jnp.exp(jnp.sum(lv[2] * lv[3])) + lam_init
    cq_d, ck_d, cv_d, cq_g, ck_g, cv_g = attn_project(h_ctx, w_in, qk_g, None)
    lq_d, lk_d, lv_d, lq_g, lk_g, lv_g = attn_project(h_lat, w_in, qk_g, rope)
    ctx_keys = (ck_d, cv_d, ck_g, cv_g)
    all_keys = tuple(jnp.concatenate([kc, kl], axis=1)
                     for kc, kl in zip(ctx_keys, (lk_d, lv_d, lk_g, lv_g)))
    o_lat = sweep_query_blocks(
        lambda dq, gq: attend_heads(dq, gq, all_keys, lam, lam_init, subln_g), lq_d, lq_g)
    y_lat = o_lat @ w_out
    y_ctx = None
    if need_ctx:
        y_ctx = attend_heads(cq_d, cq_g, ctx_keys, lam, lam_init, subln_g) @ w_out
    return y_lat, y_ctx


def short_conv(x, w):
    L = x.shape[1]
    xp = jnp.pad(x, ((0, 0), (GDN_CONV_LEFT, GDN_CONV_K - 1 - GDN_CONV_LEFT), (0, 0)))
    out = xp[:, 0:L] * w[0]
    for j in range(1, GDN_CONV_K):
        out = out + xp[:, j:j + L] * w[j]
    return out


def gdn_project(h, w_in, conv_w, a_log, dt_bias):
    B, L, _ = h.shape
    p = h @ w_in
    qkv, z, a, b = jnp.split(p, (3 * GDN_W, 4 * GDN_W, 4 * GDN_W + 2 * GDN_HEADS), axis=-1)
    qkv = jax.nn.silu(short_conv(qkv, conv_w))
    q, k, v = (t.reshape(B, L, GDN_HEADS, GDN_HEAD_DIM) for t in jnp.split(qkv, 3, axis=-1))
    q = l2_normalize(q) * (GDN_HEAD_DIM ** -0.5)
    k = l2_normalize(k)
    a = a.reshape(B, L, 2, GDN_HEADS).astype(jnp.float32)
    b = b.reshape(B, L, 2, GDN_HEADS).astype(jnp.float32)
    g = -jnp.exp(a_log.astype(jnp.float32)) * jax.nn.softplus(a + dt_bias.astype(jnp.float32))
    beta = jax.nn.sigmoid(b)
    return q, k, v, z.reshape(B, L, GDN_HEADS, GDN_HEAD_DIM), g, beta


def gated_delta_chunked(q, k, v, g, beta, state0):
    B, L, H, DK = q.shape
    DV = v.shape[-1]
    n = L // GDN_CHUNK
    f32 = jnp.float32

    def to_chunks(t):
        return jnp.transpose(t.astype(f32).reshape(B, n, GDN_CHUNK, H, t.shape[-1]), (1, 0, 3, 2, 4))

    qc, kc, vc = to_chunks(q), to_chunks(k), to_chunks(v)
    gc = jnp.transpose(g.astype(f32).reshape(B, n, GDN_CHUNK, H), (1, 0, 3, 2))
    bc = jnp.transpose(beta.astype(f32).reshape(B, n, GDN_CHUNK, H), (1, 0, 3, 2))
    gcum = jnp.cumsum(gc, axis=-1)
    tril = jnp.tril(jnp.ones((GDN_CHUNK, GDN_CHUNK), dtype=bool))
    strict = jnp.tril(jnp.ones((GDN_CHUNK, GDN_CHUNK), dtype=bool), -1)
    diff = gcum[..., :, None] - gcum[..., None, :]
    decay = jnp.where(tril, jnp.exp(jnp.where(tril, diff, 0.0)), 0.0)
    kb = kc * bc[..., None]
    vb = vc * bc[..., None]
    m = jnp.where(strict, jnp.einsum('nbhid,nbhjd->nbhij', kb, kc) * decay, 0.0)
    eye = jnp.eye(GDN_CHUNK, dtype=f32)
    tmat = lax.linalg.triangular_solve(eye + m, jnp.broadcast_to(eye, m.shape),
                                       left_side=True, lower=True, unit_diagonal=True)
    u = jnp.einsum('nbhij,nbhjd->nbhid', tmat, vb)
    w = jnp.einsum('nbhij,nbhjd->nbhid', tmat, kb * jnp.exp(gcum)[..., None])

    def step(S, xs):
        q_i, k_i, u_i, w_i, g_i, dec_i = xs
        attn = jnp.einsum('bhid,bhjd->bhij', q_i, k_i) * dec_i
        v_new = u_i - jnp.einsum('bhid,bhde->bhie', w_i, S)
        o = (jnp.einsum('bhid,bhde->bhie', q_i * jnp.exp(g_i)[..., None], S)
             + jnp.einsum('bhij,bhje->bhie', attn, v_new))
        g_last = g_i[..., -1]
        S = (S * jnp.exp(g_last)[..., None, None]
             + jnp.einsum('bhid,bhie->bhde', k_i * jnp.exp(g_last[..., None] - g_i)[..., None], v_new))
        return S, o

    S, o = lax.scan(step, state0.astype(f32), (qc, kc, u, w, gcum, decay))
    o = jnp.transpose(o, (1, 0, 3, 2, 4)).reshape(B, L, H, DV)
    return o.astype(v.dtype), S


def odd_mixer(h_lat, h_ctx, w_in, conv_w, a_log, dt_bias, norm_g, w_out, need_ctx):
    cq, ck, cv, cz, cg, cb = gdn_project(h_ctx, w_in, conv_w, a_log, dt_bias)
    lq, lk, lv, lz, lg, lb = gdn_project(h_lat, w_in, conv_w, a_log, dt_bias)
    B = h_lat.shape[0]
    s0 = jnp.zeros((B, GDN_HEADS, GDN_HEAD_DIM, GDN_HEAD_DIM), jnp.float32)
    flip = lambda t: jnp.flip(t, axis=1)
    o_cf, s_f = gated_delta_chunked(cq, ck, cv, cg[:, :, 0], cb[:, :, 0], s0)
    o_lf, _ = gated_delta_chunked(lq, lk, lv, lg[:, :, 0], lb[:, :, 0], s_f)
    o_cb, s_b = gated_delta_chunked(flip(cq), flip(ck), flip(cv), flip(cg[:, :, 1]), flip(cb[:, :, 1]), s0)
    o_lb, _ = gated_delta_chunked(flip(lq), flip(lk), flip(lv), flip(lg[:, :, 1]), flip(lb[:, :, 1]), s_b)

    def readout(o, z):
        o = rms_norm(o, norm_g) * jax.nn.silu(z)
        return o.reshape(o.shape[0], o.shape[1], GDN_W) @ w_out

    y_lat = readout(o_lf + flip(o_lb), lz)
    y_ctx = readout(o_cf + flip(o_cb), cz) if need_ctx else None
    return y_lat, y_ctx


def swiglu(h, w_gate_up, w_down):
    gate, up = jnp.split(h @ w_gate_up, 2, axis=-1)
    return (jax.nn.silu(gate) * up) @ w_down


def setup_inputs(seed: int = 0) -> dict:
    key = jax.random.key(seed)
    ks = jax.random.split(key, 20)
    f32 = jnp.float32
    n_even = (DEPTH + 1) // 2
    n_odd = DEPTH // 2

    def nrm(k, shape, scale):
        return jax.random.normal(k, shape, f32) * scale

    x = nrm(ks[0], (BATCH, SEQ, D_MODEL), 1.0)
    c = nrm(ks[1], (BATCH, D_MODEL), 1.0)
    ctx = nrm(ks[2], (BATCH, CTX_LEN, D_MODEL), 1.0)
    c_ctx = nrm(ks[3], (D_MODEL,), 1.0)
    ada_w = nrm(ks[4], (DEPTH, D_MODEL, 6 * D_MODEL), 0.5 * D_MODEL ** -0.5)
    ada_b = nrm(ks[5], (DEPTH, 6 * D_MODEL), 0.02)
    norm_g = 1.0 + nrm(ks[6], (DEPTH, 4, D_MODEL), 0.05)
    attn_w_in = nrm(ks[7], (n_even, D_MODEL, ATTN_IN_W), D_MODEL ** -0.5)
    attn_w_out = nrm(ks[8], (n_even, ATTN_OUT_W, D_MODEL), ATTN_OUT_W ** -0.5)
    diff_lambda = nrm(ks[9], (n_even, 4, HEAD_DIM), 0.1)
    diff_subln_g = 1.0 + nrm(ks[10], (n_even, DIFF_V_DIM), 0.05)
    gqa_qk_g = 1.0 + nrm(ks[11], (n_even, 2, HEAD_DIM), 0.05)
    gdn_w_in = nrm(ks[12], (n_odd, D_MODEL, GDN_IN_W), D_MODEL ** -0.5)
    gdn_conv_w = nrm(ks[13], (n_odd, GDN_CONV_K, 3 * GDN_W), GDN_CONV_K ** -0.5)
    gdn_a_log = jnp.log(jax.random.uniform(ks[14], (n_odd, 2, GDN_HEADS), f32, 1.0, 16.0))
    dt = jnp.exp(jax.random.uniform(ks[15], (n_odd, 2, GDN_HEADS), f32, math.log(1e-3), math.log(1e-1)))
    gdn_dt_bias = dt + jnp.log(-jnp.expm1(-dt))
    gdn_norm_g = 1.0 + nrm(ks[16], (n_odd, GDN_HEAD_DIM), 0.05)
    gdn_w_out = nrm(ks[17], (n_odd, GDN_W, D_MODEL), GDN_W ** -0.5)
    ffn_w_gate_up = nrm(ks[18], (DEPTH, D_MODEL, 2 * FFN_HIDDEN), D_MODEL ** -0.5)
    ffn_w_down = nrm(ks[19], (DEPTH, FFN_HIDDEN, D_MODEL), FFN_HIDDEN ** -0.5)
    return {"x": x, "c": c, "ctx": ctx, "c_ctx": c_ctx, "ada_w": ada_w, "ada_b": ada_b,
            "norm_g": norm_g, "attn_w_in": attn_w_in, "attn_w_out": attn_w_out,
            "diff_lambda": diff_lambda, "diff_subln_g": diff_subln_g, "gqa_qk_g": gqa_qk_g,
            "gdn_w_in": gdn_w_in, "gdn_conv_w": gdn_conv_w, "gdn_a_log": gdn_a_log,
            "gdn_dt_bias": gdn_dt_bias, "gdn_norm_g": gdn_norm_g, "gdn_w_out": gdn_w_out,
            "ffn_w_gate_up": ffn_w_gate_up, "ffn_w_down": ffn_w_down}


def reference(x, c, ctx, c_ctx, ada_w, ada_b, norm_g, attn_w_in, attn_w_out, diff_lambda,
              diff_subln_g, gqa_qk_g, gdn_w_in, gdn_conv_w, gdn_a_log, gdn_dt_bias, gdn_norm_g,
              gdn_w_out, ffn_w_gate_up, ffn_w_down):
    n_tokens = x.shape[1]
    rows = n_tokens // GRID_W
    rope = axial_rope_tables(n_tokens, rows)
    x_ctx = ctx
    silu_c = jax.nn.silu(c)
    silu_cc = jax.nn.silu(c_ctx)
    for l in range(DEPTH):
        need_ctx = l < DEPTH - 1
        i = l // 2
        mod_lat = (silu_c @ ada_w[l] + ada_b[l])[:, None, :]
        mod_ctx = silu_cc @ ada_w[l] + ada_b[l]
        sh1, sc1, gt1, sh2, sc2, gt2 = jnp.split(mod_lat, 6, axis=-1)
        csh1, csc1, cgt1, csh2, csc2, cgt2 = jnp.split(mod_ctx, 6, axis=-1)
        h_lat = modulate(rms_norm(x, norm_g[l, 0]), sh1, sc1)
        h_ctx = modulate(rms_norm(x_ctx, norm_g[l, 0]), csh1, csc1)
        if l % 2 == 0:
            lam_init = 0.8 - 0.6 * math.exp(-0.3 * l)
            y_lat, y_ctx = even_mixer(h_lat, h_ctx, attn_w_in[i], attn_w_out[i], diff_lambda[i],
                                      diff_subln_g[i], gqa_qk_g[i], lam_init, rope, need_ctx)
        else:
            y_lat, y_ctx = odd_mixer(h_lat, h_ctx, gdn_w_in[i], gdn_conv_w[i], gdn_a_log[i],
                                     gdn_dt_bias[i], gdn_norm_g[i], gdn_w_out[i], need_ctx)
        x = x + gt1 * rms_norm(y_lat, norm_g[l, 1])
        h = modulate(rms_norm(x, norm_g[l, 2]), sh2, sc2)
        x = x + gt2 * rms_norm(swiglu(h, ffn_w_gate_up[l], ffn_w_down[l]), norm_g[l, 3])
        if need_ctx:
            x_ctx = x_ctx + cgt1 * rms_norm(y_ctx, norm_g[l, 1])
            hc = modulate(rms_norm(x_ctx, norm_g[l, 2]), csh2, csc2)
            x_ctx = x_ctx + cgt2 * rms_norm(swiglu(hc, ffn_w_gate_up[l], ffn_w_down[l]), norm_g[l, 3])
    return x
```

```python
import functools
import math

import jax
import jax.numpy as jnp
from jax import lax
from jax.experimental import pallas as pl
from jax.experimental.pallas import tpu as pltpu

F32 = jnp.float32
BF16 = jnp.bfloat16

D_MODEL = 1024
CTX_LEN = 256
GRID_W = 64
RMS_EPS = 1e-6
L2_EPS = 1e-6
ROPE_THETA = 10000.0

HEAD_DIM = 64
ATTN_SCALE = HEAD_DIM ** -0.5
DIFF_HEADS = 4
DIFF_V_DIM = 2 * HEAD_DIM
GQA_Q_HEADS = 8
GQA_KV_HEADS = 2
GQA_REP = GQA_Q_HEADS // GQA_KV_HEADS
DIFF_QK_W = DIFF_HEADS * 2 * HEAD_DIM
DIFF_V_W = DIFF_HEADS * DIFF_V_DIM
GQA_Q_W = GQA_Q_HEADS * HEAD_DIM
GQA_KV_W = GQA_KV_HEADS * HEAD_DIM
Q_W = DIFF_QK_W + GQA_Q_W
K_W = DIFF_QK_W + GQA_KV_W
V_W = DIFF_V_W + GQA_KV_W

GDN_HEADS = 8
GDN_HEAD_DIM = 128
GDN_W = GDN_HEADS * GDN_HEAD_DIM
GDN_CONV_K = 4
GDN_CONV_LEFT = 2
GDN_CHUNK = 64
CONV_HALO = 8

FFN_HIDDEN = 2816
FFN_CHUNK = 256

TOK_TILE = 256
KV_CHUNK = 256
VMEM_LIMIT = 56 * 1024 * 1024

_NT = (((1,), (1,)), ((), ()))
_TN = (((0,), (0,)), ((), ()))


def _rms(x, g):
    return x * lax.rsqrt(jnp.mean(x * x, axis=-1, keepdims=True) + RMS_EPS) * g


def _silu(x):
    return x * jax.nn.sigmoid(x)


def _params(sem):
    return pltpu.CompilerParams(dimension_semantics=sem, vmem_limit_bytes=VMEM_LIMIT)


def _const_spec(shape):
    n = len(shape)
    return pl.BlockSpec(shape, lambda *_: (0,) * n)


def _mod_body(c_ref, w_ref, b_ref, o_ref):
    sc = _silu(c_ref[...])
    o_ref[0] = jnp.dot(sc, w_ref[0], precision=lax.Precision.HIGHEST,
                       preferred_element_type=F32) + b_ref[0]


def _modulation(cc, ada_w, ada_b):
    depth, d, w6 = ada_w.shape
    rows = cc.shape[0]
    nblk = w6 // d
    return pl.pallas_call(
        _mod_body,
        grid=(depth, nblk),
        in_specs=[pl.BlockSpec((rows, d), lambda l, j: (0, 0)),
                  pl.BlockSpec((1, d, d), lambda l, j: (l, 0, j)),
                  pl.BlockSpec((1, 1, d), lambda l, j: (l, 0, j))],
        out_specs=pl.BlockSpec((1, rows, d), lambda l, j: (l, 0, j)),
        out_shape=jax.ShapeDtypeStruct((depth, rows, w6), F32),
        compiler_params=_params(("parallel", "parallel")),
        name="adaln_mod",
    )(cc, ada_w, ada_b.reshape(depth, 1, w6))


def _attn_proj_body(x_ref, mod_ref, g_ref, wqk_ref, wv_ref, cos_ref, sin_ref, qkg_ref,
                    q_out, kt_out, v_out):
    x = x_ref[0]
    mod = mod_ref[0, 0]
    h = (_rms(x, g_ref[...]) * (1.0 + mod[1:2]) + mod[0:1]).astype(BF16)
    v_out[0] = jnp.dot(h, wv_ref[...], preferred_element_type=F32).astype(BF16)
    qk = lax.dot_general(wqk_ref[...], h, _NT, preferred_element_type=F32)
    cos = cos_ref[...]
    sin = sin_ref[...]

    def rope(xh):
        swapped = jnp.concatenate([xh[16:32], xh[0:16], xh[48:64], xh[32:48]], axis=0)
        return xh * cos + swapped * sin

    def norm(xh, g):
        r = lax.rsqrt(jnp.mean(xh * xh, axis=0, keepdims=True) + RMS_EPS)
        return xh * r * g

    gq_g = qkg_ref[0]
    gk_g = qkg_ref[1]
    q_heads = []
    for j in range(Q_W // HEAD_DIM):
        xh = qk[j * HEAD_DIM:(j + 1) * HEAD_DIM]
        if j * HEAD_DIM >= DIFF_QK_W:
            xh = norm(xh, gq_g)
        q_heads.append(rope(xh) * ATTN_SCALE)
    q_out[0] = jnp.concatenate(q_heads, axis=0).T.astype(BF16)
    for j in range(K_W // HEAD_DIM):
        r0 = j * HEAD_DIM
        xh = qk[Q_W + r0:Q_W + r0 + HEAD_DIM]
        if r0 >= DIFF_QK_W:
            xh = norm(xh, gk_g)
        kt_out[0, 0, r0:r0 + HEAD_DIM, :] = rope(xh).astype(BF16)


def _attn_proj(xa, modl, g0, wqk_t, wv, cos_t, sin_t, qkg):
    b, t, d = xa.shape
    nt = t // TOK_TILE
    return pl.pallas_call(
        _attn_proj_body,
        grid=(b, nt),
        in_specs=[pl.BlockSpec((1, TOK_TILE, d), lambda i, j: (i, j, 0)),
                  pl.BlockSpec((1, 1, 6, d), lambda i, j: (i, jnp.minimum(j, 1), 0, 0)),
                  _const_spec((1, d)),
                  _const_spec((Q_W + K_W, d)),
                  _const_spec((d, V_W)),
                  pl.BlockSpec((HEAD_DIM, TOK_TILE), lambda i, j: (0, j)),
                  pl.BlockSpec((HEAD_DIM, TOK_TILE), lambda i, j: (0, j)),
                  _const_spec((2, HEAD_DIM, 1))],
        out_specs=[pl.BlockSpec((1, TOK_TILE, Q_W), lambda i, j: (i, j, 0)),
                   pl.BlockSpec((1, 1, K_W, TOK_TILE), lambda i, j: (i, j, 0, 0)),
                   pl.BlockSpec((1, TOK_TILE, V_W), lambda i, j: (i, j, 0))],
        out_shape=[jax.ShapeDtypeStruct((b, t, Q_W), BF16),
                   jax.ShapeDtypeStruct((b, nt, K_W, TOK_TILE), BF16),
                   jax.ShapeDtypeStruct((b, t, V_W), BF16)],
        compiler_params=_params(("parallel", "parallel")),
        name="attn_in_proj",
    )(xa, modl, g0, wqk_t, wv, cos_t, sin_t, qkg)


def _flash_head(q, kt_ref, k_row0, v_ref, v_col0, v_w, n_kv):
    tq = q.shape[0]

    def step(c, carry):
        m, l, acc = carry
        kt = kt_ref[0, c, k_row0:k_row0 + HEAD_DIM, :]
        s = jnp.dot(q, kt, preferred_element_type=F32)
        m_new = jnp.maximum(m, jnp.max(s, axis=-1, keepdims=True))
        alpha = jnp.exp(m - m_new)
        p = jnp.exp(s - m_new)
        l = alpha * l + jnp.sum(p, axis=-1, keepdims=True)
        c0 = pl.multiple_of(c * KV_CHUNK, KV_CHUNK)
        v = v_ref[0, pl.ds(c0, KV_CHUNK), v_col0:v_col0 + v_w]
        acc = alpha * acc + jnp.dot(p.astype(BF16), v, preferred_element_type=F32)
        return m_new, l, acc

    init = (jnp.full((tq, 1), -jnp.inf, F32), jnp.zeros((tq, 1), F32), jnp.zeros((tq, v_w), F32))
    _, l, acc = lax.fori_loop(0, n_kv, step, init)
    return acc / l


def _attn_body(lam_init, q_ref, kt_ref, v_ref, lam_ref, subln_ref, o_ref):
    qt = pl.program_id(1)
    n_kv = jnp.where(qt == 0, 1, kt_ref.shape[1])
    lv = lam_ref[...]
    lam = (jnp.exp(jnp.sum(lv[0:1] * lv[1:2], axis=-1, keepdims=True))
           - jnp.exp(jnp.sum(lv[2:3] * lv[3:4], axis=-1, keepdims=True)) + lam_init)

    def q_head(j):
        return q_ref[0, :, j * HEAD_DIM:(j + 1) * HEAD_DIM]

    for hd in range(DIFF_HEADS):
        r0 = hd * 2 * HEAD_DIM
        o1 = _flash_head(q_head(2 * hd), kt_ref, r0, v_ref, hd * DIFF_V_DIM, DIFF_V_DIM, n_kv)
        o2 = _flash_head(q_head(2 * hd + 1), kt_ref, r0 + HEAD_DIM, v_ref,
                         hd * DIFF_V_DIM, DIFF_V_DIM, n_kv)
        o = _rms(o1 - lam * o2, subln_ref[...]) * (1.0 - lam_init)
        o_ref[0, :, hd * DIFF_V_DIM:(hd + 1) * DIFF_V_DIM] = o.astype(o_ref.dtype)

    lane = lax.broadcasted_iota(jnp.int32, (o_ref.shape[1], 2 * HEAD_DIM), 1)
    for pair in range(GQA_Q_HEADS // 2):
        grp = (2 * pair) // GQA_REP
        halves = []
        for hq in (2 * pair, 2 * pair + 1):
            halves.append(_flash_head(q_head(2 * DIFF_HEADS + hq), kt_ref, DIFF_QK_W + grp * HEAD_DIM,
                                      v_ref, DIFF_V_W, GQA_KV_W, n_kv))
        a, bb = halves
        if grp == 0:
            bb = pltpu.roll(bb, HEAD_DIM, 1)
        else:
            a = pltpu.roll(a, HEAD_DIM, 1)
        slab = jnp.where(lane < HEAD_DIM, a, bb)
        c0 = DIFF_V_W + pair * 2 * HEAD_DIM
        o_ref[0, :, c0:c0 + 2 * HEAD_DIM] = slab.astype(o_ref.dtype)


def _attention(q, kt, v, lam_vec, subln_g, lam_init):
    b, t, _ = q.shape
    nq = t // TOK_TILE
    return pl.pallas_call(
        functools.partial(_attn_body, lam_init),
        grid=(b, nq),
        in_specs=[pl.BlockSpec((1, TOK_TILE, Q_W), lambda i, j: (i, j, 0)),
                  pl.BlockSpec((1,) + kt.shape[1:], lambda i, j: (i, 0, 0, 0)),
                  pl.BlockSpec((1, t, V_W), lambda i, j: (i, 0, 0)),
                  _const_spec((4, HEAD_DIM)),
                  _const_spec((1, DIFF_V_DIM))],
        out_specs=pl.BlockSpec((1, TOK_TILE, D_MODEL), lambda i, j: (i, j, 0)),
        out_shape=jax.ShapeDtypeStruct((b, t, D_MODEL), BF16),
        compiler_params=_params(("parallel", "parallel")),
        name="diff_gqa_attention",
    )(q, kt, v, lam_vec, subln_g)


def _post_body(gdn, *refs):
    if gdn:
        (x_ref, of_ref, ob_ref, sz_ref, ng_ref, mod_ref, g_ref, wo_ref, wgu_ref, wd_ref,
         xo_ref, a_ref) = refs
        parts = []
        for hd in range(GDN_HEADS):
            o = of_ref[0, hd].astype(F32) + ob_ref[0, hd].astype(F32)
            parts.append(_rms(o, ng_ref[...]))
        o = (jnp.concatenate(parts, axis=-1) * sz_ref[0].astype(F32)).astype(BF16)
    else:
        x_ref, o_ref, mod_ref, g_ref, wo_ref, wgu_ref, wd_ref, xo_ref, a_ref = refs
        o = o_ref[0]
    x = x_ref[0]
    mod = mod_ref[0, 0]
    g = g_ref[...]
    y = jnp.dot(o, wo_ref[...], preferred_element_type=F32)
    x = x + mod[2:3] * _rms(y, g[1:2])
    h = (_rms(x, g[2:3]) * (1.0 + mod[4:5]) + mod[3:4]).astype(BF16)
    for c in range(FFN_HIDDEN // FFN_CHUNK):
        c0 = c * FFN_CHUNK
        gate = jnp.dot(h, wgu_ref[:, c0:c0 + FFN_CHUNK], preferred_element_type=F32)
        up = jnp.dot(h, wgu_ref[:, FFN_HIDDEN + c0:FFN_HIDDEN + c0 + FFN_CHUNK],
                     preferred_element_type=F32)
        a_ref[:, c0:c0 + FFN_CHUNK] = (_silu(gate) * up).astype(BF16)
    ff = jnp.dot(a_ref[...], wd_ref[...], preferred_element_type=F32)
    xo_ref[0] = x + mod[5:6] * _rms(ff, g[3:4])


def _post_ffn(xa, mixer_out, modl, g, wo, wgu, wd, gdn):
    b, t, d = xa.shape
    nt = t // TOK_TILE
    tile = pl.BlockSpec((1, TOK_TILE, d), lambda i, j: (i, j, 0))
    if gdn:
        of, ob, sz, ng = mixer_out
        head_tile = pl.BlockSpec((1, GDN_HEADS, TOK_TILE, GDN_HEAD_DIM), lambda i, j: (i, 0, j, 0))
        mix_specs = [head_tile, head_tile, tile, _const_spec((1, GDN_HEAD_DIM))]
        mix_args = (of, ob, sz, ng)
    else:
        mix_specs = [tile]
        mix_args = (mixer_out,)
    return pl.pallas_call(
        functools.partial(_post_body, gdn),
        grid=(b, nt),
        in_specs=[tile] + mix_specs + [
            pl.BlockSpec((1, 1, 6, d), lambda i, j: (i, jnp.minimum(j, 1), 0, 0)),
            _const_spec((4, d)),
            _const_spec(wo.shape), _const_spec(wgu.shape), _const_spec(wd.shape)],
        out_specs=tile,
        out_shape=jax.ShapeDtypeStruct((b, t, d), F32),
        scratch_shapes=[pltpu.VMEM((TOK_TILE, FFN_HIDDEN), BF16)],
        compiler_params=_params(("parallel", "parallel")),
        name="out_proj_ffn_gdn" if gdn else "out_proj_ffn_attn",
    )(xa, *mix_args, modl, g, wo, wgu, wd)


def _gdn_proj_body(xp_ref, x_ref, xn_ref, mod_ref, g_ref, wqkv_ref, wz_ref, wab_ref, wabt_ref,
                   conv_ref, alog_ref, dtb_ref, alogt_ref, dtbt_ref,
                   q_out, k_out, v_out, sz_out, gb_out, gbt_out):
    j = pl.program_id(1)
    nt = pl.num_programs(1)
    mod = mod_ref[0, 0]
    g = g_ref[...]

    def prep(xv):
        return _rms(xv, g) * (1.0 + mod[1:2]) + mod[0:1]

    prev_ok = jnp.where(j >= 2, 1.0, 0.0)
    next_ok = jnp.where(jnp.logical_and(j >= 1, j < nt - 1), 1.0, 0.0)
    h_f32 = prep(x_ref[0])
    h_main = h_f32.astype(BF16)
    h_cat = jnp.concatenate([prep(xp_ref[0]) * prev_ok, h_f32, prep(xn_ref[0]) * next_ok],
                            axis=0).astype(BF16)
    p = jnp.dot(h_cat, wqkv_ref[...], preferred_element_type=F32)
    cw = conv_ref[...]
    acc = p[CONV_HALO - GDN_CONV_LEFT:CONV_HALO - GDN_CONV_LEFT + TOK_TILE] * cw[0:1]
    for tap in range(1, GDN_CONV_K):
        r0 = CONV_HALO - GDN_CONV_LEFT + tap
        acc = acc + p[r0:r0 + TOK_TILE] * cw[tap:tap + 1]
    qkv = _silu(acc)
    for hd in range(GDN_HEADS):
        c0 = hd * GDN_HEAD_DIM
        qh = qkv[:, c0:c0 + GDN_HEAD_DIM]
        kh = qkv[:, GDN_W + c0:GDN_W + c0 + GDN_HEAD_DIM]
        qn = qh * lax.rsqrt(jnp.sum(qh * qh, axis=-1, keepdims=True) + L2_EPS) * (GDN_HEAD_DIM ** -0.5)
        kn = kh * lax.rsqrt(jnp.sum(kh * kh, axis=-1, keepdims=True) + L2_EPS)
        q_out[0, hd] = qn.astype(BF16)
        k_out[0, hd] = kn.astype(BF16)
        v_out[0, hd] = qkv[:, 2 * GDN_W + c0:2 * GDN_W + c0 + GDN_HEAD_DIM].astype(BF16)
    sz_out[0] = _silu(jnp.dot(h_main, wz_ref[...], preferred_element_type=F32)).astype(BF16)

    nh2 = 2 * GDN_HEADS
    ab = jnp.dot(h_main, wab_ref[...], preferred_element_type=F32)
    gdec = -jnp.exp(alog_ref[...]) * jax.nn.softplus(ab[:, :nh2] + dtb_ref[...])
    gb_out[0] = jnp.concatenate([gdec, jax.nn.sigmoid(ab[:, nh2:])], axis=-1)
    abt = lax.dot_general(wabt_ref[...], h_main, _NT, preferred_element_type=F32)
    gdec_t = -jnp.exp(alogt_ref[...]) * jax.nn.softplus(abt[:nh2] + dtbt_ref[...])
    gbt_out[0] = jnp.concatenate([gdec_t, jax.nn.sigmoid(abt[nh2:])], axis=0)


def _gdn_proj(xa, modl, g0, wqkv, wz, wab, wab_t, conv_w, a_log, dt_bias):
    b, t, d = xa.shape
    nt = t // TOK_TILE
    per = TOK_TILE // CONV_HALO
    last = t // CONV_HALO - 1
    nh2 = 2 * GDN_HEADS
    head_out = pl.BlockSpec((1, GDN_HEADS, TOK_TILE, GDN_HEAD_DIM), lambda i, j: (i, 0, j, 0))
    head_shape = jax.ShapeDtypeStruct((b, GDN_HEADS, t, GDN_HEAD_DIM), BF16)
    return pl.pallas_call(
        _gdn_proj_body,
        grid=(b, nt),
        in_specs=[pl.BlockSpec((1, CONV_HALO, d), lambda i, j: (i, jnp.maximum(j * per - 1, 0), 0)),
                  pl.BlockSpec((1, TOK_TILE, d), lambda i, j: (i, j, 0)),
                  pl.BlockSpec((1, CONV_HALO, d), lambda i, j: (i, jnp.minimum((j + 1) * per, last), 0)),
                  pl.BlockSpec((1, 1, 6, d), lambda i, j: (i, jnp.minimum(j, 1), 0, 0)),
                  _const_spec((1, d)),
                  _const_spec(wqkv.shape), _const_spec(wz.shape), _const_spec(wab.shape),
                  _const_spec(wab_t.shape), _const_spec(conv_w.shape),
                  _const_spec((1, nh2)), _const_spec((1, nh2)),
                  _const_spec((nh2, 1)), _const_spec((nh2, 1))],
        out_specs=[head_out, head_out, head_out,
                   pl.BlockSpec((1, TOK_TILE, d), lambda i, j: (i, j, 0)),
                   pl.BlockSpec((1, TOK_TILE, 2 * nh2), lambda i, j: (i, j, 0)),
                   pl.BlockSpec((1, 2 * nh2, TOK_TILE), lambda i, j: (i, 0, j))],
        out_shape=[head_shape, head_shape, head_shape,
                   jax.ShapeDtypeStruct((b, t, d), BF16),
                   jax.ShapeDtypeStruct((b, t, 2 * nh2), F32),
                   jax.ShapeDtypeStruct((b, 2 * nh2, t), F32)],
        compiler_params=_params(("parallel", "parallel")),
        name="gdn_in_proj",
    )(xa, xa, xa, modl, g0, wqkv, wz, wab, wab_t, conv_w,
      a_log.reshape(1, nh2), dt_bias.reshape(1, nh2), a_log.reshape(nh2, 1), dt_bias.reshape(nh2, 1))


def _unit_tri_inverse(m, lower):
    n = m.shape[0]
    row = lax.broadcasted_iota(jnp.int32, (n, n), 0)
    col = lax.broadcasted_iota(jnp.int32, (n, n), 1)
    hi, lo = (row, col) if lower else (col, row)
    t_off = None
    k = 1
    while k < n:
        join = ((row ^ col) < 2 * k) & ((hi & k) != 0) & ((lo & k) == 0)
        a = jnp.where(join, m, 0.0)
        if t_off is None:
            t_off = -a
        else:
            tb = t_off.astype(BF16)
            z = a + jnp.dot(tb, a.astype(BF16), preferred_element_type=F32)
            t_off = t_off - z - jnp.dot(z.astype(BF16), tb, preferred_element_type=F32)
        k *= 2
    return t_off + jnp.where(row == col, 1.0, 0.0)


def _gdn_chain(direction, hd, q_ref, k_ref, v_ref, gb, gc, gr, s_ref, o_ref):
    c = GDN_CHUNK
    row = lax.broadcasted_iota(jnp.int32, (c, c), 0)
    col = lax.broadcasted_iota(jnp.int32, (c, c), 1)
    incl = (row >= col) if direction == 0 else (row <= col)
    strict = (row > col) if direction == 0 else (row < col)
    ch = direction * GDN_HEADS + hd
    k = k_ref[0, hd]
    q = q_ref[0, hd]
    v = v_ref[0, hd]
    beta = gb[:, 2 * GDN_HEADS + ch:2 * GDN_HEADS + ch + 1]
    gcol = gc[:, ch:ch + 1]
    grow = gr[ch:ch + 1, :]
    tot = gcol[c - 1:c] if direction == 0 else gcol[0:1]
    decay = jnp.where(incl, jnp.exp(jnp.where(incl, gcol - grow, 0.0)), 0.0)
    gram = lax.dot_general(jnp.concatenate([k, q], axis=0), k, _NT, preferred_element_type=F32)
    m = jnp.where(strict, beta * gram[:c] * decay, 0.0)
    attn = gram[c:] * decay
    tmat = _unit_tri_inverse(m, direction == 0)
    kf = k.astype(F32)
    egc = jnp.exp(gcol)
    lhs = jnp.concatenate([(kf * (beta * egc)).astype(BF16), (q.astype(F32) * egc).astype(BF16)], axis=0)
    state = s_ref[ch]
    ks = jnp.dot(lhs, state.astype(BF16), preferred_element_type=F32)
    resid = v.astype(F32) * beta - ks[:c]
    v_new = jnp.dot(tmat.astype(BF16), resid.astype(BF16), preferred_element_type=F32)
    v_new_b = v_new.astype(BF16)
    o = ks[c:] + jnp.dot(attn.astype(BF16), v_new_b, preferred_element_type=F32)
    o_ref[0, hd] = o.astype(o_ref.dtype)
    kdec = (kf * jnp.exp(tot - gcol)).astype(BF16)
    s_ref[ch] = state * jnp.exp(tot) + lax.dot_general(kdec, v_new_b, _TN, preferred_element_type=F32)


def _gdn_scan_body(qf, kf, vf, qb, kb, vb, gbf, gbb, gtf, gtb, of, ob, s_ref):
    @pl.when(pl.program_id(1) == 0)
    def _():
        s_ref[...] = jnp.zeros_like(s_ref)

    c = GDN_CHUNK
    row = lax.broadcasted_iota(jnp.int32, (c, c), 0)
    col = lax.broadcasted_iota(jnp.int32, (c, c), 1)
    lower = jnp.where(row >= col, 1.0, 0.0)
    upper = jnp.where(row <= col, 1.0, 0.0)
    nh2 = 2 * GDN_HEADS
    hi = lax.Precision.HIGHEST
    for direction, (q_ref, k_ref, v_ref, gb_ref, gt_ref, o_ref) in enumerate(
            ((qf, kf, vf, gbf, gtf, of), (qb, kb, vb, gbb, gtb, ob))):
        gb = gb_ref[0]
        gt = gt_ref[0, 0]
        tri_c, tri_r = (lower, upper) if direction == 0 else (upper, lower)
        gc = jnp.dot(tri_c, gb[:, :nh2], precision=hi, preferred_element_type=F32)
        gr = jnp.dot(gt[:nh2], tri_r, precision=hi, preferred_element_type=F32)
        for hd in range(GDN_HEADS):
            _gdn_chain(direction, hd, q_ref, k_ref, v_ref, gb, gc, gr, s_ref, o_ref)


def _gdn_scan(q, k, v, gb, gbt4):
    b, nh, t, dh = q.shape
    nchunk = t // GDN_CHUNK
    nctx = CTX_LEN // GDN_CHUNK
    nh2 = 2 * GDN_HEADS

    def bwd(s):
        return jnp.where(s < nctx, nctx - 1 - s, nchunk - 1 + nctx - s)

    head_f = pl.BlockSpec((1, nh, GDN_CHUNK, dh), lambda i, s: (i, 0, s, 0))
    head_b = pl.BlockSpec((1, nh, GDN_CHUNK, dh), lambda i, s: (i, 0, bwd(s), 0))
    out_shape = jax.ShapeDtypeStruct((b, nh, t, dh), BF16)
    return pl.pallas_call(
        _gdn_scan_body,
        grid=(b, nchunk),
        in_specs=[head_f, head_f, head_f, head_b, head_b, head_b,
                  pl.BlockSpec((1, GDN_CHUNK, 2 * nh2), lambda i, s: (i, s, 0)),
                  pl.BlockSpec((1, GDN_CHUNK, 2 * nh2), lambda i, s: (i, bwd(s), 0)),
                  pl.BlockSpec((1, 1, 2 * nh2, GDN_CHUNK), lambda i, s: (i, s, 0, 0)),
                  pl.BlockSpec((1, 1, 2 * nh2, GDN_CHUNK), lambda i, s: (i, bwd(s), 0, 0))],
        out_specs=[head_f, head_b],
        out_shape=[out_shape, out_shape],
        scratch_shapes=[pltpu.VMEM((nh2, dh, dh), F32)],
        compiler_params=_params(("arbitrary", "arbitrary")),
        name="gdn_chunk_scan",
    )(q, k, v, q, k, v, gb, gb, gbt4, gbt4)


def _rope_tables(n_lat):
    rows = n_lat // GRID_W
    row_ids = jnp.repeat(jnp.arange(rows, dtype=F32), GRID_W)[:n_lat]
    col_ids = jnp.tile(jnp.arange(GRID_W, dtype=F32), rows)[:n_lat]
    axis_dim = HEAD_DIM // 2
    inv_freq = ROPE_THETA ** (-jnp.arange(0, axis_dim, 2, dtype=F32) / axis_dim)
    ang_r = row_ids[:, None] * inv_freq
    ang_c = col_ids[:, None] * inv_freq
    ang = jnp.concatenate([ang_r, ang_r, ang_c, ang_c], axis=-1)
    cos = jnp.concatenate([jnp.ones((CTX_LEN, HEAD_DIM), F32), jnp.cos(ang)], axis=0)
    sin = jnp.concatenate([jnp.zeros((CTX_LEN, HEAD_DIM), F32), jnp.sin(ang)], axis=0)
    sign = jnp.tile(jnp.repeat(jnp.array([-1.0, 1.0], F32), HEAD_DIM // 4), 2)
    return cos.T, (sin * sign).T


def kernel(x, c, ctx, c_ctx, ada_w, ada_b, norm_g, attn_w_in, attn_w_out, diff_lambda, diff_subln_g,
           gqa_qk_g, gdn_w_in, gdn_conv_w, gdn_a_log, gdn_dt_bias, gdn_norm_g, gdn_w_out,
           ffn_w_gate_up, ffn_w_down):
    b, n_lat, d = x.shape
    depth = ada_w.shape[0]
    assert d == D_MODEL and ctx.shape[1] == CTX_LEN and n_lat % TOK_TILE == 0
    t = CTX_LEN + n_lat
    xa = jnp.concatenate([ctx, x], axis=1)

    rows = -(-(b + 1) // 8) * 8
    cc = jnp.concatenate([c, c_ctx[None], jnp.zeros((rows - b - 1, d), F32)], axis=0)
    mods = _modulation(cc, ada_w, ada_b)
    cos_t, sin_t = _rope_tables(n_lat)
    nh2 = 2 * GDN_HEADS

    for l in range(depth):
        i = l // 2
        ml = mods[l]
        modl = jnp.stack([jnp.broadcast_to(ml[b].reshape(1, 6, d), (b, 6, d)),
                          ml[:b].reshape(b, 6, d)], axis=1)
        g = norm_g[l]
        wgu = ffn_w_gate_up[l].astype(BF16)
        wd = ffn_w_down[l].astype(BF16)
        if l % 2 == 0:
            lam_init = 0.8 - 0.6 * math.exp(-0.3 * l)
            w = attn_w_in[i]
            o_gq = 2 * DIFF_QK_W + DIFF_V_W
            wqk_t = jnp.concatenate([w[:, :DIFF_QK_W], w[:, o_gq:o_gq + GQA_Q_W],
                                     w[:, DIFF_QK_W:2 * DIFF_QK_W],
                                     w[:, o_gq + GQA_Q_W:o_gq + GQA_Q_W + GQA_KV_W]],
                                    axis=1).T.astype(BF16)
            wv = jnp.concatenate([w[:, 2 * DIFF_QK_W:2 * DIFF_QK_W + DIFF_V_W],
                                  w[:, -GQA_KV_W:]], axis=1).astype(BF16)
            q, kt, v = _attn_proj(xa, modl, g[0:1], wqk_t, wv, cos_t, sin_t,
                                  gqa_qk_g[i].reshape(2, HEAD_DIM, 1))
            o = _attention(q, kt, v, diff_lambda[i], diff_subln_g[i].reshape(1, DIFF_V_DIM), lam_init)
            xa = _post_ffn(xa, o, modl, g, attn_w_out[i].astype(BF16), wgu, wd, gdn=False)
        else:
            w = gdn_w_in[i]
            wab = w[:, 4 * GDN_W:]
            q, k, v, sz, gb, gbt = _gdn_proj(
                xa, modl, g[0:1], w[:, :3 * GDN_W].astype(BF16), w[:, 3 * GDN_W:4 * GDN_W].astype(BF16),
                wab.astype(BF16), wab.T.astype(BF16), gdn_conv_w[i], gdn_a_log[i], gdn_dt_bias[i])
            gbt4 = gbt.reshape(b, 2 * nh2, t // GDN_CHUNK, GDN_CHUNK).transpose(0, 2, 1, 3)
            of, ob = _gdn_scan(q, k, v, gb, gbt4)
            xa = _post_ffn(xa, (of, ob, sz, gdn_norm_g[i].reshape(1, GDN_HEAD_DIM)), modl, g,
                           gdn_w_out[i].astype(BF16), wgu, wd, gdn=True)
    return xa[:, CTX_LEN:]
```

```python
import functools
import math

import jax
import jax.numpy as jnp
from jax import lax
from jax.experimental import pallas as pl
from jax.experimental.pallas import tpu as pltpu

F32 = jnp.float32
BF16 = jnp.bfloat16

D_MODEL = 1024
CTX_LEN = 256
GRID_W = 64
RMS_EPS = 1e-6
L2_EPS = 1e-6
ROPE_THETA = 10000.0

HEAD_DIM = 64
ATTN_SCALE = HEAD_DIM ** -0.5
DIFF_HEADS = 4
DIFF_V_DIM = 2 * HEAD_DIM
GQA_Q_HEADS = 8
GQA_KV_HEADS = 2
GQA_REP = GQA_Q_HEADS // GQA_KV_HEADS
DIFF_QK_W = DIFF_HEADS * 2 * HEAD_DIM
DIFF_V_W = DIFF_HEADS * DIFF_V_DIM
GQA_Q_W = GQA_Q_HEADS * HEAD_DIM
GQA_KV_W = GQA_KV_HEADS * HEAD_DIM
Q_W = DIFF_QK_W + GQA_Q_W
K_W = DIFF_QK_W + GQA_KV_W
V_W = DIFF_V_W + GQA_KV_W
N_MAPS = 2 * DIFF_HEADS + GQA_Q_HEADS
QK_AHEAD = 4

GDN_HEADS = 8
GDN_HEAD_DIM = 128
GDN_W = GDN_HEADS * GDN_HEAD_DIM
GDN_CONV_K = 4
GDN_CONV_LEFT = 2
GDN_CHUNK = 64
CONV_HALO = 8

FFN_HIDDEN = 2816
FFN_CHUNK = 256

TOK_TILE = 256
KV_CHUNK = 256
VMEM_LIMIT = 56 * 1024 * 1024

_NT = (((1,), (1,)), ((), ()))
_TN = (((0,), (0,)), ((), ()))


def _rms(x, g):
    return x * lax.rsqrt(jnp.mean(x * x, axis=-1, keepdims=True) + RMS_EPS) * g


def _silu(x):
    return x * jax.nn.sigmoid(x)


def _params(sem):
    return pltpu.CompilerParams(dimension_semantics=sem, vmem_limit_bytes=VMEM_LIMIT)


def _const_spec(shape):
    n = len(shape)
    return pl.BlockSpec(shape, lambda *_: (0,) * n)


def _mod_body(c_ref, w_ref, b_ref, o_ref):
    sc = _silu(c_ref[...])
    o_ref[0] = jnp.dot(sc, w_ref[0], precision=lax.Precision.HIGHEST,
                       preferred_element_type=F32) + b_ref[0]


def _modulation(cc, ada_w, ada_b):
    depth, d, w6 = ada_w.shape
    rows = cc.shape[0]
    nblk = w6 // d
    return pl.pallas_call(
        _mod_body,
        grid=(depth, nblk),
        in_specs=[pl.BlockSpec((rows, d), lambda l, j: (0, 0)),
                  pl.BlockSpec((1, d, d), lambda l, j: (l, 0, j)),
                  pl.BlockSpec((1, 1, d), lambda l, j: (l, 0, j))],
        out_specs=pl.BlockSpec((1, rows, d), lambda l, j: (l, 0, j)),
        out_shape=jax.ShapeDtypeStruct((depth, rows, w6), F32),
        compiler_params=_params(("parallel", "parallel")),
        name="adaln_mod",
    )(cc, ada_w, ada_b.reshape(depth, 1, w6))


def _attn_proj_body(x_ref, mod_ref, g_ref, w_ref, cos_ref, sin_ref, qkg_ref, qt_out, k_out, vt_out):
    x = x_ref[0]
    mod = mod_ref[0, 0]
    h = (_rms(x, g_ref[...]) * (1.0 + mod[1:2]) + mod[0:1]).astype(BF16)
    p = lax.dot_general(w_ref[...], h, _NT, preferred_element_type=F32)
    vt_out[0, 0] = p[Q_W + K_W:].astype(BF16)
    cos = cos_ref[...]
    sin = sin_ref[...]

    def rope(xh):
        swapped = jnp.concatenate([xh[16:32], xh[0:16], xh[48:64], xh[32:48]], axis=0)
        return xh * cos + swapped * sin

    def norm(xh, g):
        r = lax.rsqrt(jnp.mean(xh * xh, axis=0, keepdims=True) + RMS_EPS)
        return xh * r * g

    gq_g = qkg_ref[0]
    gk_g = qkg_ref[1]
    for j in range(Q_W // HEAD_DIM):
        r0 = j * HEAD_DIM
        xh = p[r0:r0 + HEAD_DIM]
        if r0 >= DIFF_QK_W:
            xh = norm(xh, gq_g)
        qt_out[0, 0, r0:r0 + HEAD_DIM, :] = (rope(xh) * ATTN_SCALE).astype(BF16)
    k_heads = []
    for j in range(K_W // HEAD_DIM):
        r0 = j * HEAD_DIM
        xh = p[Q_W + r0:Q_W + r0 + HEAD_DIM]
        if r0 >= DIFF_QK_W:
            xh = norm(xh, gk_g)
        k_heads.append(rope(xh))
    k_out[0] = jnp.concatenate(k_heads, axis=0).T.astype(BF16)


def _attn_proj(xa, modl, g0, w_t, cos_t, sin_t, qkg):
    b, t, d = xa.shape
    nt = t // TOK_TILE
    return pl.pallas_call(
        _attn_proj_body,
        grid=(b, nt),
        in_specs=[pl.BlockSpec((1, TOK_TILE, d), lambda i, j: (i, j, 0)),
                  pl.BlockSpec((1, 1, 6, d), lambda i, j: (i, jnp.minimum(j, 1), 0, 0)),
                  _const_spec((1, d)),
                  _const_spec(w_t.shape),
                  pl.BlockSpec((HEAD_DIM, TOK_TILE), lambda i, j: (0, j)),
                  pl.BlockSpec((HEAD_DIM, TOK_TILE), lambda i, j: (0, j)),
                  _const_spec((2, HEAD_DIM, 1))],
        out_specs=[pl.BlockSpec((1, 1, Q_W, TOK_TILE), lambda i, j: (i, j, 0, 0)),
                   pl.BlockSpec((1, TOK_TILE, K_W), lambda i, j: (i, j, 0)),
                   pl.BlockSpec((1, 1, V_W, TOK_TILE), lambda i, j: (i, j, 0, 0))],
        out_shape=[jax.ShapeDtypeStruct((b, nt, Q_W, TOK_TILE), BF16),
                   jax.ShapeDtypeStruct((b, t, K_W), BF16),
                   jax.ShapeDtypeStruct((b, nt, V_W, TOK_TILE), BF16)],
        compiler_params=_params(("parallel", "parallel")),
        name="attn_in_proj",
    )(xa, modl, g0, w_t, cos_t, sin_t, qkg)


def _attn_body(lam_init, qt_ref, k_ref, vt_ref, lam_ref, subln_ref, o_ref, qpad_ref, m_ref, l_ref, acc_ref):
    qi = pl.program_id(1)
    n_kv = jnp.where(qi == 0, 1, vt_ref.shape[1])
    slab = 2 * HEAD_DIM
    maps = []
    for hd in range(DIFF_HEADS):
        for mm in range(2):
            maps.append((2 * hd + mm, hd, hd * DIFF_V_DIM, DIFF_V_DIM))
    for hq in range(GQA_Q_HEADS):
        grp = hq // GQA_REP
        maps.append((2 * DIFF_HEADS + hq, DIFF_HEADS, DIFF_V_W + grp * HEAD_DIM, HEAD_DIM))

    zeros = jnp.zeros((HEAD_DIM, qt_ref.shape[3]), BF16)
    for i, (qh, _, _, _) in enumerate(maps):
        half = (qh % 2) if qh < 2 * DIFF_HEADS else (qh - 2 * DIFF_HEADS) // GQA_REP
        qh_t = qt_ref[0, 0, qh * HEAD_DIM:(qh + 1) * HEAD_DIM, :]
        qpad_ref[i] = jnp.concatenate([qh_t, zeros] if half == 0 else [zeros, qh_t], axis=0)
    m_ref[...] = jnp.full(m_ref.shape, -jnp.inf, F32)
    l_ref[...] = jnp.zeros(l_ref.shape, F32)
    acc_ref[...] = jnp.zeros(acc_ref.shape, F32)

    def step(c, carry):
        c0 = pl.multiple_of(c * KV_CHUNK, KV_CHUNK)

        def scores(i):
            ks = maps[i][1]
            return jnp.dot(k_ref[0, pl.ds(c0, KV_CHUNK), ks * slab:(ks + 1) * slab], qpad_ref[i],
                           preferred_element_type=F32)

        pending = [scores(i) for i in range(QK_AHEAD)]
        for i in range(N_MAPS):
            s = pending.pop(0)
            if i + QK_AHEAD < N_MAPS:
                pending.append(scores(i + QK_AHEAD))
            _, _, v_row0, v_w = maps[i]
            m_old = m_ref[i]
            m_new = jnp.maximum(m_old, jnp.max(s, axis=0, keepdims=True))
            alpha = jnp.exp(m_old - m_new)
            p = jnp.exp(s - m_new)
            l_ref[i] = alpha * l_ref[i] + jnp.sum(p, axis=0, keepdims=True)
            m_ref[i] = m_new
            pv = jnp.dot(vt_ref[0, c, v_row0:v_row0 + v_w, :], p.astype(BF16), preferred_element_type=F32)
            acc_ref[i, 0:v_w, :] = alpha * acc_ref[i, 0:v_w, :] + pv
        return carry

    lax.fori_loop(0, n_kv, step, 0)

    lv = lam_ref[...]
    lam = (jnp.exp(jnp.sum(lv[0:1] * lv[1:2], axis=-1, keepdims=True))
           - jnp.exp(jnp.sum(lv[2:3] * lv[3:4], axis=-1, keepdims=True)) + lam_init)
    for hd in range(DIFF_HEADS):
        o1 = acc_ref[2 * hd] / l_ref[2 * hd]
        o2 = acc_ref[2 * hd + 1] / l_ref[2 * hd + 1]
        od = o1 - lam * o2
        od = od * lax.rsqrt(jnp.mean(od * od, axis=0, keepdims=True) + RMS_EPS) * subln_ref[...]
        o_ref[0, :, hd * DIFF_V_DIM:(hd + 1) * DIFF_V_DIM] = (od * (1.0 - lam_init)).T.astype(o_ref.dtype)
    for pair in range(GQA_Q_HEADS // 2):
        i0 = 2 * DIFF_HEADS + 2 * pair
        og = jnp.concatenate([acc_ref[i0, 0:HEAD_DIM, :] / l_ref[i0],
                              acc_ref[i0 + 1, 0:HEAD_DIM, :] / l_ref[i0 + 1]], axis=0)
        c0 = DIFF_V_W + pair * slab
        o_ref[0, :, c0:c0 + slab] = og.T.astype(o_ref.dtype)


def _attention(qt, k, vt, lam_vec, subln_g, lam_init):
    b, t, _ = k.shape
    nq = t // TOK_TILE
    return pl.pallas_call(
        functools.partial(_attn_body, lam_init),
        grid=(b, nq),
        in_specs=[pl.BlockSpec((1, 1, Q_W, TOK_TILE), lambda i, j: (i, j, 0, 0)),
                  pl.BlockSpec((1, t, K_W), lambda i, j: (i, 0, 0)),
                  pl.BlockSpec((1,) + vt.shape[1:], lambda i, j: (i, 0, 0, 0)),
                  _const_spec((4, HEAD_DIM)),
                  _const_spec((DIFF_V_DIM, 1))],
        out_specs=pl.BlockSpec((1, TOK_TILE, D_MODEL), lambda i, j: (i, j, 0)),
        out_shape=jax.ShapeDtypeStruct((b, t, D_MODEL), BF16),
        scratch_shapes=[pltpu.VMEM((N_MAPS, 2 * HEAD_DIM, TOK_TILE), BF16),
                        pltpu.VMEM((N_MAPS, 1, TOK_TILE), F32), pltpu.VMEM((N_MAPS, 1, TOK_TILE), F32),
                        pltpu.VMEM((N_MAPS, DIFF_V_DIM, TOK_TILE), F32)],
        compiler_params=_params(("parallel", "parallel")),
        name="diff_gqa_attention",
    )(qt, k, vt, lam_vec, subln_g)


def _post_body(gdn, *refs):
    if gdn:
        (x_ref, of_ref, ob_ref, sz_ref, ng_ref, mod_ref, g_ref, wo_ref, wgu_ref, wd_ref,
         xo_ref, a_ref) = refs
        parts = []
        for hd in range(GDN_HEADS):
            o = of_ref[0, hd].astype(F32) + ob_ref[0, hd].astype(F32)
            parts.append(_rms(o, ng_ref[...]))
        o = (jnp.concatenate(parts, axis=-1) * sz_ref[0].astype(F32)).astype(BF16)
    else:
        x_ref, o_ref, mod_ref, g_ref, wo_ref, wgu_ref, wd_ref, xo_ref, a_ref = refs
        o = o_ref[0]
    x = x_ref[0]
    mod = mod_ref[0, 0]
    g = g_ref[...]
    y = jnp.dot(o, wo_ref[...], preferred_element_type=F32)
    x = x + mod[2:3] * _rms(y, g[1:2])
    h = (_rms(x, g[2:3]) * (1.0 + mod[4:5]) + mod[3:4]).astype(BF16)
    for c in range(FFN_HIDDEN // FFN_CHUNK):
        c0 = c * FFN_CHUNK
        gate = jnp.dot(h, wgu_ref[:, c0:c0 + FFN_CHUNK], preferred_element_type=F32)
        up = jnp.dot(h, wgu_ref[:, FFN_HIDDEN + c0:FFN_HIDDEN + c0 + FFN_CHUNK],
                     preferred_element_type=F32)
        a_ref[:, c0:c0 + FFN_CHUNK] = (_silu(gate) * up).astype(BF16)
    ff = jnp.dot(a_ref[...], wd_ref[...], preferred_element_type=F32)
    xo_ref[0] = x + mod[5:6] * _rms(ff, g[3:4])


def _post_ffn(xa, mixer_out, modl, g, wo, wgu, wd, gdn):
    b, t, d = xa.shape
    nt = t // TOK_TILE
    tile = pl.BlockSpec((1, TOK_TILE, d), lambda i, j: (i, j, 0))
    if gdn:
        of, ob, sz, ng = mixer_out
        head_tile = pl.BlockSpec((1, GDN_HEADS, TOK_TILE, GDN_HEAD_DIM), lambda i, j: (i, 0, j, 0))
        mix_specs = [head_tile, head_tile, tile, _const_spec((1, GDN_HEAD_DIM))]
        mix_args = (of, ob, sz, ng)
    else:
        mix_specs = [tile]
        mix_args = (mixer_out,)
    return pl.pallas_call(
        functools.partial(_post_body, gdn),
        grid=(b, nt),
        in_specs=[tile] + mix_specs + [
            pl.BlockSpec((1, 1, 6, d), lambda i, j: (i, jnp.minimum(j, 1), 0, 0)),
            _const_spec((4, d)),
            _const_spec(wo.shape), _const_spec(wgu.shape), _const_spec(wd.shape)],
        out_specs=tile,
        out_shape=jax.ShapeDtypeStruct((b, t, d), F32),
        scratch_shapes=[pltpu.VMEM((TOK_TILE, FFN_HIDDEN), BF16)],
        compiler_params=_params(("parallel", "parallel")),
        name="out_proj_ffn_gdn" if gdn else "out_proj_ffn_attn",
    )(xa, *mix_args, modl, g, wo, wgu, wd)


def _gdn_proj_body(xp_ref, x_ref, xn_ref, mod_ref, g_ref, wqkv_ref, wz_ref, wab_ref, wabt_ref,
                   conv_ref, alog_ref, dtb_ref, alogt_ref, dtbt_ref,
                   q_out, k_out, v_out, sz_out, gb_out, gbt_out):
    j = pl.program_id(1)
    nt = pl.num_programs(1)
    mod = mod_ref[0, 0]
    g = g_ref[...]

    def prep(xv):
        return _rms(xv, g) * (1.0 + mod[1:2]) + mod[0:1]

    prev_ok = jnp.where(j >= 2, 1.0, 0.0)
    next_ok = jnp.where(jnp.logical_and(j >= 1, j < nt - 1), 1.0, 0.0)
    h_f32 = prep(x_ref[0])
    h_main = h_f32.astype(BF16)
    h_cat = jnp.concatenate([prep(xp_ref[0]) * prev_ok, h_f32, prep(xn_ref[0]) * next_ok],
                            axis=0).astype(BF16)
    p = jnp.dot(h_cat, wqkv_ref[...], preferred_element_type=F32)
    cw = conv_ref[...]
    acc = p[CONV_HALO - GDN_CONV_LEFT:CONV_HALO - GDN_CONV_LEFT + TOK_TILE] * cw[0:1]
    for tap in range(1, GDN_CONV_K):
        r0 = CONV_HALO - GDN_CONV_LEFT + tap
        acc = acc + p[r0:r0 + TOK_TILE] * cw[tap:tap + 1]
    qkv = _silu(acc)
    for hd in range(GDN_HEADS):
        c0 = hd * GDN_HEAD_DIM
        qh = qkv[:, c0:c0 + GDN_HEAD_DIM]
        kh = qkv[:, GDN_W + c0:GDN_W + c0 + GDN_HEAD_DIM]
        qn = qh * lax.rsqrt(jnp.sum(qh * qh, axis=-1, keepdims=True) + L2_EPS) * (GDN_HEAD_DIM ** -0.5)
        kn = kh * lax.rsqrt(jnp.sum(kh * kh, axis=-1, keepdims=True) + L2_EPS)
        q_out[0, hd] = qn.astype(BF16)
        k_out[0, hd] = kn.astype(BF16)
        v_out[0, hd] = qkv[:, 2 * GDN_W + c0:2 * GDN_W + c0 + GDN_HEAD_DIM].astype(BF16)
    sz_out[0] = _silu(jnp.dot(h_main, wz_ref[...], preferred_element_type=F32)).astype(BF16)

    nh2 = 2 * GDN_HEADS
    ab = jnp.dot(h_main, wab_ref[...], preferred_element_type=F32)
    gdec = -jnp.exp(alog_ref[...]) * jax.nn.softplus(ab[:, :nh2] + dtb_ref[...])
    gb_out[0] = jnp.concatenate([gdec, jax.nn.sigmoid(ab[:, nh2:])], axis=-1)
    abt = lax.dot_general(wabt_ref[...], h_main, _NT, preferred_element_type=F32)
    gdec_t = -jnp.exp(alogt_ref[...]) * jax.nn.softplus(abt[:nh2] + dtbt_ref[...])
    gbt_out[0] = jnp.concatenate([gdec_t, jax.nn.sigmoid(abt[nh2:])], axis=0)


def _gdn_proj(xa, modl, g0, wqkv, wz, wab, wab_t, conv_w, a_log, dt_bias):
    b, t, d = xa.shape
    nt = t // TOK_TILE
    per = TOK_TILE // CONV_HALO
    last = t // CONV_HALO - 1
    nh2 = 2 * GDN_HEADS
    head_out = pl.BlockSpec((1, GDN_HEADS, TOK_TILE, GDN_HEAD_DIM), lambda i, j: (i, 0, j, 0))
    head_shape = jax.ShapeDtypeStruct((b, GDN_HEADS, t, GDN_HEAD_DIM), BF16)
    return pl.pallas_call(
        _gdn_proj_body,
        grid=(b, nt),
        in_specs=[pl.BlockSpec((1, CONV_HALO, d), lambda i, j: (i, jnp.maximum(j * per - 1, 0), 0)),
                  pl.BlockSpec((1, TOK_TILE, d), lambda i, j: (i, j, 0)),
                  pl.BlockSpec((1, CONV_HALO, d), lambda i, j: (i, jnp.minimum((j + 1) * per, last), 0)),
                  pl.BlockSpec((1, 1, 6, d), lambda i, j: (i, jnp.minimum(j, 1), 0, 0)),
                  _const_spec((1, d)),
                  _const_spec(wqkv.shape), _const_spec(wz.shape), _const_spec(wab.shape),
                  _const_spec(wab_t.shape), _const_spec(conv_w.shape),
                  _const_spec((1, nh2)), _const_spec((1, nh2)),
                  _const_spec((nh2, 1)), _const_spec((nh2, 1))],
        out_specs=[head_out, head_out, head_out,
                   pl.BlockSpec((1, TOK_TILE, d), lambda i, j: (i, j, 0)),
                   pl.BlockSpec((1, TOK_TILE, 2 * nh2), lambda i, j: (i, j, 0)),
                   pl.BlockSpec((1, 2 * nh2, TOK_TILE), lambda i, j: (i, 0, j))],
        out_shape=[head_shape, head_shape, head_shape,
                   jax.ShapeDtypeStruct((b, t, d), BF16),
                   jax.ShapeDtypeStruct((b, t, 2 * nh2), F32),
                   jax.ShapeDtypeStruct((b, 2 * nh2, t), F32)],
        compiler_params=_params(("parallel", "parallel")),
        name="gdn_in_proj",
    )(xa, xa, xa, modl, g0, wqkv, wz, wab, wab_t, conv_w,
      a_log.reshape(1, nh2), dt_bias.reshape(1, nh2), a_log.reshape(nh2, 1), dt_bias.reshape(nh2, 1))


def _unit_tri_inverses(ms, lowers):
    n = ms[0].shape[0]
    row = lax.broadcasted_iota(jnp.int32, (n, n), 0)
    col = lax.broadcasted_iota(jnp.int32, (n, n), 1)
    t_offs = None
    k = 1
    while k < n:
        same = (row ^ col) < 2 * k
        joins = {True: same & ((row & k) != 0) & ((col & k) == 0),
                 False: same & ((col & k) != 0) & ((row & k) == 0)}
        parts = [jnp.where(joins[lo], m, 0.0) for m, lo in zip(ms, lowers)]
        if t_offs is None:
            t_offs = [-a for a in parts]
        else:
            tbs = [t.astype(BF16) for t in t_offs]
            zs = [a + jnp.dot(tb, a.astype(BF16), preferred_element_type=F32) for a, tb in zip(parts, tbs)]
            t_offs = [t - z - jnp.dot(z.astype(BF16), tb, preferred_element_type=F32)
                      for t, z, tb in zip(t_offs, zs, tbs)]
        k *= 2
    eye = jnp.where(row == col, 1.0, 0.0)
    return [t + eye for t in t_offs]


def _gdn_scan_body(qf, kf, vf, qb, kb, vb, gbf, gbb, gtf, gtb, of, ob, s_ref):
    @pl.when(pl.program_id(1) == 0)
    def _():
        s_ref[...] = jnp.zeros_like(s_ref)

    c = GDN_CHUNK
    row = lax.broadcasted_iota(jnp.int32, (c, c), 0)
    col = lax.broadcasted_iota(jnp.int32, (c, c), 1)
    lower = jnp.where(row >= col, 1.0, 0.0)
    upper = jnp.where(row <= col, 1.0, 0.0)
    nh2 = 2 * GDN_HEADS
    hi = lax.Precision.HIGHEST

    chains = []
    for direction, (q_ref, k_ref, v_ref, gb_ref, gt_ref, o_ref) in enumerate(
            ((qf, kf, vf, gbf, gtf, of), (qb, kb, vb, gbb, gtb, ob))):
        gb = gb_ref[0]
        gt = gt_ref[0, 0]
        tri_c, tri_r = (lower, upper) if direction == 0 else (upper, lower)
        gc = jnp.dot(tri_c, gb[:, :nh2], precision=hi, preferred_element_type=F32)
        gr = jnp.dot(gt[:nh2], tri_r, precision=hi, preferred_element_type=F32)
        incl = (row >= col) if direction == 0 else (row <= col)
        strict = (row > col) if direction == 0 else (row < col)
        for hd in range(GDN_HEADS):
            ch = direction * GDN_HEADS + hd
            gcol = gc[:, ch:ch + 1]
            grow = gr[ch:ch + 1, :]
            chains.append(dict(
                ch=ch, hd=hd, lower=direction == 0, strict=strict, o_ref=o_ref,
                k=k_ref[0, hd], q=q_ref[0, hd], v=v_ref[0, hd],
                beta=gb[:, nh2 + ch:nh2 + ch + 1], gcol=gcol,
                tot=gcol[c - 1:c] if direction == 0 else gcol[0:1],
                decay=jnp.where(incl, jnp.exp(jnp.where(incl, gcol - grow, 0.0)), 0.0)))

    for w in chains:
        w["gram"] = lax.dot_general(jnp.concatenate([w["k"], w["q"]], axis=0), w["k"], _NT,
                                    preferred_element_type=F32)
    ms = [jnp.where(w["strict"], w["beta"] * w["gram"][:c] * w["decay"], 0.0) for w in chains]
    tmats = _unit_tri_inverses(ms, [w["lower"] for w in chains])
    for w in chains:
        kf32 = w["k"].astype(F32)
        egc = jnp.exp(w["gcol"])
        lhs = jnp.concatenate([(kf32 * (w["beta"] * egc)).astype(BF16),
                               (w["q"].astype(F32) * egc).astype(BF16)], axis=0)
        w["state"] = s_ref[w["ch"]]
        w["ks"] = jnp.dot(lhs, w["state"].astype(BF16), preferred_element_type=F32)
        w["kdec"] = (kf32 * jnp.exp(w["tot"] - w["gcol"])).astype(BF16)
    for w, tmat in zip(chains, tmats):
        resid = w["v"].astype(F32) * w["beta"] - w["ks"][:c]
        w["v_new"] = jnp.dot(tmat.astype(BF16), resid.astype(BF16),
                             preferred_element_type=F32).astype(BF16)
    for w in chains:
        attn = (w["gram"][c:] * w["decay"]).astype(BF16)
        o = w["ks"][c:] + jnp.dot(attn, w["v_new"], preferred_element_type=F32)
        w["o_ref"][0, w["hd"]] = o.astype(w["o_ref"].dtype)
    for w in chains:
        s_ref[w["ch"]] = (w["state"] * jnp.exp(w["tot"])
                          + lax.dot_general(w["kdec"], w["v_new"], _TN, preferred_element_type=F32))


def _gdn_scan(q, k, v, gb, gbt4):
    b, nh, t, dh = q.shape
    nchunk = t // GDN_CHUNK
    nctx = CTX_LEN // GDN_CHUNK
    nh2 = 2 * GDN_HEADS

    def bwd(s):
        return jnp.where(s < nctx, nctx - 1 - s, nchunk - 1 + nctx - s)

    head_f = pl.BlockSpec((1, nh, GDN_CHUNK, dh), lambda i, s: (i, 0, s, 0))
    head_b = pl.BlockSpec((1, nh, GDN_CHUNK, dh), lambda i, s: (i, 0, bwd(s), 0))
    out_shape = jax.ShapeDtypeStruct((b, nh, t, dh), BF16)
    return pl.pallas_call(
        _gdn_scan_body,
        grid=(b, nchunk),
        in_specs=[head_f, head_f, head_f, head_b, head_b, head_b,
                  pl.BlockSpec((1, GDN_CHUNK, 2 * nh2), lambda i, s: (i, s, 0)),
                  pl.BlockSpec((1, GDN_CHUNK, 2 * nh2), lambda i, s: (i, bwd(s), 0)),
                  pl.BlockSpec((1, 1, 2 * nh2, GDN_CHUNK), lambda i, s: (i, s, 0, 0)),
                  pl.BlockSpec((1, 1, 2 * nh2, GDN_CHUNK), lambda i, s: (i, bwd(s), 0, 0))],
        out_specs=[head_f, head_b],
        out_shape=[out_shape, out_shape],
        scratch_shapes=[pltpu.VMEM((nh2, dh, dh), F32)],
        compiler_params=_params(("arbitrary", "arbitrary")),
        name="gdn_chunk_scan",
    )(q, k, v, q, k, v, gb, gb, gbt4, gbt4)


def _rope_tables(n_lat):
    rows = n_lat // GRID_W
    row_ids = jnp.repeat(jnp.arange(rows, dtype=F32), GRID_W)[:n_lat]
    col_ids = jnp.tile(jnp.arange(GRID_W, dtype=F32), rows)[:n_lat]
    axis_dim = HEAD_DIM // 2
    inv_freq = ROPE_THETA ** (-jnp.arange(0, axis_dim, 2, dtype=F32) / axis_dim)
    ang_r = row_ids[:, None] * inv_freq
    ang_c = col_ids[:, None] * inv_freq
    ang = jnp.concatenate([ang_r, ang_r, ang_c, ang_c], axis=-1)
    cos = jnp.concatenate([jnp.ones((CTX_LEN, HEAD_DIM), F32), jnp.cos(ang)], axis=0)
    sin = jnp.concatenate([jnp.zeros((CTX_LEN, HEAD_DIM), F32), jnp.sin(ang)], axis=0)
    sign = jnp.tile(jnp.repeat(jnp.array([-1.0, 1.0], F32), HEAD_DIM // 4), 2)
    return cos.T, (sin * sign).T


def kernel(x, c, ctx, c_ctx, ada_w, ada_b, norm_g, attn_w_in, attn_w_out, diff_lambda, diff_subln_g,
           gqa_qk_g, gdn_w_in, gdn_conv_w, gdn_a_log, gdn_dt_bias, gdn_norm_g, gdn_w_out,
           ffn_w_gate_up, ffn_w_down):
    b, n_lat, d = x.shape
    depth = ada_w.shape[0]
    assert d == D_MODEL and ctx.shape[1] == CTX_LEN and n_lat % TOK_TILE == 0
    t = CTX_LEN + n_lat
    xa = jnp.concatenate([ctx, x], axis=1)

    rows = -(-(b + 1) // 8) * 8
    cc = jnp.concatenate([c, c_ctx[None], jnp.zeros((rows - b - 1, d), F32)], axis=0)
    mods = _modulation(cc, ada_w, ada_b)
    cos_t, sin_t = _rope_tables(n_lat)
    nh2 = 2 * GDN_HEADS

    for l in range(depth):
        i = l // 2
        ml = mods[l]
        modl = jnp.stack([jnp.broadcast_to(ml[b].reshape(1, 6, d), (b, 6, d)),
                          ml[:b].reshape(b, 6, d)], axis=1)
        g = norm_g[l]
        wgu = ffn_w_gate_up[l].astype(BF16)
        wd = ffn_w_down[l].astype(BF16)
        if l % 2 == 0:
            lam_init = 0.8 - 0.6 * math.exp(-0.3 * l)
            w = attn_w_in[i]
            o_dv = 2 * DIFF_QK_W
            o_gq = o_dv + DIFF_V_W
            o_gk = o_gq + GQA_Q_W
            o_gv = o_gk + GQA_KV_W
            w_t = jnp.concatenate([w[:, :DIFF_QK_W], w[:, o_gq:o_gk],
                                   w[:, DIFF_QK_W:o_dv], w[:, o_gk:o_gv],
                                   w[:, o_dv:o_gq], w[:, o_gv:]],
                                  axis=1).T.astype(BF16)
            qt, k, vt = _attn_proj(xa, modl, g[0:1], w_t, cos_t, sin_t, gqa_qk_g[i].reshape(2, HEAD_DIM, 1))
            o = _attention(qt, k, vt, diff_lambda[i], diff_subln_g[i].reshape(DIFF_V_DIM, 1), lam_init)
            xa = _post_ffn(xa, o, modl, g, attn_w_out[i].astype(BF16), wgu, wd, gdn=False)
        else:
            w = gdn_w_in[i]
            wab = w[:, 4 * GDN_W:]
            q, k, v, sz, gb, gbt = _gdn_proj(
                xa, modl, g[0:1], w[:, :3 * GDN_W].astype(BF16), w[:, 3 * GDN_W:4 * GDN_W].astype(BF16),
                wab.astype(BF16), wab.T.astype(BF16), gdn_conv_w[i], gdn_a_log[i], gdn_dt_bias[i])
            gbt4 = gbt.reshape(b, 2 * nh2, t // GDN_CHUNK, GDN_CHUNK).transpose(0, 2, 1, 3)
            of, ob = _gdn_scan(q, k, v, gb, gbt4)
            xa = _post_ffn(xa, (of, ob, sz, gdn_norm_g[i].reshape(1, GDN_HEAD_DIM)), modl, g,
                           gdn_w_out[i].astype(BF16), wgu, wd, gdn=True)
    return xa[:, CTX_LEN:]
```

```python
import functools
import math

import jax
import jax.numpy as jnp
from jax import lax
from jax.experimental import pallas as pl
from jax.experimental.pallas import tpu as pltpu

F32 = jnp.float32
BF16 = jnp.bfloat16

D_MODEL = 1024
CTX_LEN = 256
GRID_W = 64
RMS_EPS = 1e-6
L2_EPS = 1e-6
ROPE_THETA = 10000.0

HEAD_DIM = 64
ATTN_SCALE = HEAD_DIM ** -0.5
DIFF_HEADS = 4
DIFF_V_DIM = 2 * HEAD_DIM
GQA_Q_HEADS = 8
GQA_KV_HEADS = 2
GQA_REP = GQA_Q_HEADS // GQA_KV_HEADS
DIFF_QK_W = DIFF_HEADS * 2 * HEAD_DIM
DIFF_V_W = DIFF_HEADS * DIFF_V_DIM
GQA_Q_W = GQA_Q_HEADS * HEAD_DIM
GQA_KV_W = GQA_KV_HEADS * HEAD_DIM
Q_W = DIFF_QK_W + GQA_Q_W
K_W = DIFF_QK_W + GQA_KV_W
V_W = DIFF_V_W + GQA_KV_W
SUM_ROWS = 16
VT_ROWS = V_W + (DIFF_HEADS + GQA_KV_HEADS) * SUM_ROWS
LOG2_E = math.log2(math.e)
N_MAPS = 2 * DIFF_HEADS + GQA_Q_HEADS
QK_AHEAD = 4

GDN_HEADS = 8
GDN_HEAD_DIM = 128
GDN_W = GDN_HEADS * GDN_HEAD_DIM
GDN_CONV_K = 4
GDN_CONV_LEFT = 2
GDN_CHUNK = 64
CONV_HALO = 8

FFN_HIDDEN = 2816
FFN_CHUNK = 256

TOK_TILE = 256
KV_CHUNK = 256
VMEM_LIMIT = 56 * 1024 * 1024

_NT = (((1,), (1,)), ((), ()))
_TN = (((0,), (0,)), ((), ()))


def _rms(x, g):
    return x * lax.rsqrt(jnp.mean(x * x, axis=-1, keepdims=True) + RMS_EPS) * g


def _silu(x):
    return x * jax.nn.sigmoid(x)


def _params(sem):
    return pltpu.CompilerParams(dimension_semantics=sem, vmem_limit_bytes=VMEM_LIMIT)


def _const_spec(shape):
    n = len(shape)
    return pl.BlockSpec(shape, lambda *_: (0,) * n)


def _mod_body(c_ref, w_ref, b_ref, o_ref):
    sc = _silu(c_ref[...])
    o_ref[0] = jnp.dot(sc, w_ref[0], precision=lax.Precision.HIGHEST,
                       preferred_element_type=F32) + b_ref[0]


def _modulation(cc, ada_w, ada_b):
    depth, d, w6 = ada_w.shape
    rows = cc.shape[0]
    nblk = w6 // d
    return pl.pallas_call(
        _mod_body,
        grid=(depth, nblk),
        in_specs=[pl.BlockSpec((rows, d), lambda l, j: (0, 0)),
                  pl.BlockSpec((1, d, d), lambda l, j: (l, 0, j)),
                  pl.BlockSpec((1, 1, d), lambda l, j: (l, 0, j))],
        out_specs=pl.BlockSpec((1, rows, d), lambda l, j: (l, 0, j)),
        out_shape=jax.ShapeDtypeStruct((depth, rows, w6), F32),
        compiler_params=_params(("parallel", "parallel")),
        name="adaln_mod",
    )(cc, ada_w, ada_b.reshape(depth, 1, w6))


def _attn_proj_body(x_ref, mod_ref, g_ref, w_ref, cos_ref, sin_ref, qkg_ref, qt_out, k_out, vt_out):
    x = x_ref[0]
    mod = mod_ref[0, 0]
    h = (_rms(x, g_ref[...]) * (1.0 + mod[1:2]) + mod[0:1]).astype(BF16)
    p = lax.dot_general(w_ref[...], h, _NT, preferred_element_type=F32)
    ones = jnp.ones((SUM_ROWS, p.shape[1]), F32)
    v_blocks = []
    for r0, width in ([(hd * DIFF_V_DIM, DIFF_V_DIM) for hd in range(DIFF_HEADS)]
                      + [(DIFF_V_W + grp * HEAD_DIM, HEAD_DIM) for grp in range(GQA_KV_HEADS)]):
        v_blocks += [p[Q_W + K_W + r0:Q_W + K_W + r0 + width], ones]
    vt_out[0, 0] = jnp.concatenate(v_blocks, axis=0).astype(BF16)
    cos = cos_ref[...]
    sin = sin_ref[...]

    def rope(xh):
        swapped = jnp.concatenate([xh[16:32], xh[0:16], xh[48:64], xh[32:48]], axis=0)
        return xh * cos + swapped * sin

    def norm(xh, g):
        r = lax.rsqrt(jnp.mean(xh * xh, axis=0, keepdims=True) + RMS_EPS)
        return xh * r * g

    gq_g = qkg_ref[0]
    gk_g = qkg_ref[1]
    for j in range(Q_W // HEAD_DIM):
        r0 = j * HEAD_DIM
        xh = p[r0:r0 + HEAD_DIM]
        if r0 >= DIFF_QK_W:
            xh = norm(xh, gq_g)
        qt_out[0, 0, r0:r0 + HEAD_DIM, :] = (rope(xh) * (ATTN_SCALE * LOG2_E)).astype(BF16)
    k_heads = []
    for j in range(K_W // HEAD_DIM):
        r0 = j * HEAD_DIM
        xh = p[Q_W + r0:Q_W + r0 + HEAD_DIM]
        if r0 >= DIFF_QK_W:
            xh = norm(xh, gk_g)
        k_heads.append(rope(xh))
    k_out[0] = jnp.concatenate(k_heads, axis=0).T.astype(BF16)


def _attn_proj(xa, modl, g0, w_t, cos_t, sin_t, qkg):
    b, t, d = xa.shape
    nt = t // TOK_TILE
    return pl.pallas_call(
        _attn_proj_body,
        grid=(b, nt),
        in_specs=[pl.BlockSpec((1, TOK_TILE, d), lambda i, j: (i, j, 0)),
                  pl.BlockSpec((1, 1, 6, d), lambda i, j: (i, jnp.minimum(j, 1), 0, 0)),
                  _const_spec((1, d)),
                  _const_spec(w_t.shape),
                  pl.BlockSpec((HEAD_DIM, TOK_TILE), lambda i, j: (0, j)),
                  pl.BlockSpec((HEAD_DIM, TOK_TILE), lambda i, j: (0, j)),
                  _const_spec((2, HEAD_DIM, 1))],
        out_specs=[pl.BlockSpec((1, 1, Q_W, TOK_TILE), lambda i, j: (i, j, 0, 0)),
                   pl.BlockSpec((1, TOK_TILE, K_W), lambda i, j: (i, j, 0)),
                   pl.BlockSpec((1, 1, VT_ROWS, TOK_TILE), lambda i, j: (i, j, 0, 0))],
        out_shape=[jax.ShapeDtypeStruct((b, nt, Q_W, TOK_TILE), BF16),
                   jax.ShapeDtypeStruct((b, t, K_W), BF16),
                   jax.ShapeDtypeStruct((b, nt, VT_ROWS, TOK_TILE), BF16)],
        compiler_params=_params(("parallel", "parallel")),
        name="attn_in_proj",
    )(xa, modl, g0, w_t, cos_t, sin_t, qkg)


def _attn_body(lam_init, qt_ref, k_ref, vt_ref, lam_ref, subln_ref, o_ref, qpad_ref, m_ref, acc_ref):
    qi = pl.program_id(1)
    slab = 2 * HEAD_DIM
    diff_blk = DIFF_V_DIM + SUM_ROWS
    gqa_blk = HEAD_DIM + SUM_ROWS
    maps = []
    for hd in range(DIFF_HEADS):
        for mm in range(2):
            maps.append((2 * hd + mm, hd, hd * diff_blk, diff_blk))
    for hq in range(GQA_Q_HEADS):
        grp = hq // GQA_REP
        maps.append((2 * DIFF_HEADS + hq, DIFF_HEADS, DIFF_HEADS * diff_blk + grp * gqa_blk, gqa_blk))

    zeros = jnp.zeros((HEAD_DIM, qt_ref.shape[3]), BF16)
    for i, (qh, _, _, _) in enumerate(maps):
        half = (qh % 2) if qh < 2 * DIFF_HEADS else (qh - 2 * DIFF_HEADS) // GQA_REP
        qh_t = qt_ref[0, 0, qh * HEAD_DIM:(qh + 1) * HEAD_DIM, :]
        qpad_ref[i] = jnp.concatenate([qh_t, zeros] if half == 0 else [zeros, qh_t], axis=0)
    def scores(c, i):
        ks = maps[i][1]
        return jnp.dot(k_ref[0, c * KV_CHUNK:(c + 1) * KV_CHUNK, ks * slab:(ks + 1) * slab], qpad_ref[i],
                       preferred_element_type=F32)

    def run(n_chunks):
        items = [(c, i) for c in range(n_chunks) for i in range(N_MAPS)]
        pending = [scores(*items[n]) for n in range(QK_AHEAD)]
        for n, (c, i) in enumerate(items):
            s = pending.pop(0)
            if n + QK_AHEAD < len(items):
                pending.append(scores(*items[n + QK_AHEAD]))
            _, _, v_row0, v_w = maps[i]
            m_new = jnp.max(s, axis=0, keepdims=True)
            if c > 0:
                m_old = m_ref[i]
                m_new = jnp.maximum(m_old, m_new)
                alpha = jnp.exp2(m_old - m_new)
            m_ref[i] = m_new
            p = jnp.exp2(s - m_new)
            pv = jnp.dot(vt_ref[0, c, v_row0:v_row0 + v_w, :], p.astype(BF16), preferred_element_type=F32)
            acc_ref[i, 0:v_w, :] = pv if c == 0 else alpha * acc_ref[i, 0:v_w, :] + pv

    @pl.when(qi == 0)
    def _():
        run(CTX_LEN // KV_CHUNK)

    @pl.when(qi > 0)
    def _():
        run(vt_ref.shape[1])

    lv = lam_ref[...]
    lam = (jnp.exp(jnp.sum(lv[0:1] * lv[1:2], axis=-1, keepdims=True))
           - jnp.exp(jnp.sum(lv[2:3] * lv[3:4], axis=-1, keepdims=True)) + lam_init)
    def normalized(i, width):
        return acc_ref[i, 0:width, :] / acc_ref[i, width:width + 1, :]

    for hd in range(DIFF_HEADS):
        od = normalized(2 * hd, DIFF_V_DIM) - lam * normalized(2 * hd + 1, DIFF_V_DIM)
        od = od * lax.rsqrt(jnp.mean(od * od, axis=0, keepdims=True) + RMS_EPS) * subln_ref[...]
        o_ref[0, :, hd * DIFF_V_DIM:(hd + 1) * DIFF_V_DIM] = (od * (1.0 - lam_init)).T.astype(o_ref.dtype)
    for pair in range(GQA_Q_HEADS // 2):
        i0 = 2 * DIFF_HEADS + 2 * pair
        og = jnp.concatenate([normalized(i0, HEAD_DIM), normalized(i0 + 1, HEAD_DIM)], axis=0)
        c0 = DIFF_V_W + pair * slab
        o_ref[0, :, c0:c0 + slab] = og.T.astype(o_ref.dtype)


def _attention(qt, k, vt, lam_vec, subln_g, lam_init):
    b, t, _ = k.shape
    nq = t // TOK_TILE
    return pl.pallas_call(
        functools.partial(_attn_body, lam_init),
        grid=(b, nq),
        in_specs=[pl.BlockSpec((1, 1, Q_W, TOK_TILE), lambda i, j: (i, j, 0, 0)),
                  pl.BlockSpec((1, t, K_W), lambda i, j: (i, 0, 0)),
                  pl.BlockSpec((1,) + vt.shape[1:], lambda i, j: (i, 0, 0, 0)),
                  _const_spec((4, HEAD_DIM)),
                  _const_spec((DIFF_V_DIM, 1))],
        out_specs=pl.BlockSpec((1, TOK_TILE, D_MODEL), lambda i, j: (i, j, 0)),
        out_shape=jax.ShapeDtypeStruct((b, t, D_MODEL), BF16),
        scratch_shapes=[pltpu.VMEM((N_MAPS, 2 * HEAD_DIM, TOK_TILE), BF16),
                        pltpu.VMEM((N_MAPS, 1, TOK_TILE), F32),
                        pltpu.VMEM((N_MAPS, DIFF_V_DIM + SUM_ROWS, TOK_TILE), F32)],
        compiler_params=_params(("parallel", "parallel")),
        name="diff_gqa_attention",
    )(qt, k, vt, lam_vec, subln_g)


def _post_body(gdn, *refs):
    if gdn:
        (x_ref, of_ref, ob_ref, sz_ref, ng_ref, mod_ref, g_ref, wo_ref, wgu_ref, wd_ref,
         xo_ref, a_ref) = refs
        parts = []
        for hd in range(GDN_HEADS):
            o = of_ref[0, hd].astype(F32) + ob_ref[0, hd].astype(F32)
            parts.append(_rms(o, ng_ref[...]))
        o = (jnp.concatenate(parts, axis=-1) * sz_ref[0].astype(F32)).astype(BF16)
    else:
        x_ref, o_ref, mod_ref, g_ref, wo_ref, wgu_ref, wd_ref, xo_ref, a_ref = refs
        o = o_ref[0]
    x = x_ref[0]
    mod = mod_ref[0, 0]
    g = g_ref[...]
    y = jnp.dot(o, wo_ref[...], preferred_element_type=F32)
    x = x + mod[2:3] * _rms(y, g[1:2])
    h = (_rms(x, g[2:3]) * (1.0 + mod[4:5]) + mod[3:4]).astype(BF16)
    for c in range(FFN_HIDDEN // FFN_CHUNK):
        c0 = c * FFN_CHUNK
        gate = jnp.dot(h, wgu_ref[:, c0:c0 + FFN_CHUNK], preferred_element_type=F32)
        up = jnp.dot(h, wgu_ref[:, FFN_HIDDEN + c0:FFN_HIDDEN + c0 + FFN_CHUNK],
                     preferred_element_type=F32)
        a_ref[:, c0:c0 + FFN_CHUNK] = (_silu(gate) * up).astype(BF16)
    ff = jnp.dot(a_ref[...], wd_ref[...], preferred_element_type=F32)
    xo_ref[0] = x + mod[5:6] * _rms(ff, g[3:4])


def _post_ffn(xa, mixer_out, modl, g, wo, wgu, wd, gdn):
    b, t, d = xa.shape
    nt = t // TOK_TILE
    tile = pl.BlockSpec((1, TOK_TILE, d), lambda i, j: (i, j, 0))
    if gdn:
        of, ob, sz, ng = mixer_out
        head_tile = pl.BlockSpec((1, GDN_HEADS, TOK_TILE, GDN_HEAD_DIM), lambda i, j: (i, 0, j, 0))
        mix_specs = [head_tile, head_tile, tile, _const_spec((1, GDN_HEAD_DIM))]
        mix_args = (of, ob, sz, ng)
    else:
        mix_specs = [tile]
        mix_args = (mixer_out,)
    return pl.pallas_call(
        functools.partial(_post_body, gdn),
        grid=(b, nt),
        in_specs=[tile] + mix_specs + [
            pl.BlockSpec((1, 1, 6, d), lambda i, j: (i, jnp.minimum(j, 1), 0, 0)),
            _const_spec((4, d)),
            _const_spec(wo.shape), _const_spec(wgu.shape), _const_spec(wd.shape)],
        out_specs=tile,
        out_shape=jax.ShapeDtypeStruct((b, t, d), F32),
        scratch_shapes=[pltpu.VMEM((TOK_TILE, FFN_HIDDEN), BF16)],
        compiler_params=_params(("parallel", "parallel")),
        name="out_proj_ffn_gdn" if gdn else "out_proj_ffn_attn",
    )(xa, *mix_args, modl, g, wo, wgu, wd)


def _gdn_proj_body(xp_ref, x_ref, xn_ref, mod_ref, g_ref, wqkv_ref, wz_ref, wab_ref, wabt_ref,
                   conv_ref, alog_ref, dtb_ref, alogt_ref, dtbt_ref,
                   q_out, k_out, v_out, sz_out, gb_out, gbt_out):
    j = pl.program_id(1)
    nt = pl.num_programs(1)
    mod = mod_ref[0, 0]
    g = g_ref[...]

    def prep(xv):
        return _rms(xv, g) * (1.0 + mod[1:2]) + mod[0:1]

    prev_ok = jnp.where(j >= 2, 1.0, 0.0)
    next_ok = jnp.where(jnp.logical_and(j >= 1, j < nt - 1), 1.0, 0.0)
    h_f32 = prep(x_ref[0])
    h_main = h_f32.astype(BF16)
    h_cat = jnp.concatenate([prep(xp_ref[0]) * prev_ok, h_f32, prep(xn_ref[0]) * next_ok],
                            axis=0).astype(BF16)
    p = jnp.dot(h_cat, wqkv_ref[...], preferred_element_type=F32)
    cw = conv_ref[...]
    acc = p[CONV_HALO - GDN_CONV_LEFT:CONV_HALO - GDN_CONV_LEFT + TOK_TILE] * cw[0:1]
    for tap in range(1, GDN_CONV_K):
        r0 = CONV_HALO - GDN_CONV_LEFT + tap
        acc = acc + p[r0:r0 + TOK_TILE] * cw[tap:tap + 1]
    qkv = _silu(acc)
    for hd in range(GDN_HEADS):
        c0 = hd * GDN_HEAD_DIM
        qh = qkv[:, c0:c0 + GDN_HEAD_DIM]
        kh = qkv[:, GDN_W + c0:GDN_W + c0 + GDN_HEAD_DIM]
        qn = qh * lax.rsqrt(jnp.sum(qh * qh, axis=-1, keepdims=True) + L2_EPS) * (GDN_HEAD_DIM ** -0.5)
        kn = kh * lax.rsqrt(jnp.sum(kh * kh, axis=-1, keepdims=True) + L2_EPS)
        q_out[0, hd] = qn.astype(BF16)
        k_out[0, hd] = kn.astype(BF16)
        v_out[0, hd] = qkv[:, 2 * GDN_W + c0:2 * GDN_W + c0 + GDN_HEAD_DIM].astype(BF16)
    sz_out[0] = _silu(jnp.dot(h_main, wz_ref[...], preferred_element_type=F32)).astype(BF16)

    nh2 = 2 * GDN_HEADS
    ab = jnp.dot(h_main, wab_ref[...], preferred_element_type=F32)
    gdec = -jnp.exp(alog_ref[...]) * jax.nn.softplus(ab[:, :nh2] + dtb_ref[...])
    gb_out[0] = jnp.concatenate([gdec, jax.nn.sigmoid(ab[:, nh2:])], axis=-1)
    abt = lax.dot_general(wabt_ref[...], h_main, _NT, preferred_element_type=F32)
    gdec_t = -jnp.exp(alogt_ref[...]) * jax.nn.softplus(abt[:nh2] + dtbt_ref[...])
    gbt_out[0] = jnp.concatenate([gdec_t, jax.nn.sigmoid(abt[nh2:])], axis=0)


def _gdn_proj(xa, modl, g0, wqkv, wz, wab, wab_t, conv_w, a_log, dt_bias):
    b, t, d = xa.shape
    nt = t // TOK_TILE
    per = TOK_TILE // CONV_HALO
    last = t // CONV_HALO - 1
    nh2 = 2 * GDN_HEADS
    head_out = pl.BlockSpec((1, GDN_HEADS, TOK_TILE, GDN_HEAD_DIM), lambda i, j: (i, 0, j, 0))
    head_shape = jax.ShapeDtypeStruct((b, GDN_HEADS, t, GDN_HEAD_DIM), BF16)
    return pl.pallas_call(
        _gdn_proj_body,
        grid=(b, nt),
        in_specs=[pl.BlockSpec((1, CONV_HALO, d), lambda i, j: (i, jnp.maximum(j * per - 1, 0), 0)),
                  pl.BlockSpec((1, TOK_TILE, d), lambda i, j: (i, j, 0)),
                  pl.BlockSpec((1, CONV_HALO, d), lambda i, j: (i, jnp.minimum((j + 1) * per, last), 0)),
                  pl.BlockSpec((1, 1, 6, d), lambda i, j: (i, jnp.minimum(j, 1), 0, 0)),
                  _const_spec((1, d)),
                  _const_spec(wqkv.shape), _const_spec(wz.shape), _const_spec(wab.shape),
                  _const_spec(wab_t.shape), _const_spec(conv_w.shape),
                  _const_spec((1, nh2)), _const_spec((1, nh2)),
                  _const_spec((nh2, 1)), _const_spec((nh2, 1))],
        out_specs=[head_out, head_out, head_out,
                   pl.BlockSpec((1, TOK_TILE, d), lambda i, j: (i, j, 0)),
                   pl.BlockSpec((1, TOK_TILE, 2 * nh2), lambda i, j: (i, j, 0)),
                   pl.BlockSpec((1, 2 * nh2, TOK_TILE), lambda i, j: (i, 0, j))],
        out_shape=[head_shape, head_shape, head_shape,
                   jax.ShapeDtypeStruct((b, t, d), BF16),
                   jax.ShapeDtypeStruct((b, t, 2 * nh2), F32),
                   jax.ShapeDtypeStruct((b, 2 * nh2, t), F32)],
        compiler_params=_params(("parallel", "parallel")),
        name="gdn_in_proj",
    )(xa, xa, xa, modl, g0, wqkv, wz, wab, wab_t, conv_w,
      a_log.reshape(1, nh2), dt_bias.reshape(1, nh2), a_log.reshape(nh2, 1), dt_bias.reshape(nh2, 1))


def _unit_tri_inverses(ms, lowers):
    n = ms[0].shape[0]
    row = lax.broadcasted_iota(jnp.int32, (n, n), 0)
    col = lax.broadcasted_iota(jnp.int32, (n, n), 1)
    t_offs = None
    k = 1
    while k < n:
        same = (row ^ col) < 2 * k
        joins = {True: same & ((row & k) != 0) & ((col & k) == 0),
                 False: same & ((col & k) != 0) & ((row & k) == 0)}
        parts = [jnp.where(joins[lo], m, 0.0) for m, lo in zip(ms, lowers)]
        if t_offs is None:
            t_offs = [-a for a in parts]
        else:
            tbs = [t.astype(BF16) for t in t_offs]
            zs = [a + jnp.dot(tb, a.astype(BF16), preferred_element_type=F32) for a, tb in zip(parts, tbs)]
            t_offs = [t - z - jnp.dot(z.astype(BF16), tb, preferred_element_type=F32)
                      for t, z, tb in zip(t_offs, zs, tbs)]
        k *= 2
    eye = jnp.where(row == col, 1.0, 0.0)
    return [t + eye for t in t_offs]


def _gdn_scan_body(qf, kf, vf, qb, kb, vb, gbf, gbb, gtf, gtb, of, ob, s_ref):
    @pl.when(pl.program_id(1) == 0)
    def _():
        s_ref[...] = jnp.zeros_like(s_ref)

    c = GDN_CHUNK
    row = lax.broadcasted_iota(jnp.int32, (c, c), 0)
    col = lax.broadcasted_iota(jnp.int32, (c, c), 1)
    lower = jnp.where(row >= col, 1.0, 0.0)
    upper = jnp.where(row <= col, 1.0, 0.0)
    nh2 = 2 * GDN_HEADS
    hi = lax.Precision.HIGHEST

    chains = []
    for direction, (q_ref, k_ref, v_ref, gb_ref, gt_ref, o_ref) in enumerate(
            ((qf, kf, vf, gbf, gtf, of), (qb, kb, vb, gbb, gtb, ob))):
        gb = gb_ref[0]
        gt = gt_ref[0, 0]
        tri_c, tri_r = (lower, upper) if direction == 0 else (upper, lower)
        gc = jnp.dot(tri_c, gb[:, :nh2], precision=hi, preferred_element_type=F32)
        gr = jnp.dot(gt[:nh2], tri_r, precision=hi, preferred_element_type=F32)
        incl = (row >= col) if direction == 0 else (row <= col)
        strict = (row > col) if direction == 0 else (row < col)
        for hd in range(GDN_HEADS):
            ch = direction * GDN_HEADS + hd
            gcol = gc[:, ch:ch + 1]
            grow = gr[ch:ch + 1, :]
            chains.append(dict(
                ch=ch, hd=hd, lower=direction == 0, strict=strict, o_ref=o_ref,
                k=k_ref[0, hd], q=q_ref[0, hd], v=v_ref[0, hd],
                beta=gb[:, nh2 + ch:nh2 + ch + 1], gcol=gcol,
                tot=gcol[c - 1:c] if direction == 0 else gcol[0:1],
                decay=jnp.where(incl, jnp.exp(jnp.where(incl, gcol - grow, 0.0)), 0.0)))

    for w in chains:
        w["gram"] = lax.dot_general(jnp.concatenate([w["k"], w["q"]], axis=0), w["k"], _NT,
                                    preferred_element_type=F32)
    ms = [jnp.where(w["strict"], w["beta"] * w["gram"][:c] * w["decay"], 0.0) for w in chains]
    tmats = _unit_tri_inverses(ms, [w["lower"] for w in chains])
    for w in chains:
        kf32 = w["k"].astype(F32)
        egc = jnp.exp(w["gcol"])
        lhs = jnp.concatenate([(kf32 * (w["beta"] * egc)).astype(BF16),
                               (w["q"].astype(F32) * egc).astype(BF16)], axis=0)
        w["state"] = s_ref[w["ch"]]
        w["ks"] = jnp.dot(lhs, w["state"].astype(BF16), preferred_element_type=F32)
        w["kdec"] = (kf32 * jnp.exp(w["tot"] - w["gcol"])).astype(BF16)
    for w, tmat in zip(chains, tmats):
        resid = w["v"].astype(F32) * w["beta"] - w["ks"][:c]
        w["v_new"] = jnp.dot(tmat.astype(BF16), resid.astype(BF16),
                             preferred_element_type=F32).astype(BF16)
    for w in chains:
        attn = (w["gram"][c:] * w["decay"]).astype(BF16)
        o = w["ks"][c:] + jnp.dot(attn, w["v_new"], preferred_element_type=F32)
        w["o_ref"][0, w["hd"]] = o.astype(w["o_ref"].dtype)
    for w in chains:
        s_ref[w["ch"]] = (w["state"] * jnp.exp(w["tot"])
                          + lax.dot_general(w["kdec"], w["v_new"], _TN, preferred_element_type=F32))


def _gdn_scan(q, k, v, gb, gbt4):
    b, nh, t, dh = q.shape
    nchunk = t // GDN_CHUNK
    nctx = CTX_LEN // GDN_CHUNK
    nh2 = 2 * GDN_HEADS

    def bwd(s):
        return jnp.where(s < nctx, nctx - 1 - s, nchunk - 1 + nctx - s)

    head_f = pl.BlockSpec((1, nh, GDN_CHUNK, dh), lambda i, s: (i, 0, s, 0))
    head_b = pl.BlockSpec((1, nh, GDN_CHUNK, dh), lambda i, s: (i, 0, bwd(s), 0))
    out_shape = jax.ShapeDtypeStruct((b, nh, t, dh), BF16)
    return pl.pallas_call(
        _gdn_scan_body,
        grid=(b, nchunk),
        in_specs=[head_f, head_f, head_f, head_b, head_b, head_b,
                  pl.BlockSpec((1, GDN_CHUNK, 2 * nh2), lambda i, s: (i, s, 0)),
                  pl.BlockSpec((1, GDN_CHUNK, 2 * nh2), lambda i, s: (i, bwd(s), 0)),
                  pl.BlockSpec((1, 1, 2 * nh2, GDN_CHUNK), lambda i, s: (i, s, 0, 0)),
                  pl.BlockSpec((1, 1, 2 * nh2, GDN_CHUNK), lambda i, s: (i, bwd(s), 0, 0))],
        out_specs=[head_f, head_b],
        out_shape=[out_shape, out_shape],
        scratch_shapes=[pltpu.VMEM((nh2, dh, dh), F32)],
        compiler_params=_params(("arbitrary", "arbitrary")),
        name="gdn_chunk_scan",
    )(q, k, v, q, k, v, gb, gb, gbt4, gbt4)


def _rope_tables(n_lat):
    rows = n_lat // GRID_W
    row_ids = jnp.repeat(jnp.arange(rows, dtype=F32), GRID_W)[:n_lat]
    col_ids = jnp.tile(jnp.arange(GRID_W, dtype=F32), rows)[:n_lat]
    axis_dim = HEAD_DIM // 2
    inv_freq = ROPE_THETA ** (-jnp.arange(0, axis_dim, 2, dtype=F32) / axis_dim)
    ang_r = row_ids[:, None] * inv_freq
    ang_c = col_ids[:, None] * inv_freq
    ang = jnp.concatenate([ang_r, ang_r, ang_c, ang_c], axis=-1)
    cos = jnp.concatenate([jnp.ones((CTX_LEN, HEAD_DIM), F32), jnp.cos(ang)], axis=0)
    sin = jnp.concatenate([jnp.zeros((CTX_LEN, HEAD_DIM), F32), jnp.sin(ang)], axis=0)
    sign = jnp.tile(jnp.repeat(jnp.array([-1.0, 1.0], F32), HEAD_DIM // 4), 2)
    return cos.T, (sin * sign).T


def kernel(x, c, ctx, c_ctx, ada_w, ada_b, norm_g, attn_w_in, attn_w_out, diff_lambda, diff_subln_g,
           gqa_qk_g, gdn_w_in, gdn_conv_w, gdn_a_log, gdn_dt_bias, gdn_norm_g, gdn_w_out,
           ffn_w_gate_up, ffn_w_down):
    b, n_lat, d = x.shape
    depth = ada_w.shape[0]
    assert d == D_MODEL and ctx.shape[1] == CTX_LEN and n_lat % TOK_TILE == 0
    t = CTX_LEN + n_lat
    xa = jnp.concatenate([ctx, x], axis=1)

    rows = -(-(b + 1) // 8) * 8
    cc = jnp.concatenate([c, c_ctx[None], jnp.zeros((rows - b - 1, d), F32)], axis=0)
    mods = _modulation(cc, ada_w, ada_b)
    cos_t, sin_t = _rope_tables(n_lat)
    nh2 = 2 * GDN_HEADS

    for l in range(depth):
        i = l // 2
        ml = mods[l]
        modl = jnp.stack([jnp.broadcast_to(ml[b].reshape(1, 6, d), (b, 6, d)),
                          ml[:b].reshape(b, 6, d)], axis=1)
        g = norm_g[l]
        wgu = ffn_w_gate_up[l].astype(BF16)
        wd = ffn_w_down[l].astype(BF16)
        if l % 2 == 0:
            lam_init = 0.8 - 0.6 * math.exp(-0.3 * l)
            w = attn_w_in[i]
            o_dv = 2 * DIFF_QK_W
            o_gq = o_dv + DIFF_V_W
            o_gk = o_gq + GQA_Q_W
            o_gv = o_gk + GQA_KV_W
            w_t = jnp.concatenate([w[:, :DIFF_QK_W], w[:, o_gq:o_gk],
                                   w[:, DIFF_QK_W:o_dv], w[:, o_gk:o_gv],
                                   w[:, o_dv:o_gq], w[:, o_gv:]],
                                  axis=1).T.astype(BF16)
            qt, k, vt = _attn_proj(xa, modl, g[0:1], w_t, cos_t, sin_t, gqa_qk_g[i].reshape(2, HEAD_DIM, 1))
            o = _attention(qt, k, vt, diff_lambda[i], diff_subln_g[i].reshape(DIFF_V_DIM, 1), lam_init)
            xa = _post_ffn(xa, o, modl, g, attn_w_out[i].astype(BF16), wgu, wd, gdn=False)
        else:
            w = gdn_w_in[i]
            wab = w[:, 4 * GDN_W:]
            q, k, v, sz, gb, gbt = _gdn_proj(
                xa, modl, g[0:1], w[:, :3 * GDN_W].astype(BF16), w[:, 3 * GDN_W:4 * GDN_W].astype(BF16),
                wab.astype(BF16), wab.T.astype(BF16), gdn_conv_w[i], gdn_a_log[i], gdn_dt_bias[i])
            gbt4 = gbt.reshape(b, 2 * nh2, t // GDN_CHUNK, GDN_CHUNK).transpose(0, 2, 1, 3)
            of, ob = _gdn_scan(q, k, v, gb, gbt4)
            xa = _post_ffn(xa, (of, ob, sz, gdn_norm_g[i].reshape(1, GDN_HEAD_DIM)), modl, g,
                           gdn_w_out[i].astype(BF16), wgu, wd, gdn=True)
    return xa[:, CTX_LEN:]
```

```python
import functools
import math

import jax
import jax.numpy as jnp
from jax import lax
from jax.experimental import pallas as pl
from jax.experimental.pallas import tpu as pltpu

F32 = jnp.float32
BF16 = jnp.bfloat16

D_MODEL = 1024
CTX_LEN = 256
GRID_W = 64
RMS_EPS = 1e-6
L2_EPS = 1e-6
ROPE_THETA = 10000.0

HEAD_DIM = 64
ATTN_SCALE = HEAD_DIM ** -0.5
DIFF_HEADS = 4
DIFF_V_DIM = 2 * HEAD_DIM
GQA_Q_HEADS = 8
GQA_KV_HEADS = 2
GQA_REP = GQA_Q_HEADS // GQA_KV_HEADS
DIFF_QK_W = DIFF_HEADS * 2 * HEAD_DIM
DIFF_V_W = DIFF_HEADS * DIFF_V_DIM
GQA_Q_W = GQA_Q_HEADS * HEAD_DIM
GQA_KV_W = GQA_KV_HEADS * HEAD_DIM
Q_W = DIFF_QK_W + GQA_Q_W
K_W = DIFF_QK_W + GQA_KV_W
V_W = DIFF_V_W + GQA_KV_W
SUM_ROWS = 16
VT_ROWS = V_W + (DIFF_HEADS + GQA_KV_HEADS) * SUM_ROWS
LOG2_E = math.log2(math.e)
N_MAPS = 2 * DIFF_HEADS + GQA_Q_HEADS
QK_AHEAD = 4

GDN_HEADS = 8
GDN_HEAD_DIM = 128
GDN_W = GDN_HEADS * GDN_HEAD_DIM
GDN_CONV_K = 4
GDN_CONV_LEFT = 2
GDN_CHUNK = 64
CONV_HALO = 8

FFN_HIDDEN = 2816
FFN_CHUNK = 256

TOK_TILE = 256
KV_CHUNK = 256
VMEM_LIMIT = 56 * 1024 * 1024

_NT = (((1,), (1,)), ((), ()))
_TN = (((0,), (0,)), ((), ()))


def _rms(x, g):
    return x * lax.rsqrt(jnp.mean(x * x, axis=-1, keepdims=True) + RMS_EPS) * g


def _silu(x):
    return x * jax.nn.sigmoid(x)


def _params(sem):
    return pltpu.CompilerParams(dimension_semantics=sem, vmem_limit_bytes=VMEM_LIMIT)


def _const_spec(shape):
    n = len(shape)
    return pl.BlockSpec(shape, lambda *_: (0,) * n)


def _mod_body(c_ref, w_ref, b_ref, o_ref):
    sc = _silu(c_ref[...])
    o_ref[0] = jnp.dot(sc, w_ref[0], precision=lax.Precision.HIGHEST,
                       preferred_element_type=F32) + b_ref[0]


def _modulation(cc, ada_w, ada_b):
    depth, d, w6 = ada_w.shape
    rows = cc.shape[0]
    nblk = w6 // d
    return pl.pallas_call(
        _mod_body,
        grid=(depth, nblk),
        in_specs=[pl.BlockSpec((rows, d), lambda l, j: (0, 0)),
                  pl.BlockSpec((1, d, d), lambda l, j: (l, 0, j)),
                  pl.BlockSpec((1, 1, d), lambda l, j: (l, 0, j))],
        out_specs=pl.BlockSpec((1, rows, d), lambda l, j: (l, 0, j)),
        out_shape=jax.ShapeDtypeStruct((depth, rows, w6), F32),
        compiler_params=_params(("parallel", "parallel")),
        name="adaln_mod",
    )(cc, ada_w, ada_b.reshape(depth, 1, w6))


def _attn_proj_body(x_ref, mod_ref, g_ref, w_ref, cos_ref, sin_ref, qkg_ref, qt_out, k_out, vt_out):
    x = x_ref[0]
    mod = mod_ref[0, 0]
    h = (_rms(x, g_ref[...]) * (1.0 + mod[1:2]) + mod[0:1]).astype(BF16)
    p = lax.dot_general(w_ref[...], h, _NT, preferred_element_type=F32)
    ones = jnp.ones((SUM_ROWS, p.shape[1]), F32)
    v_blocks = []
    for r0, width in ([(hd * DIFF_V_DIM, DIFF_V_DIM) for hd in range(DIFF_HEADS)]
                      + [(DIFF_V_W + grp * HEAD_DIM, HEAD_DIM) for grp in range(GQA_KV_HEADS)]):
        v_blocks += [p[Q_W + K_W + r0:Q_W + K_W + r0 + width], ones]
    vt_out[0, 0] = jnp.concatenate(v_blocks, axis=0).astype(BF16)
    cos = cos_ref[...]
    sin = sin_ref[...]

    def rope(xh):
        swapped = jnp.concatenate([xh[16:32], xh[0:16], xh[48:64], xh[32:48]], axis=0)
        return xh * cos + swapped * sin

    def norm(xh, g):
        r = lax.rsqrt(jnp.mean(xh * xh, axis=0, keepdims=True) + RMS_EPS)
        return xh * r * g

    gq_g = qkg_ref[0]
    gk_g = qkg_ref[1]
    for j in range(Q_W // HEAD_DIM):
        r0 = j * HEAD_DIM
        xh = p[r0:r0 + HEAD_DIM]
        if r0 >= DIFF_QK_W:
            xh = norm(xh, gq_g)
        qt_out[0, 0, r0:r0 + HEAD_DIM, :] = (rope(xh) * (ATTN_SCALE * LOG2_E)).astype(BF16)
    k_heads = []
    for j in range(K_W // HEAD_DIM):
        r0 = j * HEAD_DIM
        xh = p[Q_W + r0:Q_W + r0 + HEAD_DIM]
        if r0 >= DIFF_QK_W:
            xh = norm(xh, gk_g)
        k_heads.append(rope(xh))
    k_out[0] = jnp.concatenate(k_heads, axis=0).T.astype(BF16)


def _attn_proj(xa, modl, g0, w_t, cos_t, sin_t, qkg):
    b, t, d = xa.shape
    nt = t // TOK_TILE
    return pl.pallas_call(
        _attn_proj_body,
        grid=(b, nt),
        in_specs=[pl.BlockSpec((1, TOK_TILE, d), lambda i, j: (i, j, 0)),
                  pl.BlockSpec((1, 1, 6, d), lambda i, j: (i, jnp.minimum(j, 1), 0, 0)),
                  _const_spec((1, d)),
                  _const_spec(w_t.shape),
                  pl.BlockSpec((HEAD_DIM, TOK_TILE), lambda i, j: (0, j)),
                  pl.BlockSpec((HEAD_DIM, TOK_TILE), lambda i, j: (0, j)),
                  _const_spec((2, HEAD_DIM, 1))],
        out_specs=[pl.BlockSpec((1, 1, Q_W, TOK_TILE), lambda i, j: (i, j, 0, 0)),
                   pl.BlockSpec((1, TOK_TILE, K_W), lambda i, j: (i, j, 0)),
                   pl.BlockSpec((1, 1, VT_ROWS, TOK_TILE), lambda i, j: (i, j, 0, 0))],
        out_shape=[jax.ShapeDtypeStruct((b, nt, Q_W, TOK_TILE), BF16),
                   jax.ShapeDtypeStruct((b, t, K_W), BF16),
                   jax.ShapeDtypeStruct((b, nt, VT_ROWS, TOK_TILE), BF16)],
        compiler_params=_params(("parallel", "parallel")),
        name="attn_in_proj",
    )(xa, modl, g0, w_t, cos_t, sin_t, qkg)


def _attn_body(lam_init, qt_ref, k_ref, vt_ref, lam_ref, subln_ref, o_ref, qpad_ref, m_ref, acc_ref):
    qi = pl.program_id(1)
    slab = 2 * HEAD_DIM
    diff_blk = DIFF_V_DIM + SUM_ROWS
    gqa_blk = HEAD_DIM + SUM_ROWS
    maps = []
    for hd in range(DIFF_HEADS):
        for mm in range(2):
            maps.append((2 * hd + mm, hd, hd * diff_blk, diff_blk))
    for hq in range(GQA_Q_HEADS):
        grp = hq // GQA_REP
        maps.append((2 * DIFF_HEADS + hq, DIFF_HEADS, DIFF_HEADS * diff_blk + grp * gqa_blk, gqa_blk))

    zeros = jnp.zeros((HEAD_DIM, qt_ref.shape[3]), BF16)
    for i, (qh, _, _, _) in enumerate(maps):
        half = (qh % 2) if qh < 2 * DIFF_HEADS else (qh - 2 * DIFF_HEADS) // GQA_REP
        qh_t = qt_ref[0, 0, qh * HEAD_DIM:(qh + 1) * HEAD_DIM, :]
        qpad_ref[i] = jnp.concatenate([qh_t, zeros] if half == 0 else [zeros, qh_t], axis=0)
    def scores(c, i):
        ks = maps[i][1]
        return jnp.dot(k_ref[0, c * KV_CHUNK:(c + 1) * KV_CHUNK, ks * slab:(ks + 1) * slab], qpad_ref[i],
                       preferred_element_type=F32)

    def run(n_chunks):
        items = [(c, i) for c in range(n_chunks) for i in range(N_MAPS)]
        pending = [scores(*items[n]) for n in range(QK_AHEAD)]
        for n, (c, i) in enumerate(items):
            s = pending.pop(0)
            if n + QK_AHEAD < len(items):
                pending.append(scores(*items[n + QK_AHEAD]))
            _, _, v_row0, v_w = maps[i]
            m_new = jnp.max(s, axis=0, keepdims=True)
            if c > 0:
                m_old = m_ref[i]
                m_new = jnp.maximum(m_old, m_new)
                alpha = jnp.exp2(m_old - m_new)
            m_ref[i] = m_new
            p = jnp.exp2(s - m_new)
            pv = jnp.dot(vt_ref[0, c, v_row0:v_row0 + v_w, :], p.astype(BF16), preferred_element_type=F32)
            acc_ref[i, 0:v_w, :] = pv if c == 0 else alpha * acc_ref[i, 0:v_w, :] + pv

    @pl.when(qi == 0)
    def _():
        run(CTX_LEN // KV_CHUNK)

    @pl.when(qi > 0)
    def _():
        run(vt_ref.shape[1])

    lv = lam_ref[...]
    lam = (jnp.exp(jnp.sum(lv[0:1] * lv[1:2], axis=-1, keepdims=True))
           - jnp.exp(jnp.sum(lv[2:3] * lv[3:4], axis=-1, keepdims=True)) + lam_init)
    def normalized(i, width):
        return acc_ref[i, 0:width, :] / acc_ref[i, width:width + 1, :]

    for hd in range(DIFF_HEADS):
        od = normalized(2 * hd, DIFF_V_DIM) - lam * normalized(2 * hd + 1, DIFF_V_DIM)
        od = od * lax.rsqrt(jnp.mean(od * od, axis=0, keepdims=True) + RMS_EPS) * subln_ref[...]
        o_ref[0, :, hd * DIFF_V_DIM:(hd + 1) * DIFF_V_DIM] = (od * (1.0 - lam_init)).T.astype(o_ref.dtype)
    for pair in range(GQA_Q_HEADS // 2):
        i0 = 2 * DIFF_HEADS + 2 * pair
        og = jnp.concatenate([normalized(i0, HEAD_DIM), normalized(i0 + 1, HEAD_DIM)], axis=0)
        c0 = DIFF_V_W + pair * slab
        o_ref[0, :, c0:c0 + slab] = og.T.astype(o_ref.dtype)


def _attention(qt, k, vt, lam_vec, subln_g, lam_init):
    b, t, _ = k.shape
    nq = t // TOK_TILE
    return pl.pallas_call(
        functools.partial(_attn_body, lam_init),
        grid=(b, nq),
        in_specs=[pl.BlockSpec((1, 1, Q_W, TOK_TILE), lambda i, j: (i, j, 0, 0)),
                  pl.BlockSpec((1, t, K_W), lambda i, j: (i, 0, 0)),
                  pl.BlockSpec((1,) + vt.shape[1:], lambda i, j: (i, 0, 0, 0)),
                  _const_spec((4, HEAD_DIM)),
                  _const_spec((DIFF_V_DIM, 1))],
        out_specs=pl.BlockSpec((1, TOK_TILE, D_MODEL), lambda i, j: (i, j, 0)),
        out_shape=jax.ShapeDtypeStruct((b, t, D_MODEL), BF16),
        scratch_shapes=[pltpu.VMEM((N_MAPS, 2 * HEAD_DIM, TOK_TILE), BF16),
                        pltpu.VMEM((N_MAPS, 1, TOK_TILE), F32),
                        pltpu.VMEM((N_MAPS, DIFF_V_DIM + SUM_ROWS, TOK_TILE), F32)],
        compiler_params=_params(("parallel", "parallel")),
        name="diff_gqa_attention",
    )(qt, k, vt, lam_vec, subln_g)


def _post_body(gdn, *refs):
    if gdn:
        (x_ref, of_ref, ob_ref, sz_ref, ng_ref, mod_ref, g_ref, wo_ref, wgu_ref, wd_ref,
         xo_ref, a_ref) = refs
        parts = []
        for hd in range(GDN_HEADS):
            o = of_ref[0, hd].astype(F32) + ob_ref[0, hd].astype(F32)
            parts.append(_rms(o, ng_ref[...]))
        o = (jnp.concatenate(parts, axis=-1) * sz_ref[0].astype(F32)).astype(BF16)
    else:
        x_ref, o_ref, mod_ref, g_ref, wo_ref, wgu_ref, wd_ref, xo_ref, a_ref = refs
        o = o_ref[0]
    x = x_ref[0]
    mod = mod_ref[0, 0]
    g = g_ref[...]
    y = jnp.dot(o, wo_ref[...], preferred_element_type=F32)
    x = x + mod[2:3] * _rms(y, g[1:2])
    h = (_rms(x, g[2:3]) * (1.0 + mod[4:5]) + mod[3:4]).astype(BF16)
    for c in range(FFN_HIDDEN // FFN_CHUNK):
        c0 = c * FFN_CHUNK
        gate = jnp.dot(h, wgu_ref[:, c0:c0 + FFN_CHUNK], preferred_element_type=F32)
        up = jnp.dot(h, wgu_ref[:, FFN_HIDDEN + c0:FFN_HIDDEN + c0 + FFN_CHUNK],
                     preferred_element_type=F32)
        a_ref[:, c0:c0 + FFN_CHUNK] = (_silu(gate) * up).astype(BF16)
    ff = jnp.dot(a_ref[...], wd_ref[...], preferred_element_type=F32)
    xo_ref[0] = x + mod[5:6] * _rms(ff, g[3:4])


def _post_ffn(xa, mixer_out, modl, g, wo, wgu, wd, gdn, latent_only):
    b, t, d = xa.shape
    skip = CTX_LEN // TOK_TILE if latent_only else 0
    nt = t // TOK_TILE - skip
    tile = pl.BlockSpec((1, TOK_TILE, d), lambda i, j: (i, j + skip, 0))
    if gdn:
        of, ob, sz, ng = mixer_out
        head_tile = pl.BlockSpec((1, GDN_HEADS, TOK_TILE, GDN_HEAD_DIM), lambda i, j: (i, 0, j + skip, 0))
        mix_specs = [head_tile, head_tile, tile, _const_spec((1, GDN_HEAD_DIM))]
        mix_args = (of, ob, sz, ng)
    else:
        mix_specs = [tile]
        mix_args = (mixer_out,)
    return pl.pallas_call(
        functools.partial(_post_body, gdn),
        grid=(b, nt),
        in_specs=[tile] + mix_specs + [
            pl.BlockSpec((1, 1, 6, d), lambda i, j: (i, jnp.minimum(j + skip, 1), 0, 0)),
            _const_spec((4, d)),
            _const_spec(wo.shape), _const_spec(wgu.shape), _const_spec(wd.shape)],
        out_specs=pl.BlockSpec((1, TOK_TILE, d), lambda i, j: (i, j, 0)),
        out_shape=jax.ShapeDtypeStruct((b, nt * TOK_TILE, d), F32),
        scratch_shapes=[pltpu.VMEM((TOK_TILE, FFN_HIDDEN), BF16)],
        compiler_params=_params(("parallel", "parallel")),
        name="out_proj_ffn_gdn" if gdn else "out_proj_ffn_attn",
    )(xa, *mix_args, modl, g, wo, wgu, wd)


def _gdn_proj_body(xp_ref, x_ref, xn_ref, mod_ref, g_ref, wqkv_ref, wz_ref, wab_ref, wabt_ref,
                   conv_ref, alog_ref, dtb_ref, alogt_ref, dtbt_ref,
                   q_out, k_out, v_out, sz_out, gb_out, gbt_out, xs_ref, ys_ref):
    j = pl.program_id(1)
    nt = pl.num_programs(1)
    mod = mod_ref[0, 0]
    g = g_ref[...]
    lanes = GDN_HEAD_DIM
    rows = TOK_TILE + 2 * CONV_HALO
    pitch = rows // 8

    def prep(xv):
        return _rms(xv, g) * (1.0 + mod[1:2]) + mod[0:1]

    prev_ok = jnp.where(j >= 2, 1.0, 0.0)
    next_ok = jnp.where(jnp.logical_and(j >= 1, j < nt - 1), 1.0, 0.0)
    h_f32 = prep(x_ref[0])
    h_main = h_f32.astype(BF16)
    h_cat = jnp.concatenate([prep(xp_ref[0]) * prev_ok, h_f32, prep(xn_ref[0]) * next_ok], axis=0)
    n_slab = h_cat.shape[1] // lanes
    for s in range(n_slab):
        xs_ref[s] = h_cat[:, s * lanes:(s + 1) * lanes]
    h_perm = jnp.concatenate(
        [jnp.concatenate([xs_ref[s, pl.ds(a, 8, stride=pitch), :] for s in range(n_slab)], axis=1)
         for a in range(pitch)], axis=0).astype(BF16)
    p = jnp.dot(h_perm, wqkv_ref[...], preferred_element_type=F32)
    cw = conv_ref[...]

    def taps(groups):
        acc = groups[0] * cw[0:1]
        for tap in range(1, GDN_CONV_K):
            acc = acc + groups[tap] * cw[tap:tap + 1]
        return acc

    def grp(a):
        return p[8 * a:8 * (a + 1)]

    below = [pltpu.roll(grp(pitch - 2), 1, 0), pltpu.roll(grp(pitch - 1), 1, 0)]
    above = pltpu.roll(grp(0), 7, 0)
    mid = taps([p[8 * t:8 * (t + pitch - 3)] for t in range(GDN_CONV_K)])
    conv = jnp.concatenate([taps([below[0], below[1], grp(0), grp(1)]),
                            taps([below[1], grp(0), grp(1), grp(2)]),
                            mid,
                            taps([grp(pitch - 3), grp(pitch - 2), grp(pitch - 1), above])], axis=0)
    qkv = _silu(conv)
    for hd in range(GDN_HEADS):
        c0 = hd * lanes
        qh = qkv[:, c0:c0 + lanes]
        kh = qkv[:, GDN_W + c0:GDN_W + c0 + lanes]
        qn = qh * lax.rsqrt(jnp.sum(qh * qh, axis=-1, keepdims=True) + L2_EPS) * (GDN_HEAD_DIM ** -0.5)
        kn = kh * lax.rsqrt(jnp.sum(kh * kh, axis=-1, keepdims=True) + L2_EPS)
        for slab, val, out in ((hd, qn, q_out), (GDN_HEADS + hd, kn, k_out),
                               (2 * GDN_HEADS + hd, qkv[:, 2 * GDN_W + c0:2 * GDN_W + c0 + lanes], v_out)):
            for a in range(pitch):
                ys_ref[slab, pl.ds(a, 8, stride=pitch), :] = val[8 * a:8 * (a + 1)]
            out[0, hd] = ys_ref[slab, CONV_HALO:CONV_HALO + TOK_TILE, :].astype(BF16)
    sz_out[0] = _silu(jnp.dot(h_main, wz_ref[...], preferred_element_type=F32)).astype(BF16)

    nh2 = 2 * GDN_HEADS
    ab = jnp.dot(h_main, wab_ref[...], preferred_element_type=F32)
    gdec = -jnp.exp(alog_ref[...]) * jax.nn.softplus(ab[:, :nh2] + dtb_ref[...])
    gb_out[0] = jnp.concatenate([gdec, jax.nn.sigmoid(ab[:, nh2:])], axis=-1)
    abt = lax.dot_general(wabt_ref[...], h_main, _NT, preferred_element_type=F32)
    gdec_t = -jnp.exp(alogt_ref[...]) * jax.nn.softplus(abt[:nh2] + dtbt_ref[...])
    gbt_out[0] = jnp.concatenate([gdec_t, jax.nn.sigmoid(abt[nh2:])], axis=0)


def _gdn_proj(xa, modl, g0, wqkv, wz, wab, wab_t, conv_w, a_log, dt_bias):
    b, t, d = xa.shape
    nt = t // TOK_TILE
    per = TOK_TILE // CONV_HALO
    last = t // CONV_HALO - 1
    nh2 = 2 * GDN_HEADS
    head_out = pl.BlockSpec((1, GDN_HEADS, TOK_TILE, GDN_HEAD_DIM), lambda i, j: (i, 0, j, 0))
    head_shape = jax.ShapeDtypeStruct((b, GDN_HEADS, t, GDN_HEAD_DIM), BF16)
    return pl.pallas_call(
        _gdn_proj_body,
        grid=(b, nt),
        in_specs=[pl.BlockSpec((1, CONV_HALO, d), lambda i, j: (i, jnp.maximum(j * per - 1, 0), 0)),
                  pl.BlockSpec((1, TOK_TILE, d), lambda i, j: (i, j, 0)),
                  pl.BlockSpec((1, CONV_HALO, d), lambda i, j: (i, jnp.minimum((j + 1) * per, last), 0)),
                  pl.BlockSpec((1, 1, 6, d), lambda i, j: (i, jnp.minimum(j, 1), 0, 0)),
                  _const_spec((1, d)),
                  _const_spec(wqkv.shape), _const_spec(wz.shape), _const_spec(wab.shape),
                  _const_spec(wab_t.shape), _const_spec(conv_w.shape),
                  _const_spec((1, nh2)), _const_spec((1, nh2)),
                  _const_spec((nh2, 1)), _const_spec((nh2, 1))],
        out_specs=[head_out, head_out, head_out,
                   pl.BlockSpec((1, TOK_TILE, d), lambda i, j: (i, j, 0)),
                   pl.BlockSpec((1, TOK_TILE, 2 * nh2), lambda i, j: (i, j, 0)),
                   pl.BlockSpec((1, 2 * nh2, TOK_TILE), lambda i, j: (i, 0, j))],
        out_shape=[head_shape, head_shape, head_shape,
                   jax.ShapeDtypeStruct((b, t, d), BF16),
                   jax.ShapeDtypeStruct((b, t, 2 * nh2), F32),
                   jax.ShapeDtypeStruct((b, 2 * nh2, t), F32)],
        scratch_shapes=[pltpu.VMEM((d // GDN_HEAD_DIM, TOK_TILE + 2 * CONV_HALO, GDN_HEAD_DIM), F32),
                        pltpu.VMEM((3 * GDN_HEADS, TOK_TILE + 2 * CONV_HALO, GDN_HEAD_DIM), F32)],
        compiler_params=_params(("parallel", "parallel")),
        name="gdn_in_proj",
    )(xa, xa, xa, modl, g0, wqkv, wz, wab, wab_t, conv_w,
      a_log.reshape(1, nh2), dt_bias.reshape(1, nh2), a_log.reshape(nh2, 1), dt_bias.reshape(nh2, 1))


def _unit_tri_inverses(ms, lowers):
    n = ms[0].shape[0]
    row = lax.broadcasted_iota(jnp.int32, (n, n), 0)
    col = lax.broadcasted_iota(jnp.int32, (n, n), 1)
    t_offs = None
    k = 1
    while k < n:
        same = (row ^ col) < 2 * k
        joins = {True: same & ((row & k) != 0) & ((col & k) == 0),
                 False: same & ((col & k) != 0) & ((row & k) == 0)}
        parts = [jnp.where(joins[lo], m, 0.0) for m, lo in zip(ms, lowers)]
        if t_offs is None:
            t_offs = [-a for a in parts]
        else:
            tbs = [t.astype(BF16) for t in t_offs]
            zs = [a + jnp.dot(tb, a.astype(BF16), preferred_element_type=F32) for a, tb in zip(parts, tbs)]
            t_offs = [t - z - jnp.dot(z.astype(BF16), tb, preferred_element_type=F32)
                      for t, z, tb in zip(t_offs, zs, tbs)]
        k *= 2
    eye = jnp.where(row == col, 1.0, 0.0)
    return [t + eye for t in t_offs]


def _gdn_scan_body(qf, kf, vf, qb, kb, vb, gbf, gbb, gtf, gtb, of, ob, s_ref):
    @pl.when(pl.program_id(1) == 0)
    def _():
        s_ref[...] = jnp.zeros_like(s_ref)

    c = GDN_CHUNK
    row = lax.broadcasted_iota(jnp.int32, (c, c), 0)
    col = lax.broadcasted_iota(jnp.int32, (c, c), 1)
    lower = jnp.where(row >= col, 1.0, 0.0)
    upper = jnp.where(row <= col, 1.0, 0.0)
    nh2 = 2 * GDN_HEADS
    hi = lax.Precision.HIGHEST
    n_sub = qf.shape[2] // c
    for sub in range(n_sub):
        _gdn_scan_chunk(sub, n_sub - 1 - sub, c, row, col, lower, upper, nh2, hi,
                        qf, kf, vf, qb, kb, vb, gbf, gbb, gtf, gtb, of, ob, s_ref)


def _gdn_scan_chunk(sub_f, sub_b, c, row, col, lower, upper, nh2, hi,
                    qf, kf, vf, qb, kb, vb, gbf, gbb, gtf, gtb, of, ob, s_ref):
    chains = []
    for direction, (q_ref, k_ref, v_ref, gb_ref, gt_ref, o_ref, sub) in enumerate(
            ((qf, kf, vf, gbf, gtf, of, sub_f), (qb, kb, vb, gbb, gtb, ob, sub_b))):
        r0 = sub * c
        gb = gb_ref[0, r0:r0 + c, :]
        gt = gt_ref[0, :, r0:r0 + c]
        tri_c, tri_r = (lower, upper) if direction == 0 else (upper, lower)
        gc = jnp.dot(tri_c, gb[:, :nh2], precision=hi, preferred_element_type=F32)
        gr = jnp.dot(gt[:nh2], tri_r, precision=hi, preferred_element_type=F32)
        incl = (row >= col) if direction == 0 else (row <= col)
        strict = (row > col) if direction == 0 else (row < col)
        for hd in range(GDN_HEADS):
            ch = direction * GDN_HEADS + hd
            gcol = gc[:, ch:ch + 1]
            grow = gr[ch:ch + 1, :]
            chains.append(dict(
                ch=ch, hd=hd, lower=direction == 0, strict=strict, o_ref=o_ref, r0=r0,
                k=k_ref[0, hd, r0:r0 + c, :], q=q_ref[0, hd, r0:r0 + c, :], v=v_ref[0, hd, r0:r0 + c, :],
                beta=gb[:, nh2 + ch:nh2 + ch + 1], gcol=gcol,
                tot=gcol[c - 1:c] if direction == 0 else gcol[0:1],
                decay=jnp.where(incl, jnp.exp(jnp.where(incl, gcol - grow, 0.0)), 0.0)))

    for w in chains:
        w["gram"] = lax.dot_general(jnp.concatenate([w["k"], w["q"]], axis=0), w["k"], _NT,
                                    preferred_element_type=F32)
    ms = [jnp.where(w["strict"], w["beta"] * w["gram"][:c] * w["decay"], 0.0) for w in chains]
    tmats = _unit_tri_inverses(ms, [w["lower"] for w in chains])
    for w in chains:
        kf32 = w["k"].astype(F32)
        egc = jnp.exp(w["gcol"])
        lhs = jnp.concatenate([(kf32 * (w["beta"] * egc)).astype(BF16),
                               (w["q"].astype(F32) * egc).astype(BF16)], axis=0)
        w["state"] = s_ref[w["ch"]]
        w["ks"] = jnp.dot(lhs, w["state"].astype(BF16), preferred_element_type=F32)
        w["kdec"] = (kf32 * jnp.exp(w["tot"] - w["gcol"])).astype(BF16)
    for w, tmat in zip(chains, tmats):
        resid = w["v"].astype(F32) * w["beta"] - w["ks"][:c]
        w["v_new"] = jnp.dot(tmat.astype(BF16), resid.astype(BF16),
                             preferred_element_type=F32).astype(BF16)
    for w in chains:
        attn = (w["gram"][c:] * w["decay"]).astype(BF16)
        o = w["ks"][c:] + jnp.dot(attn, w["v_new"], preferred_element_type=F32)
        w["o_ref"][0, w["hd"], w["r0"]:w["r0"] + c, :] = o.astype(w["o_ref"].dtype)
    for w in chains:
        s_ref[w["ch"]] = (w["state"] * jnp.exp(w["tot"])
                          + lax.dot_general(w["kdec"], w["v_new"], _TN, preferred_element_type=F32))


def _gdn_scan(q, k, v, gb, gbt):
    b, nh, t, dh = q.shape
    nt = t // TOK_TILE
    nh2 = 2 * GDN_HEADS

    def bwd(s):
        return jnp.where(s == 0, 0, nt - s)

    head_f = pl.BlockSpec((1, nh, TOK_TILE, dh), lambda i, s: (i, 0, s, 0))
    head_b = pl.BlockSpec((1, nh, TOK_TILE, dh), lambda i, s: (i, 0, bwd(s), 0))
    out_shape = jax.ShapeDtypeStruct((b, nh, t, dh), BF16)
    return pl.pallas_call(
        _gdn_scan_body,
        grid=(b, nt),
        in_specs=[head_f, head_f, head_f, head_b, head_b, head_b,
                  pl.BlockSpec((1, TOK_TILE, 2 * nh2), lambda i, s: (i, s, 0)),
                  pl.BlockSpec((1, TOK_TILE, 2 * nh2), lambda i, s: (i, bwd(s), 0)),
                  pl.BlockSpec((1, 2 * nh2, TOK_TILE), lambda i, s: (i, 0, s)),
                  pl.BlockSpec((1, 2 * nh2, TOK_TILE), lambda i, s: (i, 0, bwd(s)))],
        out_specs=[head_f, head_b],
        out_shape=[out_shape, out_shape],
        scratch_shapes=[pltpu.VMEM((nh2, dh, dh), F32)],
        compiler_params=_params(("arbitrary", "arbitrary")),
        name="gdn_chunk_scan",
    )(q, k, v, q, k, v, gb, gb, gbt, gbt)


def _rope_tables(n_lat):
    rows = n_lat // GRID_W
    row_ids = jnp.repeat(jnp.arange(rows, dtype=F32), GRID_W)[:n_lat]
    col_ids = jnp.tile(jnp.arange(GRID_W, dtype=F32), rows)[:n_lat]
    axis_dim = HEAD_DIM // 2
    inv_freq = ROPE_THETA ** (-jnp.arange(0, axis_dim, 2, dtype=F32) / axis_dim)
    ang_r = row_ids[:, None] * inv_freq
    ang_c = col_ids[:, None] * inv_freq
    ang = jnp.concatenate([ang_r, ang_r, ang_c, ang_c], axis=-1)
    cos = jnp.concatenate([jnp.ones((CTX_LEN, HEAD_DIM), F32), jnp.cos(ang)], axis=0)
    sin = jnp.concatenate([jnp.zeros((CTX_LEN, HEAD_DIM), F32), jnp.sin(ang)], axis=0)
    sign = jnp.tile(jnp.repeat(jnp.array([-1.0, 1.0], F32), HEAD_DIM // 4), 2)
    return cos.T, (sin * sign).T


def kernel(x, c, ctx, c_ctx, ada_w, ada_b, norm_g, attn_w_in, attn_w_out, diff_lambda, diff_subln_g,
           gqa_qk_g, gdn_w_in, gdn_conv_w, gdn_a_log, gdn_dt_bias, gdn_norm_g, gdn_w_out,
           ffn_w_gate_up, ffn_w_down):
    b, n_lat, d = x.shape
    depth = ada_w.shape[0]
    assert d == D_MODEL and ctx.shape[1] == CTX_LEN and n_lat % TOK_TILE == 0
    t = CTX_LEN + n_lat
    xa = jnp.concatenate([ctx, x], axis=1)

    rows = -(-(b + 1) // 8) * 8
    cc = jnp.concatenate([c, c_ctx[None], jnp.zeros((rows - b - 1, d), F32)], axis=0)
    mods = _modulation(cc, ada_w, ada_b)
    cos_t, sin_t = _rope_tables(n_lat)
    nh2 = 2 * GDN_HEADS

    for l in range(depth):
        i = l // 2
        last = l == depth - 1
        ml = mods[l]
        modl = jnp.stack([jnp.broadcast_to(ml[b].reshape(1, 6, d), (b, 6, d)),
                          ml[:b].reshape(b, 6, d)], axis=1)
        g = norm_g[l]
        wgu = ffn_w_gate_up[l].astype(BF16)
        wd = ffn_w_down[l].astype(BF16)
        if l % 2 == 0:
            lam_init = 0.8 - 0.6 * math.exp(-0.3 * l)
            w = attn_w_in[i]
            o_dv = 2 * DIFF_QK_W
            o_gq = o_dv + DIFF_V_W
            o_gk = o_gq + GQA_Q_W
            o_gv = o_gk + GQA_KV_W
            w_t = jnp.concatenate([w[:, :DIFF_QK_W], w[:, o_gq:o_gk],
                                   w[:, DIFF_QK_W:o_dv], w[:, o_gk:o_gv],
                                   w[:, o_dv:o_gq], w[:, o_gv:]],
                                  axis=1).T.astype(BF16)
            qt, k, vt = _attn_proj(xa, modl, g[0:1], w_t, cos_t, sin_t, gqa_qk_g[i].reshape(2, HEAD_DIM, 1))
            o = _attention(qt, k, vt, diff_lambda[i], diff_subln_g[i].reshape(DIFF_V_DIM, 1), lam_init)
            xa = _post_ffn(xa, o, modl, g, attn_w_out[i].astype(BF16), wgu, wd, gdn=False, latent_only=last)
        else:
            w = gdn_w_in[i]
            wab = w[:, 4 * GDN_W:]
            q, k, v, sz, gb, gbt = _gdn_proj(
                xa, modl, g[0:1], w[:, :3 * GDN_W].astype(BF16), w[:, 3 * GDN_W:4 * GDN_W].astype(BF16),
                wab.astype(BF16), wab.T.astype(BF16), gdn_conv_w[i], gdn_a_log[i], gdn_dt_bias[i])
            of, ob = _gdn_scan(q, k, v, gb, gbt)
            xa = _post_ffn(xa, (of, ob, sz, gdn_norm_g[i].reshape(1, GDN_HEAD_DIM)), modl, g,
                           gdn_w_out[i].astype(BF16), wgu, wd, gdn=True, latent_only=last)
    return xa
```

```python
import functools
import math

import jax
import jax.numpy as jnp
from jax import lax
from jax.experimental import pallas as pl
from jax.experimental.pallas import tpu as pltpu

F32 = jnp.float32
BF16 = jnp.bfloat16

D_MODEL = 1024
CTX_LEN = 256
GRID_W = 64
RMS_EPS = 1e-6
L2_EPS = 1e-6
ROPE_THETA = 10000.0

HEAD_DIM = 64
ATTN_SCALE = HEAD_DIM ** -0.5
DIFF_HEADS = 4
DIFF_V_DIM = 2 * HEAD_DIM
GQA_Q_HEADS = 8
GQA_KV_HEADS = 2
GQA_REP = GQA_Q_HEADS // GQA_KV_HEADS
DIFF_QK_W = DIFF_HEADS * 2 * HEAD_DIM
DIFF_V_W = DIFF_HEADS * DIFF_V_DIM
GQA_Q_W = GQA_Q_HEADS * HEAD_DIM
GQA_KV_W = GQA_KV_HEADS * HEAD_DIM
Q_W = DIFF_QK_W + GQA_Q_W
K_W = DIFF_QK_W + GQA_KV_W
V_W = DIFF_V_W + GQA_KV_W
SUM_ROWS = 16
VT_ROWS = V_W + (DIFF_HEADS + GQA_KV_HEADS) * SUM_ROWS
LOG2_E = math.log2(math.e)
N_MAPS = 2 * DIFF_HEADS + GQA_Q_HEADS
QK_AHEAD = 6

GDN_HEADS = 8
GDN_HEAD_DIM = 128
GDN_W = GDN_HEADS * GDN_HEAD_DIM
GDN_CONV_K = 4
GDN_CONV_LEFT = 2
GDN_CHUNK = 64
LOCAL_STAGES_PER_STATE_STAGE = 3
CONV_HALO = 8

FFN_HIDDEN = 2816
FFN_CHUNK = 256
FFN_GROUP = 2

TOK_TILE = 256
KV_CHUNK = 256
VMEM_LIMIT = 56 * 1024 * 1024

_NT = (((1,), (1,)), ((), ()))
_TN = (((0,), (0,)), ((), ()))


def _rms(x, g):
    return x * lax.rsqrt(jnp.mean(x * x, axis=-1, keepdims=True) + RMS_EPS) * g


def _silu(x):
    return x * jax.nn.sigmoid(x)


def _params(sem):
    return pltpu.CompilerParams(dimension_semantics=sem, vmem_limit_bytes=VMEM_LIMIT)


def _const_spec(shape):
    n = len(shape)
    return pl.BlockSpec(shape, lambda *_: (0,) * n)


def _mod_body(c_ref, w_ref, b_ref, o_ref):
    sc = _silu(c_ref[...])
    o_ref[0] = jnp.dot(sc, w_ref[0], precision=lax.Precision.HIGHEST,
                       preferred_element_type=F32) + b_ref[0]


def _modulation(cc, ada_w, ada_b):
    depth, d, w6 = ada_w.shape
    rows = cc.shape[0]
    nblk = w6 // d
    return pl.pallas_call(
        _mod_body,
        grid=(depth, nblk),
        in_specs=[pl.BlockSpec((rows, d), lambda l, j: (0, 0)),
                  pl.BlockSpec((1, d, d), lambda l, j: (l, 0, j)),
                  pl.BlockSpec((1, 1, d), lambda l, j: (l, 0, j))],
        out_specs=pl.BlockSpec((1, rows, d), lambda l, j: (l, 0, j)),
        out_shape=jax.ShapeDtypeStruct((depth, rows, w6), F32),
        compiler_params=_params(("parallel", "parallel")),
        name="adaln_mod",
    )(cc, ada_w, ada_b.reshape(depth, 1, w6))


def _attn_proj_body(x_ref, mod_ref, g_ref, w_ref, cos_ref, sin_ref, qkg_ref, qt_out, k_out, vt_out):
    x = x_ref[0]
    mod = mod_ref[0, 0]
    h = (_rms(x, g_ref[...]) * (1.0 + mod[1:2]) + mod[0:1]).astype(BF16)
    p = lax.dot_general(w_ref[...], h, _NT, preferred_element_type=F32)
    ones = jnp.ones((SUM_ROWS, p.shape[1]), F32)
    v_blocks = []
    for r0, width in ([(hd * DIFF_V_DIM, DIFF_V_DIM) for hd in range(DIFF_HEADS)]
                      + [(DIFF_V_W + grp * HEAD_DIM, HEAD_DIM) for grp in range(GQA_KV_HEADS)]):
        v_blocks += [p[Q_W + K_W + r0:Q_W + K_W + r0 + width], ones]
    vt_out[0, 0] = jnp.concatenate(v_blocks, axis=0).astype(BF16)
    cos = cos_ref[...]
    sin = sin_ref[...]

    def rope(xh):
        swapped = jnp.concatenate([xh[16:32], xh[0:16], xh[48:64], xh[32:48]], axis=0)
        return xh * cos + swapped * sin

    def norm(xh, g):
        r = lax.rsqrt(jnp.mean(xh * xh, axis=0, keepdims=True) + RMS_EPS)
        return xh * r * g

    gq_g = qkg_ref[0]
    gk_g = qkg_ref[1]
    for j in range(Q_W // HEAD_DIM):
        r0 = j * HEAD_DIM
        xh = p[r0:r0 + HEAD_DIM]
        if r0 >= DIFF_QK_W:
            xh = norm(xh, gq_g)
        qt_out[0, 0, r0:r0 + HEAD_DIM, :] = (rope(xh) * (ATTN_SCALE * LOG2_E)).astype(BF16)
    k_heads = []
    for j in range(K_W // HEAD_DIM):
        r0 = j * HEAD_DIM
        xh = p[Q_W + r0:Q_W + r0 + HEAD_DIM]
        if r0 >= DIFF_QK_W:
            xh = norm(xh, gk_g)
        k_heads.append(rope(xh))
    k_out[0] = jnp.concatenate(k_heads, axis=0).T.astype(BF16)


def _attn_proj(xa, modl, g0, w_t, cos_t, sin_t, qkg):
    b, t, d = xa.shape
    nt = t // TOK_TILE
    return pl.pallas_call(
        _attn_proj_body,
        grid=(b, nt),
        in_specs=[pl.BlockSpec((1, TOK_TILE, d), lambda i, j: (i, j, 0)),
                  pl.BlockSpec((1, 1, 6, d), lambda i, j: (i, jnp.minimum(j, 1), 0, 0)),
                  _const_spec((1, d)),
                  _const_spec(w_t.shape),
                  pl.BlockSpec((HEAD_DIM, TOK_TILE), lambda i, j: (0, j)),
                  pl.BlockSpec((HEAD_DIM, TOK_TILE), lambda i, j: (0, j)),
                  _const_spec((2, HEAD_DIM, 1))],
        out_specs=[pl.BlockSpec((1, 1, Q_W, TOK_TILE), lambda i, j: (i, j, 0, 0)),
                   pl.BlockSpec((1, TOK_TILE, K_W), lambda i, j: (i, j, 0)),
                   pl.BlockSpec((1, 1, VT_ROWS, TOK_TILE), lambda i, j: (i, j, 0, 0))],
        out_shape=[jax.ShapeDtypeStruct((b, nt, Q_W, TOK_TILE), BF16),
                   jax.ShapeDtypeStruct((b, t, K_W), BF16),
                   jax.ShapeDtypeStruct((b, nt, VT_ROWS, TOK_TILE), BF16)],
        compiler_params=_params(("parallel", "parallel")),
        name="attn_in_proj",
    )(xa, modl, g0, w_t, cos_t, sin_t, qkg)


def _attn_body(lam_init, qt_ref, k_ref, vt_ref, lam_ref, subln_ref, o_ref, qpad_ref, m_ref, acc_ref):
    qi = pl.program_id(1)
    slab = 2 * HEAD_DIM
    diff_blk = DIFF_V_DIM + SUM_ROWS
    gqa_blk = HEAD_DIM + SUM_ROWS
    maps = []
    for hd in range(DIFF_HEADS):
        for mm in range(2):
            maps.append((2 * hd + mm, hd, hd * diff_blk, diff_blk))
    for hq in range(GQA_Q_HEADS):
        grp = hq // GQA_REP
        maps.append((2 * DIFF_HEADS + hq, DIFF_HEADS, DIFF_HEADS * diff_blk + grp * gqa_blk, gqa_blk))

    zeros = jnp.zeros((HEAD_DIM, qt_ref.shape[3]), BF16)
    for i, (qh, _, _, _) in enumerate(maps):
        half = (qh % 2) if qh < 2 * DIFF_HEADS else (qh - 2 * DIFF_HEADS) // GQA_REP
        qh_t = qt_ref[0, 0, qh * HEAD_DIM:(qh + 1) * HEAD_DIM, :]
        qpad_ref[i] = jnp.concatenate([qh_t, zeros] if half == 0 else [zeros, qh_t], axis=0)
    def scores(c, i):
        ks = maps[i][1]
        return jnp.dot(k_ref[0, c * KV_CHUNK:(c + 1) * KV_CHUNK, ks * slab:(ks + 1) * slab], qpad_ref[i],
                       preferred_element_type=F32)

    def run(n_chunks):
        items = [(c, i) for c in range(n_chunks) for i in range(N_MAPS)]
        pending = [scores(*items[n]) for n in range(QK_AHEAD)]
        for n, (c, i) in enumerate(items):
            s = pending.pop(0)
            if n + QK_AHEAD < len(items):
                pending.append(scores(*items[n + QK_AHEAD]))
            _, _, v_row0, v_w = maps[i]
            m_new = jnp.max(s, axis=0, keepdims=True)
            if c > 0:
                m_old = m_ref[i]
                m_new = jnp.maximum(m_old, m_new)
                alpha = jnp.exp2(m_old - m_new)
            m_ref[i] = m_new
            p = jnp.exp2(s - m_new)
            pv = jnp.dot(vt_ref[0, c, v_row0:v_row0 + v_w, :], p.astype(BF16), preferred_element_type=F32)
            acc_ref[i, 0:v_w, :] = pv if c == 0 else alpha * acc_ref[i, 0:v_w, :] + pv

    @pl.when(qi == 0)
    def _():
        run(CTX_LEN // KV_CHUNK)

    @pl.when(qi > 0)
    def _():
        run(vt_ref.shape[1])

    lv = lam_ref[...]
    lam = (jnp.exp(jnp.sum(lv[0:1] * lv[1:2], axis=-1, keepdims=True))
           - jnp.exp(jnp.sum(lv[2:3] * lv[3:4], axis=-1, keepdims=True)) + lam_init)
    def normalized(i, width):
        return acc_ref[i, 0:width, :] / acc_ref[i, width:width + 1, :]

    for hd in range(DIFF_HEADS):
        od = normalized(2 * hd, DIFF_V_DIM) - lam * normalized(2 * hd + 1, DIFF_V_DIM)
        od = od * lax.rsqrt(jnp.mean(od * od, axis=0, keepdims=True) + RMS_EPS) * subln_ref[...]
        o_ref[0, :, hd * DIFF_V_DIM:(hd + 1) * DIFF_V_DIM] = (od * (1.0 - lam_init)).T.astype(o_ref.dtype)
    for pair in range(GQA_Q_HEADS // 2):
        i0 = 2 * DIFF_HEADS + 2 * pair
        og = jnp.concatenate([normalized(i0, HEAD_DIM), normalized(i0 + 1, HEAD_DIM)], axis=0)
        c0 = DIFF_V_W + pair * slab
        o_ref[0, :, c0:c0 + slab] = og.T.astype(o_ref.dtype)


def _attention(qt, k, vt, lam_vec, subln_g, lam_init):
    b, t, _ = k.shape
    nq = t // TOK_TILE
    return pl.pallas_call(
        functools.partial(_attn_body, lam_init),
        grid=(b, nq),
        in_specs=[pl.BlockSpec((1, 1, Q_W, TOK_TILE), lambda i, j: (i, j, 0, 0)),
                  pl.BlockSpec((1, t, K_W), lambda i, j: (i, 0, 0)),
                  pl.BlockSpec((1,) + vt.shape[1:], lambda i, j: (i, 0, 0, 0)),
                  _const_spec((4, HEAD_DIM)),
                  _const_spec((DIFF_V_DIM, 1))],
        out_specs=pl.BlockSpec((1, TOK_TILE, D_MODEL), lambda i, j: (i, j, 0)),
        out_shape=jax.ShapeDtypeStruct((b, t, D_MODEL), BF16),
        scratch_shapes=[pltpu.VMEM((N_MAPS, 2 * HEAD_DIM, TOK_TILE), BF16),
                        pltpu.VMEM((N_MAPS, 1, TOK_TILE), F32),
                        pltpu.VMEM((N_MAPS, DIFF_V_DIM + SUM_ROWS, TOK_TILE), F32)],
        compiler_params=_params(("parallel", "parallel")),
        name="diff_gqa_attention",
    )(qt, k, vt, lam_vec, subln_g)


def _post_body(gdn, group, *refs):
    per = 5 if gdn else 3
    tiles = [refs[per * m:per * (m + 1)] for m in range(group)]
    rest = refs[per * group:]
    if gdn:
        ng_ref, rest = rest[0], rest[1:]
    g_ref, wo_ref, wgu_ref, wd_ref, xo_ref, a_ref = rest
    g = g_ref[...]

    def mixer_out(tile):
        if not gdn:
            return tile[1][0]
        _, of_ref, ob_ref, sz_ref, _ = tile
        parts = []
        for hd in range(GDN_HEADS):
            o = of_ref[0, hd].astype(F32) + ob_ref[0, hd].astype(F32)
            parts.append(_rms(o, ng_ref[...]))
        return (jnp.concatenate(parts, axis=-1) * sz_ref[0].astype(F32)).astype(BF16)

    ys = [jnp.dot(mixer_out(tile), wo_ref[...], preferred_element_type=F32) for tile in tiles]
    for m, (tile, y) in enumerate(zip(tiles, ys)):
        mod = tile[-1][0, 0]
        x = tile[0][0] + mod[2:3] * _rms(y, g[1:2])
        h = (_rms(x, g[2:3]) * (1.0 + mod[4:5]) + mod[3:4]).astype(BF16)
        for c in range(FFN_HIDDEN // FFN_CHUNK):
            c0 = c * FFN_CHUNK
            gate = jnp.dot(h, wgu_ref[:, c0:c0 + FFN_CHUNK], preferred_element_type=F32)
            up = jnp.dot(h, wgu_ref[:, FFN_HIDDEN + c0:FFN_HIDDEN + c0 + FFN_CHUNK],
                         preferred_element_type=F32)
            a_ref[m, :, c0:c0 + FFN_CHUNK] = (_silu(gate) * up).astype(BF16)
        ff = jnp.dot(a_ref[m], wd_ref[...], preferred_element_type=F32)
        xo_ref[0, m * TOK_TILE:(m + 1) * TOK_TILE, :] = x + mod[5:6] * _rms(ff, g[3:4])


def _post_ffn(xa, mixer_out, modl, g, wo, wgu, wd, gdn, latent_only):
    b, t, d = xa.shape
    skip = CTX_LEN // TOK_TILE if latent_only else 0
    nt = t // TOK_TILE - skip
    group = FFN_GROUP
    assert (b * nt) % group == 0

    def at(m, fn):
        def index_map(p):
            n = p * group + m
            return fn(n // nt, n % nt + skip)
        return index_map

    in_specs, args = [], []
    for m in range(group):
        tile = pl.BlockSpec((1, TOK_TILE, d), at(m, lambda i, j: (i, j, 0)))
        if gdn:
            of, ob, sz, ng = mixer_out
            head_tile = pl.BlockSpec((1, GDN_HEADS, TOK_TILE, GDN_HEAD_DIM), at(m, lambda i, j: (i, 0, j, 0)))
            in_specs += [tile, head_tile, head_tile, tile]
            args += [xa, of, ob, sz]
        else:
            in_specs += [tile, tile]
            args += [xa, mixer_out]
        in_specs.append(pl.BlockSpec((1, 1, 6, d), at(m, lambda i, j: (i, jnp.minimum(j, 1), 0, 0))))
        args.append(modl)
    if gdn:
        in_specs.append(_const_spec((1, GDN_HEAD_DIM)))
        args.append(ng)
    in_specs += [_const_spec((4, d)), _const_spec(wo.shape), _const_spec(wgu.shape), _const_spec(wd.shape)]
    args += [g, wo, wgu, wd]
    rows = group * TOK_TILE
    out = pl.pallas_call(
        functools.partial(_post_body, gdn, group),
        grid=(b * nt // group,),
        in_specs=in_specs,
        out_specs=pl.BlockSpec((1, rows, d), lambda p: (p, 0, 0)),
        out_shape=jax.ShapeDtypeStruct((b * nt // group, rows, d), F32),
        scratch_shapes=[pltpu.VMEM((group, TOK_TILE, FFN_HIDDEN), BF16)],
        compiler_params=_params(("parallel",)),
        name="out_proj_ffn_gdn" if gdn else "out_proj_ffn_attn",
    )(*args)
    return out.reshape(b, nt * TOK_TILE, d)


def _gdn_proj_body(xp_ref, x_ref, xn_ref, mod_ref, g_ref, wqkv_ref, wz_ref, wab_ref, wabt_ref,
                   conv_ref, alog_ref, dtb_ref, alogt_ref, dtbt_ref,
                   q_out, k_out, v_out, sz_out, gb_out, gbt_out, xs_ref, ys_ref):
    j = pl.program_id(1)
    nt = pl.num_programs(1)
    mod = mod_ref[0, 0]
    g = g_ref[...]
    lanes = GDN_HEAD_DIM
    rows = TOK_TILE + 2 * CONV_HALO
    pitch = rows // 8

    def prep(xv):
        return _rms(xv, g) * (1.0 + mod[1:2]) + mod[0:1]

    prev_ok = jnp.where(j >= 2, 1.0, 0.0)
    next_ok = jnp.where(jnp.logical_and(j >= 1, j < nt - 1), 1.0, 0.0)
    h_f32 = prep(x_ref[0])
    h_main = h_f32.astype(BF16)
    h_cat = jnp.concatenate([prep(xp_ref[0]) * prev_ok, h_f32, prep(xn_ref[0]) * next_ok], axis=0)
    n_slab = h_cat.shape[1] // lanes
    for s in range(n_slab):
        xs_ref[s] = h_cat[:, s * lanes:(s + 1) * lanes]
    h_perm = jnp.concatenate(
        [jnp.concatenate([xs_ref[s, pl.ds(a, 8, stride=pitch), :] for s in range(n_slab)], axis=1)
         for a in range(pitch)], axis=0).astype(BF16)
    p = jnp.dot(h_perm, wqkv_ref[...], preferred_element_type=F32)
    cw = conv_ref[...]

    def taps(groups):
        acc = groups[0] * cw[0:1]
        for tap in range(1, GDN_CONV_K):
            acc = acc + groups[tap] * cw[tap:tap + 1]
        return acc

    def grp(a):
        return p[8 * a:8 * (a + 1)]

    below = [pltpu.roll(grp(pitch - 2), 1, 0), pltpu.roll(grp(pitch - 1), 1, 0)]
    above = pltpu.roll(grp(0), 7, 0)
    mid = taps([p[8 * t:8 * (t + pitch - 3)] for t in range(GDN_CONV_K)])
    conv = jnp.concatenate([taps([below[0], below[1], grp(0), grp(1)]),
                            taps([below[1], grp(0), grp(1), grp(2)]),
                            mid,
                            taps([grp(pitch - 3), grp(pitch - 2), grp(pitch - 1), above])], axis=0)
    qkv = _silu(conv)
    for hd in range(GDN_HEADS):
        c0 = hd * lanes
        qh = qkv[:, c0:c0 + lanes]
        kh = qkv[:, GDN_W + c0:GDN_W + c0 + lanes]
        qn = qh * lax.rsqrt(jnp.sum(qh * qh, axis=-1, keepdims=True) + L2_EPS) * (GDN_HEAD_DIM ** -0.5)
        kn = kh * lax.rsqrt(jnp.sum(kh * kh, axis=-1, keepdims=True) + L2_EPS)
        for slab, val, out in ((hd, qn, q_out), (GDN_HEADS + hd, kn, k_out),
                               (2 * GDN_HEADS + hd, qkv[:, 2 * GDN_W + c0:2 * GDN_W + c0 + lanes], v_out)):
            for a in range(pitch):
                ys_ref[slab, pl.ds(a, 8, stride=pitch), :] = val[8 * a:8 * (a + 1)]
            out[0, hd] = ys_ref[slab, CONV_HALO:CONV_HALO + TOK_TILE, :].astype(BF16)
    sz_out[0] = _silu(jnp.dot(h_main, wz_ref[...], preferred_element_type=F32)).astype(BF16)

    nh2 = 2 * GDN_HEADS
    ab = jnp.dot(h_main, wab_ref[...], preferred_element_type=F32)
    gdec = -jnp.exp(alog_ref[...]) * jax.nn.softplus(ab[:, :nh2] + dtb_ref[...])
    gb_out[0] = jnp.concatenate([gdec, jax.nn.sigmoid(ab[:, nh2:])], axis=-1)
    abt = lax.dot_general(wabt_ref[...], h_main, _NT, preferred_element_type=F32)
    gdec_t = -jnp.exp(alogt_ref[...]) * jax.nn.softplus(abt[:nh2] + dtbt_ref[...])
    gbt_out[0] = jnp.concatenate([gdec_t, jax.nn.sigmoid(abt[nh2:])], axis=0)


def _gdn_proj(xa, modl, g0, wqkv, wz, wab, wab_t, conv_w, a_log, dt_bias):
    b, t, d = xa.shape
    nt = t // TOK_TILE
    per = TOK_TILE // CONV_HALO
    last = t // CONV_HALO - 1
    nh2 = 2 * GDN_HEADS
    head_out = pl.BlockSpec((1, GDN_HEADS, TOK_TILE, GDN_HEAD_DIM), lambda i, j: (i, 0, j, 0))
    head_shape = jax.ShapeDtypeStruct((b, GDN_HEADS, t, GDN_HEAD_DIM), BF16)
    return pl.pallas_call(
        _gdn_proj_body,
        grid=(b, nt),
        in_specs=[pl.BlockSpec((1, CONV_HALO, d), lambda i, j: (i, jnp.maximum(j * per - 1, 0), 0)),
                  pl.BlockSpec((1, TOK_TILE, d), lambda i, j: (i, j, 0)),
                  pl.BlockSpec((1, CONV_HALO, d), lambda i, j: (i, jnp.minimum((j + 1) * per, last), 0)),
                  pl.BlockSpec((1, 1, 6, d), lambda i, j: (i, jnp.minimum(j, 1), 0, 0)),
                  _const_spec((1, d)),
                  _const_spec(wqkv.shape), _const_spec(wz.shape), _const_spec(wab.shape),
                  _const_spec(wab_t.shape), _const_spec(conv_w.shape),
                  _const_spec((1, nh2)), _const_spec((1, nh2)),
                  _const_spec((nh2, 1)), _const_spec((nh2, 1))],
        out_specs=[head_out, head_out, head_out,
                   pl.BlockSpec((1, TOK_TILE, d), lambda i, j: (i, j, 0)),
                   pl.BlockSpec((1, TOK_TILE, 2 * nh2), lambda i, j: (i, j, 0)),
                   pl.BlockSpec((1, 2 * nh2, TOK_TILE), lambda i, j: (i, 0, j))],
        out_shape=[head_shape, head_shape, head_shape,
                   jax.ShapeDtypeStruct((b, t, d), BF16),
                   jax.ShapeDtypeStruct((b, t, 2 * nh2), F32),
                   jax.ShapeDtypeStruct((b, 2 * nh2, t), F32)],
        scratch_shapes=[pltpu.VMEM((d // GDN_HEAD_DIM, TOK_TILE + 2 * CONV_HALO, GDN_HEAD_DIM), F32),
                        pltpu.VMEM((3 * GDN_HEADS, TOK_TILE + 2 * CONV_HALO, GDN_HEAD_DIM), F32)],
        compiler_params=_params(("parallel", "parallel")),
        name="gdn_in_proj",
    )(xa, xa, xa, modl, g0, wqkv, wz, wab, wab_t, conv_w,
      a_log.reshape(1, nh2), dt_bias.reshape(1, nh2), a_log.reshape(nh2, 1), dt_bias.reshape(nh2, 1))


def _gdn_scan_body(qf, kf, vf, qb, kb, vb, gbf, gbb, gtf, gtb, of, ob, s_ref):
    @pl.when(pl.program_id(1) == 0)
    def _():
        s_ref[...] = jnp.zeros_like(s_ref)

    c = GDN_CHUNK
    row = lax.broadcasted_iota(jnp.int32, (c, c), 0)
    col = lax.broadcasted_iota(jnp.int32, (c, c), 1)
    lower = jnp.where(row >= col, 1.0, 0.0)
    upper = jnp.where(row <= col, 1.0, 0.0)
    nh2 = 2 * GDN_HEADS
    hi = lax.Precision.HIGHEST
    n_sub = qf.shape[2] // c
    refs = (qf, kf, vf, qb, kb, vb, gbf, gbb, gtf, gtb, of, ob)
    chunks = [[] for _ in range(n_sub)]
    local = [_gdn_local_stages(chunks[sub], sub, n_sub - 1 - sub, c, row, col, lower, upper, nh2, hi, refs)
             for sub in range(n_sub)]
    carried = [_gdn_state_stages(chunks[sub], c, s_ref) for sub in range(n_sub)]
    for _ in local[0]:
        pass
    for sub in range(n_sub):
        ahead = local[sub + 1] if sub + 1 < n_sub else iter(())
        for _ in carried[sub]:
            for _ in range(LOCAL_STAGES_PER_STATE_STAGE):
                next(ahead, None)
        for _ in ahead:
            pass


def _gdn_chunk_chains(sub_f, sub_b, c, row, col, lower, upper, nh2, hi,
                      qf, kf, vf, qb, kb, vb, gbf, gbb, gtf, gtb, of, ob):
    chains = []
    for direction, (q_ref, k_ref, v_ref, gb_ref, gt_ref, o_ref, sub) in enumerate(
            ((qf, kf, vf, gbf, gtf, of, sub_f), (qb, kb, vb, gbb, gtb, ob, sub_b))):
        r0 = sub * c
        gb = gb_ref[0, r0:r0 + c, :]
        gt = gt_ref[0, :, r0:r0 + c]
        tri_c, tri_r = (lower, upper) if direction == 0 else (upper, lower)
        gc = jnp.dot(tri_c, gb[:, :nh2], precision=hi, preferred_element_type=F32)
        gr = jnp.dot(gt[:nh2], tri_r, precision=hi, preferred_element_type=F32)
        incl = (row >= col) if direction == 0 else (row <= col)
        strict = (row > col) if direction == 0 else (row < col)
        for hd in range(GDN_HEADS):
            ch = direction * GDN_HEADS + hd
            gcol = gc[:, ch:ch + 1]
            grow = gr[ch:ch + 1, :]
            chains.append(dict(
                ch=ch, hd=hd, lower=direction == 0, strict=strict, o_ref=o_ref, r0=r0,
                k=k_ref[0, hd, r0:r0 + c, :], q=q_ref[0, hd, r0:r0 + c, :], v=v_ref[0, hd, r0:r0 + c, :],
                beta=gb[:, nh2 + ch:nh2 + ch + 1], gcol=gcol,
                tot=gcol[c - 1:c] if direction == 0 else gcol[0:1],
                decay=jnp.where(incl, jnp.exp(jnp.where(incl, gcol - grow, 0.0)), 0.0)))

    return chains


def _gdn_local_stages(chains, sub_f, sub_b, c, row, col, lower, upper, nh2, hi, refs):
    chains.extend(_gdn_chunk_chains(sub_f, sub_b, c, row, col, lower, upper, nh2, hi, *refs))
    for w in chains:
        w["kq"] = jnp.concatenate([w["k"], w["q"]], axis=0)
        w["gram"] = lax.dot_general(w["kq"], w["k"], _NT, preferred_element_type=F32)
    yield
    ms = [jnp.where(w["strict"], w["beta"] * w["gram"][:c] * w["decay"], 0.0) for w in chains]
    t_offs = None
    k = 1
    while k < c:
        same = (row ^ col) < 2 * k
        joins = {True: same & ((row & k) != 0) & ((col & k) == 0),
                 False: same & ((col & k) != 0) & ((row & k) == 0)}
        parts = [jnp.where(joins[w["lower"]], m, 0.0) for m, w in zip(ms, chains)]
        if t_offs is None:
            t_offs = [-a for a in parts]
        else:
            tbs = [t.astype(BF16) for t in t_offs]
            zs = [a + jnp.dot(tb, a.astype(BF16), preferred_element_type=F32) for a, tb in zip(parts, tbs)]
            yield
            t_offs = [t - z - jnp.dot(z.astype(BF16), tb, preferred_element_type=F32)
                      for t, z, tb in zip(t_offs, zs, tbs)]
            yield
        k *= 2
    eye = jnp.where(row == col, 1.0, 0.0)
    for w, t in zip(chains, t_offs):
        w["tmat"] = (t + eye).astype(BF16)


def _gdn_state_stages(chains, c, s_ref):
    for w in chains:
        w["state"] = s_ref[w["ch"]]
        w["ks"] = jnp.dot(w["kq"], w["state"].astype(BF16), preferred_element_type=F32)
    yield
    for w in chains:
        egc = jnp.exp(w["gcol"])
        resid = (w["v"].astype(F32) - w["ks"][:c] * egc) * w["beta"]
        w["v_new"] = jnp.dot(w["tmat"], resid.astype(BF16), preferred_element_type=F32)
        w["v_new_b"] = w["v_new"].astype(BF16)
        w["qs"] = w["ks"][c:] * egc
    yield
    for w in chains:
        attn = (w["gram"][c:] * w["decay"]).astype(BF16)
        o = w["qs"] + jnp.dot(attn, w["v_new_b"], preferred_element_type=F32)
        w["o_ref"][0, w["hd"], w["r0"]:w["r0"] + c, :] = o.astype(w["o_ref"].dtype)
    yield
    for w in chains:
        v_dec = (w["v_new"] * jnp.exp(w["tot"] - w["gcol"])).astype(BF16)
        s_ref[w["ch"]] = (w["state"] * jnp.exp(w["tot"])
                          + lax.dot_general(w["k"], v_dec, _TN, preferred_element_type=F32))
    yield


def _gdn_scan(q, k, v, gb, gbt):
    b, nh, t, dh = q.shape
    nt = t // TOK_TILE
    nh2 = 2 * GDN_HEADS

    def bwd(s):
        return jnp.where(s == 0, 0, nt - s)

    head_f = pl.BlockSpec((1, nh, TOK_TILE, dh), lambda i, s: (i, 0, s, 0))
    head_b = pl.BlockSpec((1, nh, TOK_TILE, dh), lambda i, s: (i, 0, bwd(s), 0))
    out_shape = jax.ShapeDtypeStruct((b, nh, t, dh), BF16)
    return pl.pallas_call(
        _gdn_scan_body,
        grid=(b, nt),
        in_specs=[head_f, head_f, head_f, head_b, head_b, head_b,
                  pl.BlockSpec((1, TOK_TILE, 2 * nh2), lambda i, s: (i, s, 0)),
                  pl.BlockSpec((1, TOK_TILE, 2 * nh2), lambda i, s: (i, bwd(s), 0)),
                  pl.BlockSpec((1, 2 * nh2, TOK_TILE), lambda i, s: (i, 0, s)),
                  pl.BlockSpec((1, 2 * nh2, TOK_TILE), lambda i, s: (i, 0, bwd(s)))],
        out_specs=[head_f, head_b],
        out_shape=[out_shape, out_shape],
        scratch_shapes=[pltpu.VMEM((nh2, dh, dh), F32)],
        compiler_params=_params(("arbitrary", "arbitrary")),
        name="gdn_chunk_scan",
    )(q, k, v, q, k, v, gb, gb, gbt, gbt)


def _rope_tables(n_lat):
    rows = n_lat // GRID_W
    row_ids = jnp.repeat(jnp.arange(rows, dtype=F32), GRID_W)[:n_lat]
    col_ids = jnp.tile(jnp.arange(GRID_W, dtype=F32), rows)[:n_lat]
    axis_dim = HEAD_DIM // 2
    inv_freq = ROPE_THETA ** (-jnp.arange(0, axis_dim, 2, dtype=F32) / axis_dim)
    ang_r = row_ids[:, None] * inv_freq
    ang_c = col_ids[:, None] * inv_freq
    ang = jnp.concatenate([ang_r, ang_r, ang_c, ang_c], axis=-1)
    cos = jnp.concatenate([jnp.ones((CTX_LEN, HEAD_DIM), F32), jnp.cos(ang)], axis=0)
    sin = jnp.concatenate([jnp.zeros((CTX_LEN, HEAD_DIM), F32), jnp.sin(ang)], axis=0)
    sign = jnp.tile(jnp.repeat(jnp.array([-1.0, 1.0], F32), HEAD_DIM // 4), 2)
    return cos.T, (sin * sign).T


def kernel(x, c, ctx, c_ctx, ada_w, ada_b, norm_g, attn_w_in, attn_w_out, diff_lambda, diff_subln_g,
           gqa_qk_g, gdn_w_in, gdn_conv_w, gdn_a_log, gdn_dt_bias, gdn_norm_g, gdn_w_out,
           ffn_w_gate_up, ffn_w_down):
    b, n_lat, d = x.shape
    depth = ada_w.shape[0]
    assert d == D_MODEL and ctx.shape[1] == CTX_LEN and n_lat % TOK_TILE == 0
    t = CTX_LEN + n_lat
    xa = jnp.concatenate([ctx, x], axis=1)

    rows = -(-(b + 1) // 8) * 8
    cc = jnp.concatenate([c, c_ctx[None], jnp.zeros((rows - b - 1, d), F32)], axis=0)
    mods = _modulation(cc, ada_w, ada_b)
    cos_t, sin_t = _rope_tables(n_lat)
    nh2 = 2 * GDN_HEADS

    for l in range(depth):
        i = l // 2
        last = l == depth - 1
        ml = mods[l]
        modl = jnp.stack([jnp.broadcast_to(ml[b].reshape(1, 6, d), (b, 6, d)),
                          ml[:b].reshape(b, 6, d)], axis=1)
        g = norm_g[l]
        wgu = ffn_w_gate_up[l].astype(BF16)
        wd = ffn_w_down[l].astype(BF16)
        if l % 2 == 0:
            lam_init = 0.8 - 0.6 * math.exp(-0.3 * l)
            w = attn_w_in[i]
            o_dv = 2 * DIFF_QK_W
            o_gq = o_dv + DIFF_V_W
            o_gk = o_gq + GQA_Q_W
            o_gv = o_gk + GQA_KV_W
            w_t = jnp.concatenate([w[:, :DIFF_QK_W], w[:, o_gq:o_gk],
                                   w[:, DIFF_QK_W:o_dv], w[:, o_gk:o_gv],
                                   w[:, o_dv:o_gq], w[:, o_gv:]],
                                  axis=1).T.astype(BF16)
            qt, k, vt = _attn_proj(xa, modl, g[0:1], w_t, cos_t, sin_t, gqa_qk_g[i].reshape(2, HEAD_DIM, 1))
            o = _attention(qt, k, vt, diff_lambda[i], diff_subln_g[i].reshape(DIFF_V_DIM, 1), lam_init)
            xa = _post_ffn(xa, o, modl, g, attn_w_out[i].astype(BF16), wgu, wd, gdn=False, latent_only=last)
        else:
            w = gdn_w_in[i]
            wab = w[:, 4 * GDN_W:]
            q, k, v, sz, gb, gbt = _gdn_proj(
                xa, modl, g[0:1], w[:, :3 * GDN_W].astype(BF16), w[:, 3 * GDN_W:4 * GDN_W].astype(BF16),
                wab.astype(BF16), wab.T.astype(BF16), gdn_conv_w[i], gdn_a_log[i], gdn_dt_bias[i])
            of, ob = _gdn_scan(q, k, v, gb, gbt)
            xa = _post_ffn(xa, (of, ob, sz, gdn_norm_g[i].reshape(1, GDN_HEAD_DIM)), modl, g,
                           gdn_w_out[i].astype(BF16), wgu, wd, gdn=True, latent_only=last)
    return xa
```

```python
import functools
import math

import jax
import jax.numpy as jnp
from jax import lax
from jax.experimental import pallas as pl
from jax.experimental.pallas import tpu as pltpu

F32 = jnp.float32
BF16 = jnp.bfloat16

D_MODEL = 1024
CTX_LEN = 256
GRID_W = 64
RMS_EPS = 1e-6
L2_EPS = 1e-6
ROPE_THETA = 10000.0

HEAD_DIM = 64
ATTN_SCALE = HEAD_DIM ** -0.5
DIFF_HEADS = 4
DIFF_V_DIM = 2 * HEAD_DIM
GQA_Q_HEADS = 8
GQA_KV_HEADS = 2
GQA_REP = GQA_Q_HEADS // GQA_KV_HEADS
DIFF_QK_W = DIFF_HEADS * 2 * HEAD_DIM
DIFF_V_W = DIFF_HEADS * DIFF_V_DIM
GQA_Q_W = GQA_Q_HEADS * HEAD_DIM
GQA_KV_W = GQA_KV_HEADS * HEAD_DIM
Q_W = DIFF_QK_W + GQA_Q_W
K_W = DIFF_QK_W + GQA_KV_W
V_W = DIFF_V_W + GQA_KV_W
SUM_ROWS = 16
VT_ROWS = V_W + (DIFF_HEADS + GQA_KV_HEADS) * SUM_ROWS
LOG2_E = math.log2(math.e)
N_MAPS = 2 * DIFF_HEADS + GQA_Q_HEADS
QK_AHEAD = 6

GDN_HEADS = 8
GDN_HEAD_DIM = 128
GDN_W = GDN_HEADS * GDN_HEAD_DIM
GDN_CONV_K = 4
GDN_CONV_LEFT = 2
GDN_CHUNK = 64
LOCAL_STAGES_PER_STATE_STAGE = 3
CONV_HALO = 8

FFN_HIDDEN = 2816
FFN_CHUNK = 256
FFN_GROUP = 2

TOK_TILE = 256
KV_CHUNK = 256
VMEM_LIMIT = 56 * 1024 * 1024

_NT = (((1,), (1,)), ((), ()))
_TN = (((0,), (0,)), ((), ()))


def _rms(x, g):
    return x * lax.rsqrt(jnp.mean(x * x, axis=-1, keepdims=True) + RMS_EPS) * g


def _silu(x):
    return x * jax.nn.sigmoid(x)


def _params(sem):
    return pltpu.CompilerParams(dimension_semantics=sem, vmem_limit_bytes=VMEM_LIMIT)


def _const_spec(shape):
    n = len(shape)
    return pl.BlockSpec(shape, lambda *_: (0,) * n)


def _mod_body(c_ref, w_ref, b_ref, o_ref):
    sc = _silu(c_ref[...])
    o_ref[0] = jnp.dot(sc, w_ref[0], precision=lax.Precision.HIGHEST,
                       preferred_element_type=F32) + b_ref[0]


def _modulation(cc, ada_w, ada_b):
    depth, d, w6 = ada_w.shape
    rows = cc.shape[0]
    nblk = w6 // d
    return pl.pallas_call(
        _mod_body,
        grid=(depth, nblk),
        in_specs=[pl.BlockSpec((rows, d), lambda l, j: (0, 0)),
                  pl.BlockSpec((1, d, d), lambda l, j: (l, 0, j)),
                  pl.BlockSpec((1, 1, d), lambda l, j: (l, 0, j))],
        out_specs=pl.BlockSpec((1, rows, d), lambda l, j: (l, 0, j)),
        out_shape=jax.ShapeDtypeStruct((depth, rows, w6), F32),
        compiler_params=_params(("parallel", "parallel")),
        name="adaln_mod",
    )(cc, ada_w, ada_b.reshape(depth, 1, w6))


def _attn_proj_body(x_ref, mod_ref, g_ref, w_ref, cos_ref, sin_ref, qkg_ref, qt_out, k_out, vt_out):
    x = x_ref[0]
    mod = mod_ref[0, 0]
    h = (_rms(x, g_ref[...]) * (1.0 + mod[1:2]) + mod[0:1]).astype(BF16)
    p = lax.dot_general(w_ref[...], h, _NT, preferred_element_type=F32)
    ones = jnp.ones((SUM_ROWS, p.shape[1]), F32)
    v_blocks = []
    for r0, width in ([(hd * DIFF_V_DIM, DIFF_V_DIM) for hd in range(DIFF_HEADS)]
                      + [(DIFF_V_W + grp * HEAD_DIM, HEAD_DIM) for grp in range(GQA_KV_HEADS)]):
        v_blocks += [p[Q_W + K_W + r0:Q_W + K_W + r0 + width], ones]
    vt_out[0, 0] = jnp.concatenate(v_blocks, axis=0).astype(BF16)
    cos = cos_ref[...]
    sin = sin_ref[...]

    def rope(xh):
        swapped = jnp.concatenate([xh[16:32], xh[0:16], xh[48:64], xh[32:48]], axis=0)
        return xh * cos + swapped * sin

    def norm(xh, g):
        r = lax.rsqrt(jnp.mean(xh * xh, axis=0, keepdims=True) + RMS_EPS)
        return xh * r * g

    gq_g = qkg_ref[0]
    gk_g = qkg_ref[1]
    for j in range(Q_W // HEAD_DIM):
        r0 = j * HEAD_DIM
        xh = p[r0:r0 + HEAD_DIM]
        if r0 >= DIFF_QK_W:
            xh = norm(xh, gq_g)
        qt_out[0, 0, r0:r0 + HEAD_DIM, :] = (rope(xh) * (ATTN_SCALE * LOG2_E)).astype(BF16)
    k_heads = []
    for j in range(K_W // HEAD_DIM):
        r0 = j * HEAD_DIM
        xh = p[Q_W + r0:Q_W + r0 + HEAD_DIM]
        if r0 >= DIFF_QK_W:
            xh = norm(xh, gk_g)
        k_heads.append(rope(xh))
    k_out[0] = jnp.concatenate(k_heads, axis=0).T.astype(BF16)


def _attn_proj(xa, modl, g0, w_t, cos_t, sin_t, qkg):
    b, t, d = xa.shape
    nt = t // TOK_TILE
    return pl.pallas_call(
        _attn_proj_body,
        grid=(b, nt),
        in_specs=[pl.BlockSpec((1, TOK_TILE, d), lambda i, j: (i, j, 0)),
                  pl.BlockSpec((1, 1, 6, d), lambda i, j: (i, jnp.minimum(j, 1), 0, 0)),
                  _const_spec((1, d)),
                  _const_spec(w_t.shape),
                  pl.BlockSpec((HEAD_DIM, TOK_TILE), lambda i, j: (0, j)),
                  pl.BlockSpec((HEAD_DIM, TOK_TILE), lambda i, j: (0, j)),
                  _const_spec((2, HEAD_DIM, 1))],
        out_specs=[pl.BlockSpec((1, 1, Q_W, TOK_TILE), lambda i, j: (i, j, 0, 0)),
                   pl.BlockSpec((1, TOK_TILE, K_W), lambda i, j: (i, j, 0)),
                   pl.BlockSpec((1, 1, VT_ROWS, TOK_TILE), lambda i, j: (i, j, 0, 0))],
        out_shape=[jax.ShapeDtypeStruct((b, nt, Q_W, TOK_TILE), BF16),
                   jax.ShapeDtypeStruct((b, t, K_W), BF16),
                   jax.ShapeDtypeStruct((b, nt, VT_ROWS, TOK_TILE), BF16)],
        compiler_params=_params(("parallel", "parallel")),
        name="attn_in_proj",
    )(xa, modl, g0, w_t, cos_t, sin_t, qkg)


def _attn_body(lam_init, qt_ref, k_ref, vt_ref, lam_ref, subln_ref, o_ref, qpad_ref, m_ref, acc_ref):
    qi = pl.program_id(1)
    slab = 2 * HEAD_DIM
    diff_blk = DIFF_V_DIM + SUM_ROWS
    gqa_blk = HEAD_DIM + SUM_ROWS
    maps = []
    for hd in range(DIFF_HEADS):
        for mm in range(2):
            maps.append((2 * hd + mm, hd, hd * diff_blk, diff_blk))
    for hq in range(GQA_Q_HEADS):
        grp = hq // GQA_REP
        maps.append((2 * DIFF_HEADS + hq, DIFF_HEADS, DIFF_HEADS * diff_blk + grp * gqa_blk, gqa_blk))

    zeros = jnp.zeros((HEAD_DIM, qt_ref.shape[3]), BF16)
    for i, (qh, _, _, _) in enumerate(maps):
        half = (qh % 2) if qh < 2 * DIFF_HEADS else (qh - 2 * DIFF_HEADS) // GQA_REP
        qh_t = qt_ref[0, 0, qh * HEAD_DIM:(qh + 1) * HEAD_DIM, :]
        qpad_ref[i] = jnp.concatenate([qh_t, zeros] if half == 0 else [zeros, qh_t], axis=0)
    def scores(c, i):
        ks = maps[i][1]
        return jnp.dot(k_ref[0, c * KV_CHUNK:(c + 1) * KV_CHUNK, ks * slab:(ks + 1) * slab], qpad_ref[i],
                       preferred_element_type=F32)

    def run(n_chunks):
        items = [(c, i) for c in range(n_chunks) for i in range(N_MAPS)]
        pending = [scores(*items[n]) for n in range(QK_AHEAD)]
        for n, (c, i) in enumerate(items):
            s = pending.pop(0)
            if n + QK_AHEAD < len(items):
                pending.append(scores(*items[n + QK_AHEAD]))
            _, _, v_row0, v_w = maps[i]
            m_new = jnp.max(s, axis=0, keepdims=True)
            if c > 0:
                m_old = m_ref[i]
                m_new = jnp.maximum(m_old, m_new)
                alpha = jnp.exp2(m_old - m_new)
            m_ref[i] = m_new
            p = jnp.exp2(s - m_new)
            pv = jnp.dot(vt_ref[0, c, v_row0:v_row0 + v_w, :], p.astype(BF16), preferred_element_type=F32)
            acc_ref[i, 0:v_w, :] = pv if c == 0 else alpha * acc_ref[i, 0:v_w, :] + pv

    @pl.when(qi == 0)
    def _():
        run(CTX_LEN // KV_CHUNK)

    @pl.when(qi > 0)
    def _():
        run(vt_ref.shape[1])

    lv = lam_ref[...]
    lam = (jnp.exp(jnp.sum(lv[0:1] * lv[1:2], axis=-1, keepdims=True))
           - jnp.exp(jnp.sum(lv[2:3] * lv[3:4], axis=-1, keepdims=True)) + lam_init)
    def normalized(i, width):
        return acc_ref[i, 0:width, :] / acc_ref[i, width:width + 1, :]

    for hd in range(DIFF_HEADS):
        od = normalized(2 * hd, DIFF_V_DIM) - lam * normalized(2 * hd + 1, DIFF_V_DIM)
        od = od * lax.rsqrt(jnp.mean(od * od, axis=0, keepdims=True) + RMS_EPS) * subln_ref[...]
        o_ref[0, :, hd * DIFF_V_DIM:(hd + 1) * DIFF_V_DIM] = (od * (1.0 - lam_init)).T.astype(o_ref.dtype)
    for pair in range(GQA_Q_HEADS // 2):
        i0 = 2 * DIFF_HEADS + 2 * pair
        og = jnp.concatenate([normalized(i0, HEAD_DIM), normalized(i0 + 1, HEAD_DIM)], axis=0)
        c0 = DIFF_V_W + pair * slab
        o_ref[0, :, c0:c0 + slab] = og.T.astype(o_ref.dtype)


def _attention(qt, k, vt, lam_vec, subln_g, lam_init):
    b, t, _ = k.shape
    nq = t // TOK_TILE
    return pl.pallas_call(
        functools.partial(_attn_body, lam_init),
        grid=(b, nq),
        in_specs=[pl.BlockSpec((1, 1, Q_W, TOK_TILE), lambda i, j: (i, j, 0, 0)),
                  pl.BlockSpec((1, t, K_W), lambda i, j: (i, 0, 0)),
                  pl.BlockSpec((1,) + vt.shape[1:], lambda i, j: (i, 0, 0, 0)),
                  _const_spec((4, HEAD_DIM)),
                  _const_spec((DIFF_V_DIM, 1))],
        out_specs=pl.BlockSpec((1, TOK_TILE, D_MODEL), lambda i, j: (i, j, 0)),
        out_shape=jax.ShapeDtypeStruct((b, t, D_MODEL), BF16),
        scratch_shapes=[pltpu.VMEM((N_MAPS, 2 * HEAD_DIM, TOK_TILE), BF16),
                        pltpu.VMEM((N_MAPS, 1, TOK_TILE), F32),
                        pltpu.VMEM((N_MAPS, DIFF_V_DIM + SUM_ROWS, TOK_TILE), F32)],
        compiler_params=_params(("parallel", "parallel")),
        name="diff_gqa_attention",
    )(qt, k, vt, lam_vec, subln_g)


def _post_body(gdn, group, *refs):
    per = 5 if gdn else 3
    tiles = [refs[per * m:per * (m + 1)] for m in range(group)]
    rest = refs[per * group:]
    if gdn:
        ng_ref, rest = rest[0], rest[1:]
    g_ref, wo_ref, wgu_ref, wd_ref, xo_ref, a_ref = rest
    g = g_ref[...]

    def mixer_out(tile):
        if not gdn:
            return tile[1][0]
        _, of_ref, ob_ref, sz_ref, _ = tile
        parts = []
        for hd in range(GDN_HEADS):
            o = of_ref[0, hd].astype(F32) + ob_ref[0, hd].astype(F32)
            parts.append(_rms(o, ng_ref[...]))
        return (jnp.concatenate(parts, axis=-1) * sz_ref[0].astype(F32)).astype(BF16)

    ys = [jnp.dot(mixer_out(tile), wo_ref[...], preferred_element_type=F32) for tile in tiles]
    for m, (tile, y) in enumerate(zip(tiles, ys)):
        mod = tile[-1][0, 0]
        x = tile[0][0] + mod[2:3] * _rms(y, g[1:2])
        h = (_rms(x, g[2:3]) * (1.0 + mod[4:5]) + mod[3:4]).astype(BF16)
        for c in range(FFN_HIDDEN // FFN_CHUNK):
            c0 = c * FFN_CHUNK
            gate = jnp.dot(h, wgu_ref[:, c0:c0 + FFN_CHUNK], preferred_element_type=F32)
            up = jnp.dot(h, wgu_ref[:, FFN_HIDDEN + c0:FFN_HIDDEN + c0 + FFN_CHUNK],
                         preferred_element_type=F32)
            a_ref[m, :, c0:c0 + FFN_CHUNK] = (_silu(gate) * up).astype(BF16)
        ff = jnp.dot(a_ref[m], wd_ref[...], preferred_element_type=F32)
        xo_ref[0, m * TOK_TILE:(m + 1) * TOK_TILE, :] = x + mod[5:6] * _rms(ff, g[3:4])


def _post_ffn(xa, mixer_out, modl, g, wo, wgu, wd, gdn, latent_only):
    b, t, d = xa.shape
    skip = CTX_LEN // TOK_TILE if latent_only else 0
    nt = t // TOK_TILE - skip
    group = FFN_GROUP
    assert (b * nt) % group == 0

    def at(m, fn):
        def index_map(p):
            n = p * group + m
            return fn(n // nt, n % nt + skip)
        return index_map

    in_specs, args = [], []
    for m in range(group):
        tile = pl.BlockSpec((1, TOK_TILE, d), at(m, lambda i, j: (i, j, 0)))
        if gdn:
            of, ob, sz, ng = mixer_out
            head_tile = pl.BlockSpec((1, GDN_HEADS, TOK_TILE, GDN_HEAD_DIM), at(m, lambda i, j: (i, 0, j, 0)))
            in_specs += [tile, head_tile, head_tile, tile]
            args += [xa, of, ob, sz]
        else:
            in_specs += [tile, tile]
            args += [xa, mixer_out]
        in_specs.append(pl.BlockSpec((1, 1, 6, d), at(m, lambda i, j: (i, jnp.minimum(j, 1), 0, 0))))
        args.append(modl)
    if gdn:
        in_specs.append(_const_spec((1, GDN_HEAD_DIM)))
        args.append(ng)
    in_specs += [_const_spec((4, d)), _const_spec(wo.shape), _const_spec(wgu.shape), _const_spec(wd.shape)]
    args += [g, wo, wgu, wd]
    rows = group * TOK_TILE
    out = pl.pallas_call(
        functools.partial(_post_body, gdn, group),
        grid=(b * nt // group,),
        in_specs=in_specs,
        out_specs=pl.BlockSpec((1, rows, d), lambda p: (p, 0, 0)),
        out_shape=jax.ShapeDtypeStruct((b * nt // group, rows, d), F32),
        scratch_shapes=[pltpu.VMEM((group, TOK_TILE, FFN_HIDDEN), BF16)],
        compiler_params=_params(("parallel",)),
        name="out_proj_ffn_gdn" if gdn else "out_proj_ffn_attn",
    )(*args)
    return out.reshape(b, nt * TOK_TILE, d)


def _gdn_proj_body(xp_ref, x_ref, xn_ref, mod_ref, g_ref, wqkv_ref, wz_ref, wab_ref, wabt_ref,
                   conv_ref, alog_ref, dtb_ref, alogt_ref, dtbt_ref,
                   q_out, k_out, v_out, sz_out, gb_out, gbt_out, xs_ref, ys_ref):
    j = pl.program_id(1)
    nt = pl.num_programs(1)
    mod = mod_ref[0, 0]
    g = g_ref[...]
    lanes = GDN_HEAD_DIM
    rows = TOK_TILE + 2 * CONV_HALO
    pitch = rows // 8

    def prep(xv):
        return _rms(xv, g) * (1.0 + mod[1:2]) + mod[0:1]

    prev_ok = jnp.where(j >= 2, 1.0, 0.0)
    next_ok = jnp.where(jnp.logical_and(j >= 1, j < nt - 1), 1.0, 0.0)
    h_f32 = prep(x_ref[0])
    h_main = h_f32.astype(BF16)
    h_cat = jnp.concatenate([prep(xp_ref[0]) * prev_ok, h_f32, prep(xn_ref[0]) * next_ok], axis=0)
    n_slab = h_cat.shape[1] // lanes
    for s in range(n_slab):
        xs_ref[s] = h_cat[:, s * lanes:(s + 1) * lanes]
    h_perm = jnp.concatenate(
        [jnp.concatenate([xs_ref[s, pl.ds(a, 8, stride=pitch), :] for s in range(n_slab)], axis=1)
         for a in range(pitch)], axis=0).astype(BF16)
    p = jnp.dot(h_perm, wqkv_ref[...], preferred_element_type=F32)
    cw = conv_ref[...]

    def taps(groups):
        acc = groups[0] * cw[0:1]
        for tap in range(1, GDN_CONV_K):
            acc = acc + groups[tap] * cw[tap:tap + 1]
        return acc

    def grp(a):
        return p[8 * a:8 * (a + 1)]

    below = [pltpu.roll(grp(pitch - 2), 1, 0), pltpu.roll(grp(pitch - 1), 1, 0)]
    above = pltpu.roll(grp(0), 7, 0)
    mid = taps([p[8 * t:8 * (t + pitch - 3)] for t in range(GDN_CONV_K)])
    conv = jnp.concatenate([taps([below[0], below[1], grp(0), grp(1)]),
                            taps([below[1], grp(0), grp(1), grp(2)]),
                            mid,
                            taps([grp(pitch - 3), grp(pitch - 2), grp(pitch - 1), above])], axis=0)
    qkv = _silu(conv)
    for hd in range(GDN_HEADS):
        c0 = hd * lanes
        qh = qkv[:, c0:c0 + lanes]
        kh = qkv[:, GDN_W + c0:GDN_W + c0 + lanes]
        qn = qh * lax.rsqrt(jnp.sum(qh * qh, axis=-1, keepdims=True) + L2_EPS) * (GDN_HEAD_DIM ** -0.5)
        kn = kh * lax.rsqrt(jnp.sum(kh * kh, axis=-1, keepdims=True) + L2_EPS)
        for slab, val, out in ((hd, qn, q_out), (GDN_HEADS + hd, kn, k_out),
                               (2 * GDN_HEADS + hd, qkv[:, 2 * GDN_W + c0:2 * GDN_W + c0 + lanes], v_out)):
            for a in range(pitch):
                ys_ref[slab, pl.ds(a, 8, stride=pitch), :] = val[8 * a:8 * (a + 1)]
            out[0, hd] = ys_ref[slab, CONV_HALO:CONV_HALO + TOK_TILE, :].astype(BF16)
    sz_out[0] = _silu(jnp.dot(h_main, wz_ref[...], preferred_element_type=F32)).astype(BF16)

    nh2 = 2 * GDN_HEADS
    ab = jnp.dot(h_main, wab_ref[...], preferred_element_type=F32)
    gdec = -jnp.exp(alog_ref[...]) * jax.nn.softplus(ab[:, :nh2] + dtb_ref[...])
    gb_out[0] = jnp.concatenate([gdec, jax.nn.sigmoid(ab[:, nh2:])], axis=-1)
    abt = lax.dot_general(wabt_ref[...], h_main, _NT, preferred_element_type=F32)
    gdec_t = -jnp.exp(alogt_ref[...]) * jax.nn.softplus(abt[:nh2] + dtbt_ref[...])
    gbt_out[0] = jnp.concatenate([gdec_t, jax.nn.sigmoid(abt[nh2:])], axis=0)


def _gdn_proj(xa, modl, g0, wqkv, wz, wab, wab_t, conv_w, a_log, dt_bias):
    b, t, d = xa.shape
    nt = t // TOK_TILE
    per = TOK_TILE // CONV_HALO
    last = t // CONV_HALO - 1
    nh2 = 2 * GDN_HEADS
    head_out = pl.BlockSpec((1, GDN_HEADS, TOK_TILE, GDN_HEAD_DIM), lambda i, j: (i, 0, j, 0))
    head_shape = jax.ShapeDtypeStruct((b, GDN_HEADS, t, GDN_HEAD_DIM), BF16)
    return pl.pallas_call(
        _gdn_proj_body,
        grid=(b, nt),
        in_specs=[pl.BlockSpec((1, CONV_HALO, d), lambda i, j: (i, jnp.maximum(j * per - 1, 0), 0)),
                  pl.BlockSpec((1, TOK_TILE, d), lambda i, j: (i, j, 0)),
                  pl.BlockSpec((1, CONV_HALO, d), lambda i, j: (i, jnp.minimum((j + 1) * per, last), 0)),
                  pl.BlockSpec((1, 1, 6, d), lambda i, j: (i, jnp.minimum(j, 1), 0, 0)),
                  _const_spec((1, d)),
                  _const_spec(wqkv.shape), _const_spec(wz.shape), _const_spec(wab.shape),
                  _const_spec(wab_t.shape), _const_spec(conv_w.shape),
                  _const_spec((1, nh2)), _const_spec((1, nh2)),
                  _const_spec((nh2, 1)), _const_spec((nh2, 1))],
        out_specs=[head_out, head_out, head_out,
                   pl.BlockSpec((1, TOK_TILE, d), lambda i, j: (i, j, 0)),
                   pl.BlockSpec((1, TOK_TILE, 2 * nh2), lambda i, j: (i, j, 0)),
                   pl.BlockSpec((1, 2 * nh2, TOK_TILE), lambda i, j: (i, 0, j))],
        out_shape=[head_shape, head_shape, head_shape,
                   jax.ShapeDtypeStruct((b, t, d), BF16),
                   jax.ShapeDtypeStruct((b, t, 2 * nh2), F32),
                   jax.ShapeDtypeStruct((b, 2 * nh2, t), F32)],
        scratch_shapes=[pltpu.VMEM((d // GDN_HEAD_DIM, TOK_TILE + 2 * CONV_HALO, GDN_HEAD_DIM), F32),
                        pltpu.VMEM((3 * GDN_HEADS, TOK_TILE + 2 * CONV_HALO, GDN_HEAD_DIM), F32)],
        compiler_params=_params(("parallel", "parallel")),
        name="gdn_in_proj",
    )(xa, xa, xa, modl, g0, wqkv, wz, wab, wab_t, conv_w,
      a_log.reshape(1, nh2), dt_bias.reshape(1, nh2), a_log.reshape(nh2, 1), dt_bias.reshape(nh2, 1))


def _gdn_scan_body(qf, kf, vf, qb, kb, vb, gbf, gbb, gtf, gtb, of, ob, s_ref):
    @pl.when(pl.program_id(1) == 0)
    def _():
        s_ref[...] = jnp.zeros_like(s_ref)

    c = GDN_CHUNK
    row = lax.broadcasted_iota(jnp.int32, (c, c), 0)
    col = lax.broadcasted_iota(jnp.int32, (c, c), 1)
    lower = jnp.where(row >= col, 1.0, 0.0)
    upper = jnp.where(row <= col, 1.0, 0.0)
    row = lax.broadcasted_iota(jnp.int32, (c, 2 * c), 0)
    col = lax.broadcasted_iota(jnp.int32, (c, 2 * c), 1)
    nh2 = 2 * GDN_HEADS
    hi = lax.Precision.HIGHEST
    n_sub = qf.shape[2] // c
    refs = (qf, kf, vf, qb, kb, vb, gbf, gbb, gtf, gtb, of, ob)
    chunks = [[] for _ in range(n_sub)]
    local = [_gdn_local_stages(chunks[sub], sub, n_sub - 1 - sub, c, row, col, lower, upper, nh2, hi, refs)
             for sub in range(n_sub)]
    carried = [_gdn_state_stages(chunks[sub], c, s_ref) for sub in range(n_sub)]
    for _ in local[0]:
        pass
    for sub in range(n_sub):
        ahead = local[sub + 1] if sub + 1 < n_sub else iter(())
        for _ in carried[sub]:
            for _ in range(LOCAL_STAGES_PER_STATE_STAGE):
                next(ahead, None)
        for _ in ahead:
            pass


def _gdn_chunk_chains(sub_f, sub_b, c, row, col, lower, upper, nh2, hi,
                      qf, kf, vf, qb, kb, vb, gbf, gbb, gtf, gtb, of, ob):
    dh = GDN_HEAD_DIM
    left = col < c
    zeros = jnp.zeros((c, dh), BF16)

    def wide(x0, x1):
        return jnp.concatenate([jnp.broadcast_to(x0, (c, dh)), jnp.broadcast_to(x1, (c, dh))], axis=1)

    pairs = []
    for direction, (q_ref, k_ref, v_ref, gb_ref, gt_ref, o_ref, sub) in enumerate(
            ((qf, kf, vf, gbf, gtf, of, sub_f), (qb, kb, vb, gbb, gtb, ob, sub_b))):
        r0 = sub * c
        gb = gb_ref[0, r0:r0 + c, :]
        gt = gt_ref[0, :, r0:r0 + c]
        tri_c, tri_r = (lower, upper) if direction == 0 else (upper, lower)
        gc = jnp.dot(tri_c, gb[:, :nh2], precision=hi, preferred_element_type=F32)
        gr = jnp.dot(gt[:nh2], tri_r, precision=hi, preferred_element_type=F32)
        incl = (row >= (col & (c - 1))) if direction == 0 else (row <= (col & (c - 1)))
        strict = (row > (col & (c - 1))) if direction == 0 else (row < (col & (c - 1)))
        for pr in range(GDN_HEADS // 2):
            h0 = 2 * pr
            ch = direction * GDN_HEADS + h0
            gcol0, gcol1 = gc[:, ch:ch + 1], gc[:, ch + 1:ch + 2]
            beta0, beta1 = gb[:, nh2 + ch:nh2 + ch + 1], gb[:, nh2 + ch + 1:nh2 + ch + 2]
            gcol = jnp.where(left, gcol0, gcol1)
            grow = jnp.concatenate([gr[ch:ch + 1, :], gr[ch + 1:ch + 2, :]], axis=1)
            edge = c - 1 if direction == 0 else 0
            tot0, tot1 = gcol0[edge:edge + 1], gcol1[edge:edge + 1]
            k0, k1 = k_ref[0, h0, r0:r0 + c, :], k_ref[0, h0 + 1, r0:r0 + c, :]
            q0, q1 = q_ref[0, h0, r0:r0 + c, :], q_ref[0, h0 + 1, r0:r0 + c, :]
            pairs.append(dict(
                idx=direction * (GDN_HEADS // 2) + pr, h0=h0, lower=direction == 0, strict=strict,
                o_ref=o_ref, r0=r0, left=left,
                kq=jnp.concatenate([jnp.concatenate([k0, q0], axis=0),
                                    jnp.concatenate([k1, q1], axis=0)], axis=1),
                k_diag=jnp.concatenate([jnp.concatenate([k0, zeros], axis=1),
                                        jnp.concatenate([zeros, k1], axis=1)], axis=0),
                k_rows=jnp.concatenate([k0, k1], axis=0),
                v=jnp.concatenate([v_ref[0, h0, r0:r0 + c, :], v_ref[0, h0 + 1, r0:r0 + c, :]], axis=1),
                beta=jnp.where(left, beta0, beta1), beta_w=wide(beta0, beta1),
                egc_w=wide(jnp.exp(gcol0), jnp.exp(gcol1)),
                dec_w=wide(jnp.exp(tot0 - gcol0), jnp.exp(tot1 - gcol1)),
                etot_w=jnp.concatenate([jnp.broadcast_to(jnp.exp(tot0), (1, dh)),
                                        jnp.broadcast_to(jnp.exp(tot1), (1, dh))], axis=1),
                decay=jnp.where(incl, jnp.exp(jnp.where(incl, gcol - grow, 0.0)), 0.0)))
    return pairs


def _gdn_local_stages(chains, sub_f, sub_b, c, row, col, lower, upper, nh2, hi, refs):
    chains.extend(_gdn_chunk_chains(sub_f, sub_b, c, row, col, lower, upper, nh2, hi, *refs))
    for w in chains:
        w["gram"] = lax.dot_general(w["kq"], w["k_diag"], _NT, preferred_element_type=F32)
    yield
    ms = [jnp.where(w["strict"], w["beta"] * w["gram"][:c] * w["decay"], 0.0) for w in chains]
    left = col < c
    colh = col & (c - 1)

    def diag2(x):
        return jnp.concatenate([jnp.where(left, x, 0.0), jnp.where(left, 0.0, x)], axis=0).astype(BF16)

    t_offs = None
    k = 1
    while k < c:
        same = (row ^ colh) < 2 * k
        joins = {True: same & ((row & k) != 0) & ((colh & k) == 0),
                 False: same & ((colh & k) != 0) & ((row & k) == 0)}
        parts = [jnp.where(joins[w["lower"]], m, 0.0) for m, w in zip(ms, chains)]
        if t_offs is None:
            t_offs = [-a for a in parts]
        else:
            zs = [a + jnp.dot(t.astype(BF16), diag2(a), preferred_element_type=F32)
                  for a, t in zip(parts, t_offs)]
            yield
            t_offs = [t - z - jnp.dot(z.astype(BF16), diag2(t), preferred_element_type=F32)
                      for t, z in zip(t_offs, zs)]
            yield
        k *= 2
    eye = jnp.where(row == colh, 1.0, 0.0)
    for w, t in zip(chains, t_offs):
        w["tmat"] = (t + eye).astype(BF16)


def _gdn_state_stages(chains, c, s_ref):
    dh = GDN_HEAD_DIM

    def diag2(x):
        z = jnp.zeros((x.shape[0], dh), x.dtype)
        return jnp.concatenate([jnp.concatenate([x[:, :dh], z], axis=1),
                                jnp.concatenate([z, x[:, dh:]], axis=1)], axis=0)

    for w in chains:
        w["state"] = s_ref[w["idx"]]
        w["ks"] = jnp.dot(w["kq"], diag2(w["state"].astype(BF16)), preferred_element_type=F32)
    yield
    for w in chains:
        resid = (w["v"].astype(F32) - w["ks"][:c] * w["egc_w"]) * w["beta_w"]
        w["v_new"] = jnp.dot(w["tmat"], diag2(resid.astype(BF16)), preferred_element_type=F32)
        w["qs"] = w["ks"][c:] * w["egc_w"]
    yield
    for w in chains:
        attn = (w["gram"][c:] * w["decay"]).astype(BF16)
        o = w["qs"] + jnp.dot(attn, diag2(w["v_new"].astype(BF16)), preferred_element_type=F32)
        for m in range(2):
            w["o_ref"][0, w["h0"] + m, w["r0"]:w["r0"] + c, :] = o[:, m * dh:(m + 1) * dh].astype(w["o_ref"].dtype)
    yield
    for w in chains:
        v_dec = diag2((w["v_new"] * w["dec_w"]).astype(BF16))
        s_ref[w["idx"]] = (w["state"] * w["etot_w"]
                           + lax.dot_general(w["k_rows"], v_dec, _TN, preferred_element_type=F32))
    yield


def _gdn_scan(q, k, v, gb, gbt):
    b, nh, t, dh = q.shape
    nt = t // TOK_TILE
    nh2 = 2 * GDN_HEADS

    def bwd(s):
        return jnp.where(s == 0, 0, nt - s)

    head_f = pl.BlockSpec((1, nh, TOK_TILE, dh), lambda i, s: (i, 0, s, 0))
    head_b = pl.BlockSpec((1, nh, TOK_TILE, dh), lambda i, s: (i, 0, bwd(s), 0))
    out_shape = jax.ShapeDtypeStruct((b, nh, t, dh), BF16)
    return pl.pallas_call(
        _gdn_scan_body,
        grid=(b, nt),
        in_specs=[head_f, head_f, head_f, head_b, head_b, head_b,
                  pl.BlockSpec((1, TOK_TILE, 2 * nh2), lambda i, s: (i, s, 0)),
                  pl.BlockSpec((1, TOK_TILE, 2 * nh2), lambda i, s: (i, bwd(s), 0)),
                  pl.BlockSpec((1, 2 * nh2, TOK_TILE), lambda i, s: (i, 0, s)),
                  pl.BlockSpec((1, 2 * nh2, TOK_TILE), lambda i, s: (i, 0, bwd(s)))],
        out_specs=[head_f, head_b],
        out_shape=[out_shape, out_shape],
        scratch_shapes=[pltpu.VMEM((nh2 // 2, dh, 2 * dh), F32)],
        compiler_params=_params(("arbitrary", "arbitrary")),
        name="gdn_chunk_scan",
    )(q, k, v, q, k, v, gb, gb, gbt, gbt)


def _rope_tables(n_lat):
    rows = n_lat // GRID_W
    row_ids = jnp.repeat(jnp.arange(rows, dtype=F32), GRID_W)[:n_lat]
    col_ids = jnp.tile(jnp.arange(GRID_W, dtype=F32), rows)[:n_lat]
    axis_dim = HEAD_DIM // 2
    inv_freq = ROPE_THETA ** (-jnp.arange(0, axis_dim, 2, dtype=F32) / axis_dim)
    ang_r = row_ids[:, None] * inv_freq
    ang_c = col_ids[:, None] * inv_freq
    ang = jnp.concatenate([ang_r, ang_r, ang_c, ang_c], axis=-1)
    cos = jnp.concatenate([jnp.ones((CTX_LEN, HEAD_DIM), F32), jnp.cos(ang)], axis=0)
    sin = jnp.concatenate([jnp.zeros((CTX_LEN, HEAD_DIM), F32), jnp.sin(ang)], axis=0)
    sign = jnp.tile(jnp.repeat(jnp.array([-1.0, 1.0], F32), HEAD_DIM // 4), 2)
    return cos.T, (sin * sign).T


def kernel(x, c, ctx, c_ctx, ada_w, ada_b, norm_g, attn_w_in, attn_w_out, diff_lambda, diff_subln_g,
           gqa_qk_g, gdn_w_in, gdn_conv_w, gdn_a_log, gdn_dt_bias, gdn_norm_g, gdn_w_out,
           ffn_w_gate_up, ffn_w_down):
    b, n_lat, d = x.shape
    depth = ada_w.shape[0]
    assert d == D_MODEL and ctx.shape[1] == CTX_LEN and n_lat % TOK_TILE == 0
    t = CTX_LEN + n_lat
    xa = jnp.concatenate([ctx, x], axis=1)

    rows = -(-(b + 1) // 8) * 8
    cc = jnp.concatenate([c, c_ctx[None], jnp.zeros((rows - b - 1, d), F32)], axis=0)
    mods = _modulation(cc, ada_w, ada_b)
    cos_t, sin_t = _rope_tables(n_lat)
    nh2 = 2 * GDN_HEADS

    for l in range(depth):
        i = l // 2
        last = l == depth - 1
        ml = mods[l]
        modl = jnp.stack([jnp.broadcast_to(ml[b].reshape(1, 6, d), (b, 6, d)),
                          ml[:b].reshape(b, 6, d)], axis=1)
        g = norm_g[l]
        wgu = ffn_w_gate_up[l].astype(BF16)
        wd = ffn_w_down[l].astype(BF16)
        if l % 2 == 0:
            lam_init = 0.8 - 0.6 * math.exp(-0.3 * l)
            w = attn_w_in[i]
            o_dv = 2 * DIFF_QK_W
            o_gq = o_dv + DIFF_V_W
            o_gk = o_gq + GQA_Q_W
            o_gv = o_gk + GQA_KV_W
            w_t = jnp.concatenate([w[:, :DIFF_QK_W], w[:, o_gq:o_gk],
                                   w[:, DIFF_QK_W:o_dv], w[:, o_gk:o_gv],
                                   w[:, o_dv:o_gq], w[:, o_gv:]],
                                  axis=1).T.astype(BF16)
            qt, k, vt = _attn_proj(xa, modl, g[0:1], w_t, cos_t, sin_t, gqa_qk_g[i].reshape(2, HEAD_DIM, 1))
            o = _attention(qt, k, vt, diff_lambda[i], diff_subln_g[i].reshape(DIFF_V_DIM, 1), lam_init)
            xa = _post_ffn(xa, o, modl, g, attn_w_out[i].astype(BF16), wgu, wd, gdn=False, latent_only=last)
        else:
            w = gdn_w_in[i]
            wab = w[:, 4 * GDN_W:]
            q, k, v, sz, gb, gbt = _gdn_proj(
                xa, modl, g[0:1], w[:, :3 * GDN_W].astype(BF16), w[:, 3 * GDN_W:4 * GDN_W].astype(BF16),
                wab.astype(BF16), wab.T.astype(BF16), gdn_conv_w[i], gdn_a_log[i], gdn_dt_bias[i])
            of, ob = _gdn_scan(q, k, v, gb, gbt)
            xa = _post_ffn(xa, (of, ob, sz, gdn_norm_g[i].reshape(1, GDN_HEAD_DIM)), modl, g,
                           gdn_w_out[i].astype(BF16), wgu, wd, gdn=True, latent_only=last)
    return xa
```

```python
import functools
import math

import jax
import jax.numpy as jnp
from jax import lax
from jax.experimental import pallas as pl
from jax.experimental.pallas import tpu as pltpu

F32 = jnp.float32
BF16 = jnp.bfloat16

D_MODEL = 1024
CTX_LEN = 256
GRID_W = 64
RMS_EPS = 1e-6
L2_EPS = 1e-6
ROPE_THETA = 10000.0

HEAD_DIM = 64
ATTN_SCALE = HEAD_DIM ** -0.5
DIFF_HEADS = 4
DIFF_V_DIM = 2 * HEAD_DIM
GQA_Q_HEADS = 8
GQA_KV_HEADS = 2
GQA_REP = GQA_Q_HEADS // GQA_KV_HEADS
DIFF_QK_W = DIFF_HEADS * 2 * HEAD_DIM
DIFF_V_W = DIFF_HEADS * DIFF_V_DIM
GQA_Q_W = GQA_Q_HEADS * HEAD_DIM
GQA_KV_W = GQA_KV_HEADS * HEAD_DIM
Q_W = DIFF_QK_W + GQA_Q_W
K_W = DIFF_QK_W + GQA_KV_W
V_W = DIFF_V_W + GQA_KV_W
SUM_ROWS = 16
VT_ROWS = V_W + (DIFF_HEADS + GQA_KV_HEADS) * SUM_ROWS
LOG2_E = math.log2(math.e)
N_MAPS = 2 * DIFF_HEADS + GQA_Q_HEADS
QK_AHEAD = 6

GDN_HEADS = 8
GDN_HEAD_DIM = 128
GDN_W = GDN_HEADS * GDN_HEAD_DIM
GDN_CONV_K = 4
GDN_CONV_LEFT = 2
GDN_CHUNK = 64
LOCAL_STAGES_PER_STATE_STAGE = 3
CONV_HALO = 8

FFN_HIDDEN = 2816
FFN_CHUNK = 256
FFN_GROUP = 2
PROJ_GROUP = 2

TOK_TILE = 256
KV_CHUNK = 256
VMEM_LIMIT = 56 * 1024 * 1024

_NT = (((1,), (1,)), ((), ()))
_TN = (((0,), (0,)), ((), ()))


def _rms(x, g):
    return x * lax.rsqrt(jnp.mean(x * x, axis=-1, keepdims=True) + RMS_EPS) * g


def _silu(x):
    return x * jax.nn.sigmoid(x)


def _params(sem):
    return pltpu.CompilerParams(dimension_semantics=sem, vmem_limit_bytes=VMEM_LIMIT)


def _const_spec(shape):
    n = len(shape)
    return pl.BlockSpec(shape, lambda *_: (0,) * n)


def _mod_body(c_ref, w_ref, b_ref, o_ref):
    sc = _silu(c_ref[...])
    o_ref[0] = jnp.dot(sc, w_ref[0], precision=lax.Precision.HIGHEST,
                       preferred_element_type=F32) + b_ref[0]


def _modulation(cc, ada_w, ada_b):
    depth, d, w6 = ada_w.shape
    rows = cc.shape[0]
    nblk = w6 // d
    return pl.pallas_call(
        _mod_body,
        grid=(depth, nblk),
        in_specs=[pl.BlockSpec((rows, d), lambda l, j: (0, 0)),
                  pl.BlockSpec((1, d, d), lambda l, j: (l, 0, j)),
                  pl.BlockSpec((1, 1, d), lambda l, j: (l, 0, j))],
        out_specs=pl.BlockSpec((1, rows, d), lambda l, j: (l, 0, j)),
        out_shape=jax.ShapeDtypeStruct((depth, rows, w6), F32),
        compiler_params=_params(("parallel", "parallel")),
        name="adaln_mod",
    )(cc, ada_w, ada_b.reshape(depth, 1, w6))


def _attn_proj_body(group, nt, assemble, *refs):
    per = 5 if assemble else 4
    tiles = [refs[per * m:per * (m + 1)] for m in range(group)]
    g_ref, w_ref, qkg_ref = refs[per * group:per * group + 3]
    outs = refs[per * group + 3:]
    hs = []
    for m, tile in enumerate(tiles):
        if assemble:
            j = (pl.program_id(0) * group + m) % nt
            x = jnp.where(j == 0, tile[0][0], tile[1][0])
            outs[3][m] = x
        else:
            x = tile[0][0]
        mod = tile[-3][0, 0]
        hs.append((_rms(x, g_ref[...]) * (1.0 + mod[1:2]) + mod[0:1]).astype(BF16))
    for m, (tile, h) in enumerate(zip(tiles, hs)):
        _attn_proj_heads(m, h, w_ref, tile[-2], tile[-1], qkg_ref, *outs[:3])


def _attn_proj_heads(m, h, w_ref, cos_ref, sin_ref, qkg_ref, qt_out, k_out, vt_out):
    def project(r0, rows):
        return lax.dot_general(w_ref[r0:r0 + rows, :], h, _NT, preferred_element_type=F32)

    pk = project(Q_W, K_W)
    pq = project(0, Q_W)
    pv = project(Q_W + K_W, V_W)
    cos = cos_ref[...]
    sin = sin_ref[...]

    def rope(xh):
        swapped = jnp.concatenate([xh[16:32], xh[0:16], xh[48:64], xh[32:48]], axis=0)
        return xh * cos + swapped * sin

    def norm(xh, g):
        r = lax.rsqrt(jnp.mean(xh * xh, axis=0, keepdims=True) + RMS_EPS)
        return xh * r * g

    gq_g = qkg_ref[0]
    gk_g = qkg_ref[1]
    k_heads = []
    for j in range(K_W // HEAD_DIM):
        r0 = j * HEAD_DIM
        xh = pk[r0:r0 + HEAD_DIM]
        if r0 >= DIFF_QK_W:
            xh = norm(xh, gk_g)
        k_heads.append(rope(xh))
    k_out[m] = jnp.concatenate(k_heads, axis=0).T.astype(BF16)
    for j in range(Q_W // HEAD_DIM):
        r0 = j * HEAD_DIM
        xh = pq[r0:r0 + HEAD_DIM]
        if r0 >= DIFF_QK_W:
            xh = norm(xh, gq_g)
        qt_out[m, r0:r0 + HEAD_DIM, :] = (rope(xh) * (ATTN_SCALE * LOG2_E)).astype(BF16)
    ones = jnp.ones((SUM_ROWS, pv.shape[1]), F32)
    v_blocks = []
    for r0, width in ([(hd * DIFF_V_DIM, DIFF_V_DIM) for hd in range(DIFF_HEADS)]
                      + [(DIFF_V_W + grp * HEAD_DIM, HEAD_DIM) for grp in range(GQA_KV_HEADS)]):
        v_blocks += [pv[r0:r0 + width], ones]
    vt_out[m] = jnp.concatenate(v_blocks, axis=0).astype(BF16)


def _attn_proj(xa, modl, g0, w_t, cos_t, sin_t, qkg, ctx=None):
    assemble = ctx is not None
    b, t, d = xa.shape
    if assemble:
        t += CTX_LEN
    nt = t // TOK_TILE
    group = PROJ_GROUP
    assert (b * nt) % group == 0
    steps = b * nt // group

    def at(m, fn):
        def index_map(p):
            n = p * group + m
            return fn(n // nt, n % nt)
        return index_map

    in_specs, args = [], []
    for m in range(group):
        if assemble:
            in_specs += [pl.BlockSpec((1, TOK_TILE, d), at(m, lambda i, j: (i, 0, 0))),
                         pl.BlockSpec((1, TOK_TILE, d), at(m, lambda i, j: (i, jnp.maximum(j - 1, 0), 0)))]
            args += [ctx, xa]
        else:
            in_specs.append(pl.BlockSpec((1, TOK_TILE, d), at(m, lambda i, j: (i, j, 0))))
            args.append(xa)
        in_specs += [pl.BlockSpec((1, 1, 6, d), at(m, lambda i, j: (i, jnp.minimum(j, 1), 0, 0))),
                     pl.BlockSpec((HEAD_DIM, TOK_TILE), at(m, lambda i, j: (0, j))),
                     pl.BlockSpec((HEAD_DIM, TOK_TILE), at(m, lambda i, j: (0, j)))]
        args += [modl, cos_t, sin_t]
    in_specs += [_const_spec((1, d)), _const_spec(w_t.shape), _const_spec((2, HEAD_DIM, 1))]
    args += [g0, w_t, qkg]
    out_specs = [pl.BlockSpec((group, Q_W, TOK_TILE), lambda p: (p, 0, 0)),
                 pl.BlockSpec((group, TOK_TILE, K_W), lambda p: (p, 0, 0)),
                 pl.BlockSpec((group, VT_ROWS, TOK_TILE), lambda p: (p, 0, 0))]
    out_shape = [jax.ShapeDtypeStruct((b * nt, Q_W, TOK_TILE), BF16),
                 jax.ShapeDtypeStruct((b * nt, TOK_TILE, K_W), BF16),
                 jax.ShapeDtypeStruct((b * nt, VT_ROWS, TOK_TILE), BF16)]
    if assemble:
        out_specs.append(pl.BlockSpec((group, TOK_TILE, d), lambda p: (p, 0, 0)))
        out_shape.append(jax.ShapeDtypeStruct((b * nt, TOK_TILE, d), F32))
    outs = pl.pallas_call(
        functools.partial(_attn_proj_body, group, nt, assemble),
        grid=(steps,),
        in_specs=in_specs,
        out_specs=out_specs,
        out_shape=out_shape,
        compiler_params=_params(("parallel",)),
        name="attn_in_proj",
    )(*args)
    res = (outs[0].reshape(b, nt, Q_W, TOK_TILE), outs[1].reshape(b, t, K_W),
           outs[2].reshape(b, nt, VT_ROWS, TOK_TILE))
    return res + (outs[3].reshape(b, t, d),) if assemble else res


def _attn_body(lam_init, qt_ref, k_ref, vt_ref, lam_ref, subln_ref, o_ref, qpad_ref, m_ref, acc_ref):
    qi = pl.program_id(1)
    slab = 2 * HEAD_DIM
    diff_blk = DIFF_V_DIM + SUM_ROWS
    gqa_blk = HEAD_DIM + SUM_ROWS
    maps = []
    for hd in range(DIFF_HEADS):
        for mm in range(2):
            maps.append((2 * hd + mm, hd, hd * diff_blk, diff_blk))
    for hq in range(GQA_Q_HEADS):
        grp = hq // GQA_REP
        maps.append((2 * DIFF_HEADS + hq, DIFF_HEADS, DIFF_HEADS * diff_blk + grp * gqa_blk, gqa_blk))

    zeros = jnp.zeros((HEAD_DIM, qt_ref.shape[3]), BF16)
    for i, (qh, _, _, _) in enumerate(maps):
        half = (qh % 2) if qh < 2 * DIFF_HEADS else (qh - 2 * DIFF_HEADS) // GQA_REP
        qh_t = qt_ref[0, 0, qh * HEAD_DIM:(qh + 1) * HEAD_DIM, :]
        qpad_ref[i] = jnp.concatenate([qh_t, zeros] if half == 0 else [zeros, qh_t], axis=0)
    def scores(c, i):
        ks = maps[i][1]
        return jnp.dot(k_ref[0, c * KV_CHUNK:(c + 1) * KV_CHUNK, ks * slab:(ks + 1) * slab], qpad_ref[i],
                       preferred_element_type=F32)

    def run(n_chunks):
        items = [(c, i) for c in range(n_chunks) for i in range(N_MAPS)]
        pending = [scores(*items[n]) for n in range(QK_AHEAD)]
        for n, (c, i) in enumerate(items):
            s = pending.pop(0)
            if n + QK_AHEAD < len(items):
                pending.append(scores(*items[n + QK_AHEAD]))
            _, _, v_row0, v_w = maps[i]
            m_new = jnp.max(s, axis=0, keepdims=True)
            if c > 0:
                m_old = m_ref[i]
                m_new = jnp.maximum(m_old, m_new)
                alpha = jnp.exp2(m_old - m_new)
            m_ref[i] = m_new
            p = jnp.exp2(s - m_new)
            pv = jnp.dot(vt_ref[0, c, v_row0:v_row0 + v_w, :], p.astype(BF16), preferred_element_type=F32)
            acc_ref[i, 0:v_w, :] = pv if c == 0 else alpha * acc_ref[i, 0:v_w, :] + pv

    @pl.when(qi == 0)
    def _():
        run(CTX_LEN // KV_CHUNK)

    @pl.when(qi > 0)
    def _():
        run(vt_ref.shape[1])

    lv = lam_ref[...]
    lam = (jnp.exp(jnp.sum(lv[0:1] * lv[1:2], axis=-1, keepdims=True))
           - jnp.exp(jnp.sum(lv[2:3] * lv[3:4], axis=-1, keepdims=True)) + lam_init)
    def normalized(i, width):
        return acc_ref[i, 0:width, :] / acc_ref[i, width:width + 1, :]

    for hd in range(DIFF_HEADS):
        od = normalized(2 * hd, DIFF_V_DIM) - lam * normalized(2 * hd + 1, DIFF_V_DIM)
        od = od * lax.rsqrt(jnp.mean(od * od, axis=0, keepdims=True) + RMS_EPS) * subln_ref[...]
        o_ref[0, :, hd * DIFF_V_DIM:(hd + 1) * DIFF_V_DIM] = (od * (1.0 - lam_init)).T.astype(o_ref.dtype)
    for pair in range(GQA_Q_HEADS // 2):
        i0 = 2 * DIFF_HEADS + 2 * pair
        og = jnp.concatenate([normalized(i0, HEAD_DIM), normalized(i0 + 1, HEAD_DIM)], axis=0)
        c0 = DIFF_V_W + pair * slab
        o_ref[0, :, c0:c0 + slab] = og.T.astype(o_ref.dtype)


def _attention(qt, k, vt, lam_vec, subln_g, lam_init):
    b, t, _ = k.shape
    nq = t // TOK_TILE
    return pl.pallas_call(
        functools.partial(_attn_body, lam_init),
        grid=(b, nq),
        in_specs=[pl.BlockSpec((1, 1, Q_W, TOK_TILE), lambda i, j: (i, j, 0, 0)),
                  pl.BlockSpec((1, t, K_W), lambda i, j: (i, 0, 0)),
                  pl.BlockSpec((1,) + vt.shape[1:], lambda i, j: (i, 0, 0, 0)),
                  _const_spec((4, HEAD_DIM)),
                  _const_spec((DIFF_V_DIM, 1))],
        out_specs=pl.BlockSpec((1, TOK_TILE, D_MODEL), lambda i, j: (i, j, 0)),
        out_shape=jax.ShapeDtypeStruct((b, t, D_MODEL), BF16),
        scratch_shapes=[pltpu.VMEM((N_MAPS, 2 * HEAD_DIM, TOK_TILE), BF16),
                        pltpu.VMEM((N_MAPS, 1, TOK_TILE), F32),
                        pltpu.VMEM((N_MAPS, DIFF_V_DIM + SUM_ROWS, TOK_TILE), F32)],
        compiler_params=_params(("parallel", "parallel")),
        name="diff_gqa_attention",
    )(qt, k, vt, lam_vec, subln_g)


def _post_body(gdn, group, *refs):
    per = 5 if gdn else 3
    tiles = [refs[per * m:per * (m + 1)] for m in range(group)]
    rest = refs[per * group:]
    if gdn:
        ng_ref, rest = rest[0], rest[1:]
    g_ref, wo_ref, wgu_ref, wd_ref, xo_ref, a_ref = rest
    g = g_ref[...]

    def mixer_out(tile):
        if not gdn:
            return tile[1][0]
        _, of_ref, ob_ref, sz_ref, _ = tile
        parts = []
        for hd in range(GDN_HEADS):
            o = of_ref[0, hd].astype(F32) + ob_ref[0, hd].astype(F32)
            parts.append(_rms(o, ng_ref[...]))
        return (jnp.concatenate(parts, axis=-1) * sz_ref[0].astype(F32)).astype(BF16)

    ys = [jnp.dot(mixer_out(tile), wo_ref[...], preferred_element_type=F32) for tile in tiles]
    for m, (tile, y) in enumerate(zip(tiles, ys)):
        mod = tile[-1][0, 0]
        x = tile[0][0] + mod[2:3] * _rms(y, g[1:2])
        h = (_rms(x, g[2:3]) * (1.0 + mod[4:5]) + mod[3:4]).astype(BF16)
        for c in range(FFN_HIDDEN // FFN_CHUNK):
            c0 = c * FFN_CHUNK
            gate = jnp.dot(h, wgu_ref[:, c0:c0 + FFN_CHUNK], preferred_element_type=F32)
            up = jnp.dot(h, wgu_ref[:, FFN_HIDDEN + c0:FFN_HIDDEN + c0 + FFN_CHUNK],
                         preferred_element_type=F32)
            a_ref[m, :, c0:c0 + FFN_CHUNK] = (_silu(gate) * up).astype(BF16)
        ff = jnp.dot(a_ref[m], wd_ref[...], preferred_element_type=F32)
        xo_ref[0, m * TOK_TILE:(m + 1) * TOK_TILE, :] = x + mod[5:6] * _rms(ff, g[3:4])


def _post_ffn(xa, mixer_out, modl, g, wo, wgu, wd, gdn, latent_only):
    b, t, d = xa.shape
    skip = CTX_LEN // TOK_TILE if latent_only else 0
    nt = t // TOK_TILE - skip
    group = FFN_GROUP
    assert (b * nt) % group == 0

    def at(m, fn):
        def index_map(p):
            n = p * group + m
            return fn(n // nt, n % nt + skip)
        return index_map

    in_specs, args = [], []
    for m in range(group):
        tile = pl.BlockSpec((1, TOK_TILE, d), at(m, lambda i, j: (i, j, 0)))
        if gdn:
            of, ob, sz, ng = mixer_out
            head_tile = pl.BlockSpec((1, GDN_HEADS, TOK_TILE, GDN_HEAD_DIM), at(m, lambda i, j: (i, 0, j, 0)))
            in_specs += [tile, head_tile, head_tile, tile]
            args += [xa, of, ob, sz]
        else:
            in_specs += [tile, tile]
            args += [xa, mixer_out]
        in_specs.append(pl.BlockSpec((1, 1, 6, d), at(m, lambda i, j: (i, jnp.minimum(j, 1), 0, 0))))
        args.append(modl)
    if gdn:
        in_specs.append(_const_spec((1, GDN_HEAD_DIM)))
        args.append(ng)
    in_specs += [_const_spec((4, d)), _const_spec(wo.shape), _const_spec(wgu.shape), _const_spec(wd.shape)]
    args += [g, wo, wgu, wd]
    rows = group * TOK_TILE
    out = pl.pallas_call(
        functools.partial(_post_body, gdn, group),
        grid=(b * nt // group,),
        in_specs=in_specs,
        out_specs=pl.BlockSpec((1, rows, d), lambda p: (p, 0, 0)),
        out_shape=jax.ShapeDtypeStruct((b * nt // group, rows, d), F32),
        scratch_shapes=[pltpu.VMEM((group, TOK_TILE, FFN_HIDDEN), BF16)],
        compiler_params=_params(("parallel",)),
        name="out_proj_ffn_gdn" if gdn else "out_proj_ffn_attn",
    )(*args)
    return out.reshape(b, nt * TOK_TILE, d)


def _gdn_proj_body(xp_ref, x_ref, xn_ref, mod_ref, g_ref, wqkv_ref, wz_ref, wab_ref, wabt_ref,
                   conv_ref, alog_ref, dtb_ref, alogt_ref, dtbt_ref,
                   q_out, k_out, v_out, sz_out, gb_out, gbt_out, xs_ref, ys_ref):
    j = pl.program_id(1)
    nt = pl.num_programs(1)
    mod = mod_ref[0, 0]
    g = g_ref[...]
    lanes = GDN_HEAD_DIM
    rows = TOK_TILE + 2 * CONV_HALO
    pitch = rows // 8

    def prep(xv):
        return _rms(xv, g) * (1.0 + mod[1:2]) + mod[0:1]

    prev_ok = jnp.where(j >= 2, 1.0, 0.0)
    next_ok = jnp.where(jnp.logical_and(j >= 1, j < nt - 1), 1.0, 0.0)
    h_f32 = prep(x_ref[0])
    h_main = h_f32.astype(BF16)
    h_cat = jnp.concatenate([prep(xp_ref[0]) * prev_ok, h_f32, prep(xn_ref[0]) * next_ok], axis=0)
    n_slab = h_cat.shape[1] // lanes
    for s in range(n_slab):
        xs_ref[s] = h_cat[:, s * lanes:(s + 1) * lanes]
    h_perm = jnp.concatenate(
        [jnp.concatenate([xs_ref[s, pl.ds(a, 8, stride=pitch), :] for s in range(n_slab)], axis=1)
         for a in range(pitch)], axis=0).astype(BF16)
    sections = [jnp.dot(h_perm, wqkv_ref[:, s * GDN_W:(s + 1) * GDN_W], preferred_element_type=F32)
                for s in range(3)]
    z = jnp.dot(h_main, wz_ref[...], preferred_element_type=F32)
    for sec, (p, out) in enumerate(zip(sections, (q_out, k_out, v_out))):
        cw = conv_ref[:, sec * GDN_W:(sec + 1) * GDN_W]

        def taps(groups):
            acc = groups[0] * cw[0:1]
            for tap in range(1, GDN_CONV_K):
                acc = acc + groups[tap] * cw[tap:tap + 1]
            return acc

        def grp(a):
            return p[8 * a:8 * (a + 1)]

        below = [pltpu.roll(grp(pitch - 2), 1, 0), pltpu.roll(grp(pitch - 1), 1, 0)]
        above = pltpu.roll(grp(0), 7, 0)
        mid = taps([p[8 * t:8 * (t + pitch - 3)] for t in range(GDN_CONV_K)])
        conv = jnp.concatenate([taps([below[0], below[1], grp(0), grp(1)]),
                                taps([below[1], grp(0), grp(1), grp(2)]),
                                mid,
                                taps([grp(pitch - 3), grp(pitch - 2), grp(pitch - 1), above])], axis=0)
        act = _silu(conv)
        for hd in range(GDN_HEADS):
            val = act[:, hd * lanes:(hd + 1) * lanes]
            if sec < 2:
                val = val * lax.rsqrt(jnp.sum(val * val, axis=-1, keepdims=True) + L2_EPS)
            if sec == 0:
                val = val * (GDN_HEAD_DIM ** -0.5)
            slab = sec * GDN_HEADS + hd
            for a in range(pitch):
                ys_ref[slab, pl.ds(a, 8, stride=pitch), :] = val[8 * a:8 * (a + 1)]
            out[0, hd] = ys_ref[slab, CONV_HALO:CONV_HALO + TOK_TILE, :].astype(BF16)
    sz_out[0] = _silu(z).astype(BF16)

    nh2 = 2 * GDN_HEADS
    ab = jnp.dot(h_main, wab_ref[...], preferred_element_type=F32)
    gdec = -jnp.exp(alog_ref[...]) * jax.nn.softplus(ab[:, :nh2] + dtb_ref[...])
    gb_out[0] = jnp.concatenate([gdec, jax.nn.sigmoid(ab[:, nh2:])], axis=-1)
    abt = lax.dot_general(wabt_ref[...], h_main, _NT, preferred_element_type=F32)
    gdec_t = -jnp.exp(alogt_ref[...]) * jax.nn.softplus(abt[:nh2] + dtbt_ref[...])
    gbt_out[0] = jnp.concatenate([gdec_t, jax.nn.sigmoid(abt[nh2:])], axis=0)


def _gdn_proj(xa, modl, g0, wqkv, wz, wab, wab_t, conv_w, a_log, dt_bias):
    b, t, d = xa.shape
    nt = t // TOK_TILE
    per = TOK_TILE // CONV_HALO
    last = t // CONV_HALO - 1
    nh2 = 2 * GDN_HEADS
    head_out = pl.BlockSpec((1, GDN_HEADS, TOK_TILE, GDN_HEAD_DIM), lambda i, j: (i, 0, j, 0))
    head_shape = jax.ShapeDtypeStruct((b, GDN_HEADS, t, GDN_HEAD_DIM), BF16)
    return pl.pallas_call(
        _gdn_proj_body,
        grid=(b, nt),
        in_specs=[pl.BlockSpec((1, CONV_HALO, d), lambda i, j: (i, jnp.maximum(j * per - 1, 0), 0)),
                  pl.BlockSpec((1, TOK_TILE, d), lambda i, j: (i, j, 0)),
                  pl.BlockSpec((1, CONV_HALO, d), lambda i, j: (i, jnp.minimum((j + 1) * per, last), 0)),
                  pl.BlockSpec((1, 1, 6, d), lambda i, j: (i, jnp.minimum(j, 1), 0, 0)),
                  _const_spec((1, d)),
                  _const_spec(wqkv.shape), _const_spec(wz.shape), _const_spec(wab.shape),
                  _const_spec(wab_t.shape), _const_spec(conv_w.shape),
                  _const_spec((1, nh2)), _const_spec((1, nh2)),
                  _const_spec((nh2, 1)), _const_spec((nh2, 1))],
        out_specs=[head_out, head_out, head_out,
                   pl.BlockSpec((1, TOK_TILE, d), lambda i, j: (i, j, 0)),
                   pl.BlockSpec((1, TOK_TILE, 2 * nh2), lambda i, j: (i, j, 0)),
                   pl.BlockSpec((1, 2 * nh2, TOK_TILE), lambda i, j: (i, 0, j))],
        out_shape=[head_shape, head_shape, head_shape,
                   jax.ShapeDtypeStruct((b, t, d), BF16),
                   jax.ShapeDtypeStruct((b, t, 2 * nh2), F32),
                   jax.ShapeDtypeStruct((b, 2 * nh2, t), F32)],
        scratch_shapes=[pltpu.VMEM((d // GDN_HEAD_DIM, TOK_TILE + 2 * CONV_HALO, GDN_HEAD_DIM), F32),
                        pltpu.VMEM((3 * GDN_HEADS, TOK_TILE + 2 * CONV_HALO, GDN_HEAD_DIM), F32)],
        compiler_params=_params(("parallel", "parallel")),
        name="gdn_in_proj",
    )(xa, xa, xa, modl, g0, wqkv, wz, wab, wab_t, conv_w,
      a_log.reshape(1, nh2), dt_bias.reshape(1, nh2), a_log.reshape(nh2, 1), dt_bias.reshape(nh2, 1))


def _gdn_scan_body(qf, kf, vf, qb, kb, vb, gbf, gbb, gtf, gtb, of, ob, s_ref):
    @pl.when(pl.program_id(1) == 0)
    def _():
        s_ref[...] = jnp.zeros_like(s_ref)

    c = GDN_CHUNK
    row = lax.broadcasted_iota(jnp.int32, (c, c), 0)
    col = lax.broadcasted_iota(jnp.int32, (c, c), 1)
    lower = jnp.where(row >= col, 1.0, 0.0)
    upper = jnp.where(row <= col, 1.0, 0.0)
    row = lax.broadcasted_iota(jnp.int32, (c, 2 * c), 0)
    col = lax.broadcasted_iota(jnp.int32, (c, 2 * c), 1)
    nh2 = 2 * GDN_HEADS
    hi = lax.Precision.HIGHEST
    n_sub = qf.shape[2] // c
    refs = (qf, kf, vf, qb, kb, vb, gbf, gbb, gtf, gtb, of, ob)
    chunks = [[] for _ in range(n_sub)]
    local = [_gdn_local_stages(chunks[sub], sub, n_sub - 1 - sub, c, row, col, lower, upper, nh2, hi, refs)
             for sub in range(n_sub)]
    carried = [_gdn_state_stages(chunks[sub], c, s_ref) for sub in range(n_sub)]
    for _ in local[0]:
        pass
    for sub in range(n_sub):
        ahead = local[sub + 1] if sub + 1 < n_sub else iter(())
        for _ in carried[sub]:
            for _ in range(LOCAL_STAGES_PER_STATE_STAGE):
                next(ahead, None)
        for _ in ahead:
            pass


def _gdn_chunk_chains(sub_f, sub_b, c, row, col, lower, upper, nh2, hi,
                      qf, kf, vf, qb, kb, vb, gbf, gbb, gtf, gtb, of, ob):
    dh = GDN_HEAD_DIM
    left = col < c
    zeros = jnp.zeros((c, dh), BF16)

    def wide(x0, x1):
        return jnp.concatenate([jnp.broadcast_to(x0, (c, dh)), jnp.broadcast_to(x1, (c, dh))], axis=1)

    pairs = []
    for direction, (q_ref, k_ref, v_ref, gb_ref, gt_ref, o_ref, sub) in enumerate(
            ((qf, kf, vf, gbf, gtf, of, sub_f), (qb, kb, vb, gbb, gtb, ob, sub_b))):
        r0 = sub * c
        gb = gb_ref[0, r0:r0 + c, :]
        gt = gt_ref[0, :, r0:r0 + c]
        tri_c, tri_r = (lower, upper) if direction == 0 else (upper, lower)
        gc = jnp.dot(tri_c, gb[:, :nh2], precision=hi, preferred_element_type=F32)
        gr = jnp.dot(gt[:nh2], tri_r, precision=hi, preferred_element_type=F32)
        incl = (row >= (col & (c - 1))) if direction == 0 else (row <= (col & (c - 1)))
        strict = (row > (col & (c - 1))) if direction == 0 else (row < (col & (c - 1)))
        for pr in range(GDN_HEADS // 2):
            h0 = 2 * pr
            ch = direction * GDN_HEADS + h0
            gcol0, gcol1 = gc[:, ch:ch + 1], gc[:, ch + 1:ch + 2]
            beta0, beta1 = gb[:, nh2 + ch:nh2 + ch + 1], gb[:, nh2 + ch + 1:nh2 + ch + 2]
            gcol = jnp.where(left, gcol0, gcol1)
            grow = jnp.concatenate([gr[ch:ch + 1, :], gr[ch + 1:ch + 2, :]], axis=1)
            edge = c - 1 if direction == 0 else 0
            tot0, tot1 = gcol0[edge:edge + 1], gcol1[edge:edge + 1]
            k0, k1 = k_ref[0, h0, r0:r0 + c, :], k_ref[0, h0 + 1, r0:r0 + c, :]
            q0, q1 = q_ref[0, h0, r0:r0 + c, :], q_ref[0, h0 + 1, r0:r0 + c, :]
            pairs.append(dict(
                idx=direction * (GDN_HEADS // 2) + pr, h0=h0, lower=direction == 0, strict=strict,
                o_ref=o_ref, r0=r0, left=left,
                kq=jnp.concatenate([jnp.concatenate([k0, q0], axis=0),
                                    jnp.concatenate([k1, q1], axis=0)], axis=1),
                k_diag=jnp.concatenate([jnp.concatenate([k0, zeros], axis=1),
                                        jnp.concatenate([zeros, k1], axis=1)], axis=0),
                k_rows=jnp.concatenate([k0, k1], axis=0),
                v=jnp.concatenate([v_ref[0, h0, r0:r0 + c, :], v_ref[0, h0 + 1, r0:r0 + c, :]], axis=1),
                beta=jnp.where(left, beta0, beta1), beta_w=wide(beta0, beta1),
                egc_w=wide(jnp.exp(gcol0), jnp.exp(gcol1)),
                dec_w=wide(jnp.exp(tot0 - gcol0), jnp.exp(tot1 - gcol1)),
                etot_w=jnp.concatenate([jnp.broadcast_to(jnp.exp(tot0), (1, dh)),
                                        jnp.broadcast_to(jnp.exp(tot1), (1, dh))], axis=1),
                decay=jnp.where(incl, jnp.exp(jnp.where(incl, gcol - grow, 0.0)), 0.0)))
    return pairs


def _gdn_local_stages(chains, sub_f, sub_b, c, row, col, lower, upper, nh2, hi, refs):
    chains.extend(_gdn_chunk_chains(sub_f, sub_b, c, row, col, lower, upper, nh2, hi, *refs))
    for w in chains:
        w["gram"] = lax.dot_general(w["kq"], w["k_diag"], _NT, preferred_element_type=F32)
    yield
    ms = [jnp.where(w["strict"], w["beta"] * w["gram"][:c] * w["decay"], 0.0) for w in chains]
    left = col < c
    colh = col & (c - 1)

    def diag2(x):
        return jnp.concatenate([jnp.where(left, x, 0.0), jnp.where(left, 0.0, x)], axis=0).astype(BF16)

    t_offs = None
    k = 1
    while k < c:
        same = (row ^ colh) < 2 * k
        joins = {True: same & ((row & k) != 0) & ((colh & k) == 0),
                 False: same & ((colh & k) != 0) & ((row & k) == 0)}
        parts = [jnp.where(joins[w["lower"]], m, 0.0) for m, w in zip(ms, chains)]
        if t_offs is None:
            t_offs = [-a for a in parts]
        else:
            zs = [a + jnp.dot(t.astype(BF16), diag2(a), preferred_element_type=F32)
                  for a, t in zip(parts, t_offs)]
            yield
            t_offs = [t - z - jnp.dot(z.astype(BF16), diag2(t), preferred_element_type=F32)
                      for t, z in zip(t_offs, zs)]
            yield
        k *= 2
    eye = jnp.where(row == colh, 1.0, 0.0)
    for w, t in zip(chains, t_offs):
        w["tmat"] = (t + eye).astype(BF16)


def _gdn_state_stages(chains, c, s_ref):
    dh = GDN_HEAD_DIM

    def diag2(x):
        z = jnp.zeros((x.shape[0], dh), x.dtype)
        return jnp.concatenate([jnp.concatenate([x[:, :dh], z], axis=1),
                                jnp.concatenate([z, x[:, dh:]], axis=1)], axis=0)

    for w in chains:
        w["state"] = s_ref[w["idx"]]
        w["ks"] = jnp.dot(w["kq"], diag2(w["state"].astype(BF16)), preferred_element_type=F32)
    yield
    for w in chains:
        resid = (w["v"].astype(F32) - w["ks"][:c] * w["egc_w"]) * w["beta_w"]
        w["v_new"] = jnp.dot(w["tmat"], diag2(resid.astype(BF16)), preferred_element_type=F32)
        w["qs"] = w["ks"][c:] * w["egc_w"]
    yield
    for w in chains:
        attn = (w["gram"][c:] * w["decay"]).astype(BF16)
        o = w["qs"] + jnp.dot(attn, diag2(w["v_new"].astype(BF16)), preferred_element_type=F32)
        for m in range(2):
            w["o_ref"][0, w["h0"] + m, w["r0"]:w["r0"] + c, :] = o[:, m * dh:(m + 1) * dh].astype(w["o_ref"].dtype)
    yield
    for w in chains:
        v_dec = diag2((w["v_new"] * w["dec_w"]).astype(BF16))
        s_ref[w["idx"]] = (w["state"] * w["etot_w"]
                           + lax.dot_general(w["k_rows"], v_dec, _TN, preferred_element_type=F32))
    yield


def _gdn_scan(q, k, v, gb, gbt):
    b, nh, t, dh = q.shape
    nt = t // TOK_TILE
    nh2 = 2 * GDN_HEADS

    def bwd(s):
        return jnp.where(s == 0, 0, nt - s)

    head_f = pl.BlockSpec((1, nh, TOK_TILE, dh), lambda i, s: (i, 0, s, 0))
    head_b = pl.BlockSpec((1, nh, TOK_TILE, dh), lambda i, s: (i, 0, bwd(s), 0))
    out_shape = jax.ShapeDtypeStruct((b, nh, t, dh), BF16)
    return pl.pallas_call(
        _gdn_scan_body,
        grid=(b, nt),
        in_specs=[head_f, head_f, head_f, head_b, head_b, head_b,
                  pl.BlockSpec((1, TOK_TILE, 2 * nh2), lambda i, s: (i, s, 0)),
                  pl.BlockSpec((1, TOK_TILE, 2 * nh2), lambda i, s: (i, bwd(s), 0)),
                  pl.BlockSpec((1, 2 * nh2, TOK_TILE), lambda i, s: (i, 0, s)),
                  pl.BlockSpec((1, 2 * nh2, TOK_TILE), lambda i, s: (i, 0, bwd(s)))],
        out_specs=[head_f, head_b],
        out_shape=[out_shape, out_shape],
        scratch_shapes=[pltpu.VMEM((nh2 // 2, dh, 2 * dh), F32)],
        compiler_params=_params(("arbitrary", "arbitrary")),
        name="gdn_chunk_scan",
    )(q, k, v, q, k, v, gb, gb, gbt, gbt)


def _rope_tables(n_lat):
    rows = n_lat // GRID_W
    row_ids = jnp.repeat(jnp.arange(rows, dtype=F32), GRID_W)[:n_lat]
    col_ids = jnp.tile(jnp.arange(GRID_W, dtype=F32), rows)[:n_lat]
    axis_dim = HEAD_DIM // 2
    inv_freq = ROPE_THETA ** (-jnp.arange(0, axis_dim, 2, dtype=F32) / axis_dim)
    ang_r = row_ids[:, None] * inv_freq
    ang_c = col_ids[:, None] * inv_freq
    ang = jnp.concatenate([ang_r, ang_r, ang_c, ang_c], axis=-1)
    cos = jnp.concatenate([jnp.ones((CTX_LEN, HEAD_DIM), F32), jnp.cos(ang)], axis=0)
    sin = jnp.concatenate([jnp.zeros((CTX_LEN, HEAD_DIM), F32), jnp.sin(ang)], axis=0)
    sign = jnp.tile(jnp.repeat(jnp.array([-1.0, 1.0], F32), HEAD_DIM // 4), 2)
    return cos.T, (sin * sign).T


def kernel(x, c, ctx, c_ctx, ada_w, ada_b, norm_g, attn_w_in, attn_w_out, diff_lambda, diff_subln_g,
           gqa_qk_g, gdn_w_in, gdn_conv_w, gdn_a_log, gdn_dt_bias, gdn_norm_g, gdn_w_out,
           ffn_w_gate_up, ffn_w_down):
    b, n_lat, d = x.shape
    depth = ada_w.shape[0]
    assert d == D_MODEL and ctx.shape[1] == CTX_LEN and n_lat % TOK_TILE == 0
    rows = -(-(b + 1) // 8) * 8
    cc = jnp.concatenate([c, c_ctx[None], jnp.zeros((rows - b - 1, d), F32)], axis=0)
    mods = _modulation(cc, ada_w, ada_b)
    cos_t, sin_t = _rope_tables(n_lat)
    xa = None

    for l in range(depth):
        i = l // 2
        last = l == depth - 1
        ml = mods[l]
        modl = jnp.stack([jnp.broadcast_to(ml[b].reshape(1, 6, d), (b, 6, d)),
                          ml[:b].reshape(b, 6, d)], axis=1)
        g = norm_g[l]
        wgu = ffn_w_gate_up[l].astype(BF16)
        wd = ffn_w_down[l].astype(BF16)
        if l % 2 == 0:
            lam_init = 0.8 - 0.6 * math.exp(-0.3 * l)
            w = attn_w_in[i]
            o_dv = 2 * DIFF_QK_W
            o_gq = o_dv + DIFF_V_W
            o_gk = o_gq + GQA_Q_W
            o_gv = o_gk + GQA_KV_W
            w_t = jnp.concatenate([w[:, :DIFF_QK_W], w[:, o_gq:o_gk],
                                   w[:, DIFF_QK_W:o_dv], w[:, o_gk:o_gv],
                                   w[:, o_dv:o_gq], w[:, o_gv:]],
                                  axis=1).T.astype(BF16)
            qkg = gqa_qk_g[i].reshape(2, HEAD_DIM, 1)
            if l == 0:
                qt, k, vt, xa = _attn_proj(x, modl, g[0:1], w_t, cos_t, sin_t, qkg, ctx=ctx)
            else:
                qt, k, vt = _attn_proj(xa, modl, g[0:1], w_t, cos_t, sin_t, qkg)
            o = _attention(qt, k, vt, diff_lambda[i], diff_subln_g[i].reshape(DIFF_V_DIM, 1), lam_init)
            xa = _post_ffn(xa, o, modl, g, attn_w_out[i].astype(BF16), wgu, wd, gdn=False, latent_only=last)
        else:
            w = gdn_w_in[i]
            wab = w[:, 4 * GDN_W:]
            q, k, v, sz, gb, gbt = _gdn_proj(
                xa, modl, g[0:1], w[:, :3 * GDN_W].astype(BF16), w[:, 3 * GDN_W:4 * GDN_W].astype(BF16),
                wab.astype(BF16), wab.T.astype(BF16), gdn_conv_w[i], gdn_a_log[i], gdn_dt_bias[i])
            of, ob = _gdn_scan(q, k, v, gb, gbt)
            xa = _post_ffn(xa, (of, ob, sz, gdn_norm_g[i].reshape(1, GDN_HEAD_DIM)), modl, g,
                           gdn_w_out[i].astype(BF16), wgu, wd, gdn=True, latent_only=last)
    return xa
```

```python
import functools
import math

import jax
import jax.numpy as jnp
from jax import lax
from jax.experimental import pallas as pl
from jax.experimental.pallas import tpu as pltpu

F32 = jnp.float32
BF16 = jnp.bfloat16

D_MODEL = 1024
CTX_LEN = 256
GRID_W = 64
RMS_EPS = 1e-6
L2_EPS = 1e-6
ROPE_THETA = 10000.0

HEAD_DIM = 64
ATTN_SCALE = HEAD_DIM ** -0.5
DIFF_HEADS = 4
DIFF_V_DIM = 2 * HEAD_DIM
GQA_Q_HEADS = 8
GQA_KV_HEADS = 2
GQA_REP = GQA_Q_HEADS // GQA_KV_HEADS
DIFF_QK_W = DIFF_HEADS * 2 * HEAD_DIM
DIFF_V_W = DIFF_HEADS * DIFF_V_DIM
GQA_Q_W = GQA_Q_HEADS * HEAD_DIM
GQA_KV_W = GQA_KV_HEADS * HEAD_DIM
Q_W = DIFF_QK_W + GQA_Q_W
K_W = DIFF_QK_W + GQA_KV_W
V_W = DIFF_V_W + GQA_KV_W
SUM_ROWS = 16
VT_ROWS = V_W + (DIFF_HEADS + GQA_KV_HEADS) * SUM_ROWS
LOG2_E = math.log2(math.e)
N_MAPS = 2 * DIFF_HEADS + GQA_Q_HEADS
QK_AHEAD = 6

GDN_HEADS = 8
GDN_HEAD_DIM = 128
GDN_W = GDN_HEADS * GDN_HEAD_DIM
GDN_CONV_K = 4
GDN_CONV_LEFT = 2
GDN_CHUNK = 64
SCAN_LOCAL_STAGES = 1 + 2 * (GDN_CHUNK.bit_length() - 2)
SCAN_STATE_STAGES = 4
LOCAL_STAGES_PER_STATE_STAGE = -(-SCAN_LOCAL_STAGES // SCAN_STATE_STAGES)
SCAN_LOCAL_AHEAD = 1
V7X_SUBLANES = 8
CONV_HALO = V7X_SUBLANES

FFN_HIDDEN = 2816
FFN_CHUNK = 256
FFN_GROUP = 2
PROJ_GROUP = 2

TOK_TILE = CTX_LEN
KV_CHUNK = TOK_TILE
V7X_VMEM_BYTES = 64 * 1024 * 1024
VMEM_LIMIT = V7X_VMEM_BYTES * 7 // 8

_NT = (((1,), (1,)), ((), ()))
_TN = (((0,), (0,)), ((), ()))


def _rms(x, g):
    return x * lax.rsqrt(jnp.mean(x * x, axis=-1, keepdims=True) + RMS_EPS) * g


def _silu(x):
    return x * jax.nn.sigmoid(x)


def _params(sem):
    return pltpu.CompilerParams(dimension_semantics=sem, vmem_limit_bytes=VMEM_LIMIT)


def _const_spec(shape):
    n = len(shape)
    return pl.BlockSpec(shape, lambda *_: (0,) * n)


def _mod_body(c_ref, w_ref, b_ref, o_ref):
    sc = _silu(c_ref[...])
    o_ref[0] = jnp.dot(sc, w_ref[0], precision=lax.Precision.HIGHEST,
                       preferred_element_type=F32) + b_ref[0]


def _modulation(cc, ada_w, ada_b):
    depth, d, w6 = ada_w.shape
    rows = cc.shape[0]
    nblk = w6 // d
    return pl.pallas_call(
        _mod_body,
        grid=(depth, nblk),
        in_specs=[pl.BlockSpec((rows, d), lambda l, j: (0, 0)),
                  pl.BlockSpec((1, d, d), lambda l, j: (l, 0, j)),
                  pl.BlockSpec((1, 1, d), lambda l, j: (l, 0, j))],
        out_specs=pl.BlockSpec((1, rows, d), lambda l, j: (l, 0, j)),
        out_shape=jax.ShapeDtypeStruct((depth, rows, w6), F32),
        compiler_params=_params(("parallel", "parallel")),
        name="adaln_mod",
    )(cc, ada_w, ada_b.reshape(depth, 1, w6))


def _attn_proj_body(group, nt, assemble, *refs):
    per = 5 if assemble else 4
    tiles = [refs[per * m:per * (m + 1)] for m in range(group)]
    g_ref, w_ref, qkg_ref = refs[per * group:per * group + 3]
    outs = refs[per * group + 3:]
    hs = []
    for m, tile in enumerate(tiles):
        if assemble:
            j = (pl.program_id(0) * group + m) % nt
            x = jnp.where(j == 0, tile[0][0], tile[1][0])
            outs[3][m] = x
        else:
            x = tile[0][0]
        mod = tile[-3][0, 0]
        hs.append((_rms(x, g_ref[...]) * (1.0 + mod[1:2]) + mod[0:1]).astype(BF16))
    for m, (tile, h) in enumerate(zip(tiles, hs)):
        _attn_proj_heads(m, h, w_ref, tile[-2], tile[-1], qkg_ref, *outs[:3])


def _attn_proj_heads(m, h, w_ref, cos_ref, sin_ref, qkg_ref, qt_out, k_out, vt_out):
    def project(r0, rows):
        return lax.dot_general(w_ref[r0:r0 + rows, :], h, _NT, preferred_element_type=F32)

    pk = project(Q_W, K_W)
    pq = project(0, Q_W)
    pv = project(Q_W + K_W, V_W)
    cos = cos_ref[...]
    sin = sin_ref[...]

    def rope(xh):
        swapped = jnp.concatenate([xh[16:32], xh[0:16], xh[48:64], xh[32:48]], axis=0)
        return xh * cos + swapped * sin

    def norm(xh, g):
        r = lax.rsqrt(jnp.mean(xh * xh, axis=0, keepdims=True) + RMS_EPS)
        return xh * r * g

    gq_g = qkg_ref[0]
    gk_g = qkg_ref[1]
    k_heads = []
    for j in range(K_W // HEAD_DIM):
        r0 = j * HEAD_DIM
        xh = pk[r0:r0 + HEAD_DIM]
        if r0 >= DIFF_QK_W:
            xh = norm(xh, gk_g)
        k_heads.append(rope(xh))
    k_out[m] = jnp.concatenate(k_heads, axis=0).T.astype(BF16)
    for j in range(Q_W // HEAD_DIM):
        r0 = j * HEAD_DIM
        xh = pq[r0:r0 + HEAD_DIM]
        if r0 >= DIFF_QK_W:
            xh = norm(xh, gq_g)
        qt_out[m, r0:r0 + HEAD_DIM, :] = (rope(xh) * (ATTN_SCALE * LOG2_E)).astype(BF16)
    ones = jnp.ones((SUM_ROWS, pv.shape[1]), F32)
    v_blocks = []
    for r0, width in ([(hd * DIFF_V_DIM, DIFF_V_DIM) for hd in range(DIFF_HEADS)]
                      + [(DIFF_V_W + grp * HEAD_DIM, HEAD_DIM) for grp in range(GQA_KV_HEADS)]):
        v_blocks += [pv[r0:r0 + width], ones]
    vt_out[m] = jnp.concatenate(v_blocks, axis=0).astype(BF16)


def _attn_proj(xa, modl, g0, w_t, cos_t, sin_t, qkg, ctx=None):
    assemble = ctx is not None
    b, t, d = xa.shape
    if assemble:
        t += CTX_LEN
    nt = t // TOK_TILE
    group = PROJ_GROUP
    assert (b * nt) % group == 0
    steps = b * nt // group

    def at(m, fn):
        def index_map(p):
            n = p * group + m
            return fn(n // nt, n % nt)
        return index_map

    in_specs, args = [], []
    for m in range(group):
        if assemble:
            in_specs += [pl.BlockSpec((1, TOK_TILE, d), at(m, lambda i, j: (i, 0, 0))),
                         pl.BlockSpec((1, TOK_TILE, d), at(m, lambda i, j: (i, jnp.maximum(j - 1, 0), 0)))]
            args += [ctx, xa]
        else:
            in_specs.append(pl.BlockSpec((1, TOK_TILE, d), at(m, lambda i, j: (i, j, 0))))
            args.append(xa)
        in_specs += [pl.BlockSpec((1, 1, 6, d), at(m, lambda i, j: (i, jnp.minimum(j, 1), 0, 0))),
                     pl.BlockSpec((HEAD_DIM, TOK_TILE), at(m, lambda i, j: (0, j))),
                     pl.BlockSpec((HEAD_DIM, TOK_TILE), at(m, lambda i, j: (0, j)))]
        args += [modl, cos_t, sin_t]
    in_specs += [_const_spec((1, d)), _const_spec(w_t.shape), _const_spec((2, HEAD_DIM, 1))]
    args += [g0, w_t, qkg]
    out_specs = [pl.BlockSpec((group, Q_W, TOK_TILE), lambda p: (p, 0, 0)),
                 pl.BlockSpec((group, TOK_TILE, K_W), lambda p: (p, 0, 0)),
                 pl.BlockSpec((group, VT_ROWS, TOK_TILE), lambda p: (p, 0, 0))]
    out_shape = [jax.ShapeDtypeStruct((b * nt, Q_W, TOK_TILE), BF16),
                 jax.ShapeDtypeStruct((b * nt, TOK_TILE, K_W), BF16),
                 jax.ShapeDtypeStruct((b * nt, VT_ROWS, TOK_TILE), BF16)]
    if assemble:
        out_specs.append(pl.BlockSpec((group, TOK_TILE, d), lambda p: (p, 0, 0)))
        out_shape.append(jax.ShapeDtypeStruct((b * nt, TOK_TILE, d), F32))
    outs = pl.pallas_call(
        functools.partial(_attn_proj_body, group, nt, assemble),
        grid=(steps,),
        in_specs=in_specs,
        out_specs=out_specs,
        out_shape=out_shape,
        compiler_params=_params(("parallel",)),
        name="attn_in_proj",
    )(*args)
    res = (outs[0].reshape(b, nt, Q_W, TOK_TILE), outs[1].reshape(b, t, K_W),
           outs[2].reshape(b, nt, VT_ROWS, TOK_TILE))
    return res + (outs[3].reshape(b, t, d),) if assemble else res


def _attn_body(lam_init, qt_ref, k_ref, vt_ref, lam_ref, subln_ref, o_ref, qpad_ref, m_ref, acc_ref):
    qi = pl.program_id(1)
    slab = 2 * HEAD_DIM
    diff_blk = DIFF_V_DIM + SUM_ROWS
    gqa_blk = HEAD_DIM + SUM_ROWS
    maps = []
    for hd in range(DIFF_HEADS):
        for mm in range(2):
            maps.append((2 * hd + mm, hd, hd * diff_blk, diff_blk))
    for hq in range(GQA_Q_HEADS):
        grp = hq // GQA_REP
        maps.append((2 * DIFF_HEADS + hq, DIFF_HEADS, DIFF_HEADS * diff_blk + grp * gqa_blk, gqa_blk))

    zeros = jnp.zeros((HEAD_DIM, qt_ref.shape[3]), BF16)
    for i, (qh, _, _, _) in enumerate(maps):
        half = (qh % 2) if qh < 2 * DIFF_HEADS else (qh - 2 * DIFF_HEADS) // GQA_REP
        qh_t = qt_ref[0, 0, qh * HEAD_DIM:(qh + 1) * HEAD_DIM, :]
        qpad_ref[i] = jnp.concatenate([qh_t, zeros] if half == 0 else [zeros, qh_t], axis=0)
    def scores(c, i):
        ks = maps[i][1]
        return jnp.dot(k_ref[0, c * KV_CHUNK:(c + 1) * KV_CHUNK, ks * slab:(ks + 1) * slab], qpad_ref[i],
                       preferred_element_type=F32)

    def run(n_chunks):
        items = [(c, i) for c in range(n_chunks) for i in range(N_MAPS)]
        pending = [scores(*items[n]) for n in range(QK_AHEAD)]
        for n, (c, i) in enumerate(items):
            s = pending.pop(0)
            if n + QK_AHEAD < len(items):
                pending.append(scores(*items[n + QK_AHEAD]))
            _, _, v_row0, v_w = maps[i]
            m_new = jnp.max(s, axis=0, keepdims=True)
            if c > 0:
                m_old = m_ref[i]
                m_new = jnp.maximum(m_old, m_new)
                alpha = jnp.exp2(m_old - m_new)
            m_ref[i] = m_new
            p = jnp.exp2(s - m_new)
            pv = jnp.dot(vt_ref[0, c, v_row0:v_row0 + v_w, :], p.astype(BF16), preferred_element_type=F32)
            acc_ref[i, 0:v_w, :] = pv if c == 0 else alpha * acc_ref[i, 0:v_w, :] + pv

    @pl.when(qi == 0)
    def _():
        run(CTX_LEN // KV_CHUNK)

    @pl.when(qi > 0)
    def _():
        run(vt_ref.shape[1])

    lv = lam_ref[...]
    lam = (jnp.exp(jnp.sum(lv[0:1] * lv[1:2], axis=-1, keepdims=True))
           - jnp.exp(jnp.sum(lv[2:3] * lv[3:4], axis=-1, keepdims=True)) + lam_init)
    def normalized(i, width):
        return acc_ref[i, 0:width, :] / acc_ref[i, width:width + 1, :]

    for hd in range(DIFF_HEADS):
        od = normalized(2 * hd, DIFF_V_DIM) - lam * normalized(2 * hd + 1, DIFF_V_DIM)
        od = od * lax.rsqrt(jnp.mean(od * od, axis=0, keepdims=True) + RMS_EPS) * subln_ref[...]
        o_ref[0, :, hd * DIFF_V_DIM:(hd + 1) * DIFF_V_DIM] = (od * (1.0 - lam_init)).T.astype(o_ref.dtype)
    for pair in range(GQA_Q_HEADS // 2):
        i0 = 2 * DIFF_HEADS + 2 * pair
        og = jnp.concatenate([normalized(i0, HEAD_DIM), normalized(i0 + 1, HEAD_DIM)], axis=0)
        c0 = DIFF_V_W + pair * slab
        o_ref[0, :, c0:c0 + slab] = og.T.astype(o_ref.dtype)


def _attention(qt, k, vt, lam_vec, subln_g, lam_init):
    b, t, _ = k.shape
    nq = t // TOK_TILE
    return pl.pallas_call(
        functools.partial(_attn_body, lam_init),
        grid=(b, nq),
        in_specs=[pl.BlockSpec((1, 1, Q_W, TOK_TILE), lambda i, j: (i, j, 0, 0)),
                  pl.BlockSpec((1, t, K_W), lambda i, j: (i, 0, 0)),
                  pl.BlockSpec((1,) + vt.shape[1:], lambda i, j: (i, 0, 0, 0)),
                  _const_spec((4, HEAD_DIM)),
                  _const_spec((DIFF_V_DIM, 1))],
        out_specs=pl.BlockSpec((1, TOK_TILE, D_MODEL), lambda i, j: (i, j, 0)),
        out_shape=jax.ShapeDtypeStruct((b, t, D_MODEL), BF16),
        scratch_shapes=[pltpu.VMEM((N_MAPS, 2 * HEAD_DIM, TOK_TILE), BF16),
                        pltpu.VMEM((N_MAPS, 1, TOK_TILE), F32),
                        pltpu.VMEM((N_MAPS, DIFF_V_DIM + SUM_ROWS, TOK_TILE), F32)],
        compiler_params=_params(("parallel", "parallel")),
        name="diff_gqa_attention",
    )(qt, k, vt, lam_vec, subln_g)


def _post_body(gdn, group, *refs):
    per = 5 if gdn else 3
    tiles = [refs[per * m:per * (m + 1)] for m in range(group)]
    rest = refs[per * group:]
    if gdn:
        ng_ref, rest = rest[0], rest[1:]
    g_ref, wo_ref, wgu_ref, wd_ref, xo_ref, a_ref = rest
    g = g_ref[...]

    def mixer_out(tile):
        if not gdn:
            return tile[1][0]
        _, of_ref, ob_ref, sz_ref, _ = tile
        parts = []
        for hd in range(GDN_HEADS):
            o = of_ref[0, hd].astype(F32) + ob_ref[0, hd].astype(F32)
            parts.append(_rms(o, ng_ref[...]))
        return (jnp.concatenate(parts, axis=-1) * sz_ref[0].astype(F32)).astype(BF16)

    ys = [jnp.dot(mixer_out(tile), wo_ref[...], preferred_element_type=F32) for tile in tiles]
    for m, (tile, y) in enumerate(zip(tiles, ys)):
        mod = tile[-1][0, 0]
        x = tile[0][0] + mod[2:3] * _rms(y, g[1:2])
        h = (_rms(x, g[2:3]) * (1.0 + mod[4:5]) + mod[3:4]).astype(BF16)
        for c in range(FFN_HIDDEN // FFN_CHUNK):
            c0 = c * FFN_CHUNK
            gate = jnp.dot(h, wgu_ref[:, c0:c0 + FFN_CHUNK], preferred_element_type=F32)
            up = jnp.dot(h, wgu_ref[:, FFN_HIDDEN + c0:FFN_HIDDEN + c0 + FFN_CHUNK],
                         preferred_element_type=F32)
            a_ref[m, :, c0:c0 + FFN_CHUNK] = (_silu(gate) * up).astype(BF16)
        ff = jnp.dot(a_ref[m], wd_ref[...], preferred_element_type=F32)
        xo_ref[0, m * TOK_TILE:(m + 1) * TOK_TILE, :] = x + mod[5:6] * _rms(ff, g[3:4])


def _post_ffn(xa, mixer_out, modl, g, wo, wgu, wd, gdn, latent_only):
    b, t, d = xa.shape
    skip = CTX_LEN // TOK_TILE if latent_only else 0
    nt = t // TOK_TILE - skip
    group = FFN_GROUP
    assert (b * nt) % group == 0

    def at(m, fn):
        def index_map(p):
            n = p * group + m
            return fn(n // nt, n % nt + skip)
        return index_map

    in_specs, args = [], []
    for m in range(group):
        tile = pl.BlockSpec((1, TOK_TILE, d), at(m, lambda i, j: (i, j, 0)))
        if gdn:
            of, ob, sz, ng = mixer_out
            head_tile = pl.BlockSpec((1, GDN_HEADS, TOK_TILE, GDN_HEAD_DIM), at(m, lambda i, j: (i, 0, j, 0)))
            in_specs += [tile, head_tile, head_tile, tile]
            args += [xa, of, ob, sz]
        else:
            in_specs += [tile, tile]
            args += [xa, mixer_out]
        in_specs.append(pl.BlockSpec((1, 1, 6, d), at(m, lambda i, j: (i, jnp.minimum(j, 1), 0, 0))))
        args.append(modl)
    if gdn:
        in_specs.append(_const_spec((1, GDN_HEAD_DIM)))
        args.append(ng)
    in_specs += [_const_spec((4, d)), _const_spec(wo.shape), _const_spec(wgu.shape), _const_spec(wd.shape)]
    args += [g, wo, wgu, wd]
    rows = group * TOK_TILE
    out = pl.pallas_call(
        functools.partial(_post_body, gdn, group),
        grid=(b * nt // group,),
        in_specs=in_specs,
        out_specs=pl.BlockSpec((1, rows, d), lambda p: (p, 0, 0)),
        out_shape=jax.ShapeDtypeStruct((b * nt // group, rows, d), F32),
        scratch_shapes=[pltpu.VMEM((group, TOK_TILE, FFN_HIDDEN), BF16)],
        compiler_params=_params(("parallel",)),
        name="out_proj_ffn_gdn" if gdn else "out_proj_ffn_attn",
    )(*args)
    return out.reshape(b, nt * TOK_TILE, d)


def _gdn_proj_body(xp_ref, x_ref, xn_ref, mod_ref, g_ref, wqkv_ref, wz_ref, wab_ref, wabt_ref,
                   conv_ref, alog_ref, dtb_ref, alogt_ref, dtbt_ref,
                   q_out, k_out, v_out, sz_out, gb_out, gbt_out, xs_ref, ys_ref):
    j = pl.program_id(1)
    nt = pl.num_programs(1)
    mod = mod_ref[0, 0]
    g = g_ref[...]
    lanes = GDN_HEAD_DIM
    rows = TOK_TILE + 2 * CONV_HALO
    pitch = rows // 8

    def prep(xv):
        return _rms(xv, g) * (1.0 + mod[1:2]) + mod[0:1]

    prev_ok = jnp.where(j >= 2, 1.0, 0.0)
    next_ok = jnp.where(jnp.logical_and(j >= 1, j < nt - 1), 1.0, 0.0)
    h_f32 = prep(x_ref[0])
    h_main = h_f32.astype(BF16)
    h_cat = jnp.concatenate([prep(xp_ref[0]) * prev_ok, h_f32, prep(xn_ref[0]) * next_ok], axis=0)
    n_slab = h_cat.shape[1] // lanes
    for s in range(n_slab):
        xs_ref[s] = h_cat[:, s * lanes:(s + 1) * lanes]
    h_perm = jnp.concatenate(
        [jnp.concatenate([xs_ref[s, pl.ds(a, 8, stride=pitch), :] for s in range(n_slab)], axis=1)
         for a in range(pitch)], axis=0).astype(BF16)
    sections = [jnp.dot(h_perm, wqkv_ref[:, s * GDN_W:(s + 1) * GDN_W], preferred_element_type=F32)
                for s in range(3)]
    z = jnp.dot(h_main, wz_ref[...], preferred_element_type=F32)
    for sec, (p, out) in enumerate(zip(sections, (q_out, k_out, v_out))):
        cw = conv_ref[:, sec * GDN_W:(sec + 1) * GDN_W]

        def taps(groups):
            acc = groups[0] * cw[0:1]
            for tap in range(1, GDN_CONV_K):
                acc = acc + groups[tap] * cw[tap:tap + 1]
            return acc

        def grp(a):
            return p[8 * a:8 * (a + 1)]

        below = [pltpu.roll(grp(pitch - 2), 1, 0), pltpu.roll(grp(pitch - 1), 1, 0)]
        above = pltpu.roll(grp(0), 7, 0)
        mid = taps([p[8 * t:8 * (t + pitch - 3)] for t in range(GDN_CONV_K)])
        conv = jnp.concatenate([taps([below[0], below[1], grp(0), grp(1)]),
                                taps([below[1], grp(0), grp(1), grp(2)]),
                                mid,
                                taps([grp(pitch - 3), grp(pitch - 2), grp(pitch - 1), above])], axis=0)
        act = _silu(conv)
        for hd in range(GDN_HEADS):
            val = act[:, hd * lanes:(hd + 1) * lanes]
            if sec < 2:
                val = val * lax.rsqrt(jnp.sum(val * val, axis=-1, keepdims=True) + L2_EPS)
            if sec == 0:
                val = val * (GDN_HEAD_DIM ** -0.5)
            slab = sec * GDN_HEADS + hd
            for a in range(pitch):
                ys_ref[slab, pl.ds(a, 8, stride=pitch), :] = val[8 * a:8 * (a + 1)]
            out[0, hd] = ys_ref[slab, CONV_HALO:CONV_HALO + TOK_TILE, :].astype(BF16)
    sz_out[0] = _silu(z).astype(BF16)

    nh2 = 2 * GDN_HEADS
    ab = jnp.dot(h_main, wab_ref[...], preferred_element_type=F32)
    gdec = -jnp.exp(alog_ref[...]) * jax.nn.softplus(ab[:, :nh2] + dtb_ref[...])
    gb_out[0] = jnp.concatenate([gdec, jax.nn.sigmoid(ab[:, nh2:])], axis=-1)
    abt = lax.dot_general(wabt_ref[...], h_main, _NT, preferred_element_type=F32)
    gdec_t = -jnp.exp(alogt_ref[...]) * jax.nn.softplus(abt[:nh2] + dtbt_ref[...])
    gbt_out[0] = jnp.concatenate([gdec_t, jax.nn.sigmoid(abt[nh2:])], axis=0)


def _gdn_proj(xa, modl, g0, wqkv, wz, wab, wab_t, conv_w, a_log, dt_bias):
    b, t, d = xa.shape
    nt = t // TOK_TILE
    per = TOK_TILE // CONV_HALO
    last = t // CONV_HALO - 1
    nh2 = 2 * GDN_HEADS
    head_out = pl.BlockSpec((1, GDN_HEADS, TOK_TILE, GDN_HEAD_DIM), lambda i, j: (i, 0, j, 0))
    head_shape = jax.ShapeDtypeStruct((b, GDN_HEADS, t, GDN_HEAD_DIM), BF16)
    return pl.pallas_call(
        _gdn_proj_body,
        grid=(b, nt),
        in_specs=[pl.BlockSpec((1, CONV_HALO, d), lambda i, j: (i, jnp.maximum(j * per - 1, 0), 0)),
                  pl.BlockSpec((1, TOK_TILE, d), lambda i, j: (i, j, 0)),
                  pl.BlockSpec((1, CONV_HALO, d), lambda i, j: (i, jnp.minimum((j + 1) * per, last), 0)),
                  pl.BlockSpec((1, 1, 6, d), lambda i, j: (i, jnp.minimum(j, 1), 0, 0)),
                  _const_spec((1, d)),
                  _const_spec(wqkv.shape), _const_spec(wz.shape), _const_spec(wab.shape),
                  _const_spec(wab_t.shape), _const_spec(conv_w.shape),
                  _const_spec((1, nh2)), _const_spec((1, nh2)),
                  _const_spec((nh2, 1)), _const_spec((nh2, 1))],
        out_specs=[head_out, head_out, head_out,
                   pl.BlockSpec((1, TOK_TILE, d), lambda i, j: (i, j, 0)),
                   pl.BlockSpec((1, TOK_TILE, 2 * nh2), lambda i, j: (i, j, 0)),
                   pl.BlockSpec((1, 2 * nh2, TOK_TILE), lambda i, j: (i, 0, j))],
        out_shape=[head_shape, head_shape, head_shape,
                   jax.ShapeDtypeStruct((b, t, d), BF16),
                   jax.ShapeDtypeStruct((b, t, 2 * nh2), F32),
                   jax.ShapeDtypeStruct((b, 2 * nh2, t), F32)],
        scratch_shapes=[pltpu.VMEM((d // GDN_HEAD_DIM, TOK_TILE + 2 * CONV_HALO, GDN_HEAD_DIM), F32),
                        pltpu.VMEM((3 * GDN_HEADS, TOK_TILE + 2 * CONV_HALO, GDN_HEAD_DIM), F32)],
        compiler_params=_params(("parallel", "parallel")),
        name="gdn_in_proj",
    )(xa, xa, xa, modl, g0, wqkv, wz, wab, wab_t, conv_w,
      a_log.reshape(1, nh2), dt_bias.reshape(1, nh2), a_log.reshape(nh2, 1), dt_bias.reshape(nh2, 1))


def _gdn_scan_body(qf, kf, vf, qb, kb, vb, gbf, gbb, gtf, gtb, of, ob, s_ref):
    @pl.when(pl.program_id(1) == 0)
    def _():
        s_ref[...] = jnp.zeros_like(s_ref)

    c = GDN_CHUNK
    row = lax.broadcasted_iota(jnp.int32, (c, c), 0)
    col = lax.broadcasted_iota(jnp.int32, (c, c), 1)
    lower = jnp.where(row >= col, 1.0, 0.0)
    upper = jnp.where(row <= col, 1.0, 0.0)
    row = lax.broadcasted_iota(jnp.int32, (c, 2 * c), 0)
    col = lax.broadcasted_iota(jnp.int32, (c, 2 * c), 1)
    nh2 = 2 * GDN_HEADS
    hi = lax.Precision.HIGHEST
    n_sub = qf.shape[2] // c
    refs = (qf, kf, vf, qb, kb, vb, gbf, gbb, gtf, gtb, of, ob)
    chunks = [[] for _ in range(n_sub)]
    local = [_gdn_local_stages(chunks[sub], sub, n_sub - 1 - sub, c, row, col, lower, upper, nh2, hi, refs)
             for sub in range(n_sub)]
    carried = [_gdn_state_stages(chunks[sub], c, s_ref) for sub in range(n_sub)]
    for first in local[:SCAN_LOCAL_AHEAD]:
        for _ in first:
            pass
    for sub in range(n_sub):
        ahead = local[sub + SCAN_LOCAL_AHEAD] if sub + SCAN_LOCAL_AHEAD < n_sub else iter(())
        for _ in carried[sub]:
            for _ in range(LOCAL_STAGES_PER_STATE_STAGE):
                next(ahead, None)
        for _ in ahead:
            pass


def _gdn_chunk_chains(sub_f, sub_b, c, row, col, lower, upper, nh2, hi,
                      qf, kf, vf, qb, kb, vb, gbf, gbb, gtf, gtb, of, ob):
    dh = GDN_HEAD_DIM
    left = col < c
    zeros = jnp.zeros((c, dh), BF16)

    def wide(x0, x1):
        return jnp.concatenate([jnp.broadcast_to(x0, (c, dh)), jnp.broadcast_to(x1, (c, dh))], axis=1)

    pairs = []
    for direction, (q_ref, k_ref, v_ref, gb_ref, gt_ref, o_ref, sub) in enumerate(
            ((qf, kf, vf, gbf, gtf, of, sub_f), (qb, kb, vb, gbb, gtb, ob, sub_b))):
        r0 = sub * c
        gb = gb_ref[0, r0:r0 + c, :]
        gt = gt_ref[0, :, r0:r0 + c]
        tri_c, tri_r = (lower, upper) if direction == 0 else (upper, lower)
        gc = jnp.dot(tri_c, gb[:, :nh2], precision=hi, preferred_element_type=F32)
        gr = jnp.dot(gt[:nh2], tri_r, precision=hi, preferred_element_type=F32)
        incl = (row >= (col & (c - 1))) if direction == 0 else (row <= (col & (c - 1)))
        strict = (row > (col & (c - 1))) if direction == 0 else (row < (col & (c - 1)))
        for pr in range(GDN_HEADS // 2):
            h0 = 2 * pr
            ch = direction * GDN_HEADS + h0
            gcol0, gcol1 = gc[:, ch:ch + 1], gc[:, ch + 1:ch + 2]
            beta0, beta1 = gb[:, nh2 + ch:nh2 + ch + 1], gb[:, nh2 + ch + 1:nh2 + ch + 2]
            gcol = jnp.where(left, gcol0, gcol1)
            grow = jnp.concatenate([gr[ch:ch + 1, :], gr[ch + 1:ch + 2, :]], axis=1)
            edge = c - 1 if direction == 0 else 0
            tot0, tot1 = gcol0[edge:edge + 1], gcol1[edge:edge + 1]
            k0, k1 = k_ref[0, h0, r0:r0 + c, :], k_ref[0, h0 + 1, r0:r0 + c, :]
            q0, q1 = q_ref[0, h0, r0:r0 + c, :], q_ref[0, h0 + 1, r0:r0 + c, :]
            pairs.append(dict(
                idx=direction * (GDN_HEADS // 2) + pr, h0=h0, lower=direction == 0, strict=strict,
                o_ref=o_ref, r0=r0, left=left,
                kq=jnp.concatenate([jnp.concatenate([k0, q0], axis=0),
                                    jnp.concatenate([k1, q1], axis=0)], axis=1),
                k_diag=jnp.concatenate([jnp.concatenate([k0, zeros], axis=1),
                                        jnp.concatenate([zeros, k1], axis=1)], axis=0),
                k_rows=jnp.concatenate([k0, k1], axis=0),
                v=jnp.concatenate([v_ref[0, h0, r0:r0 + c, :], v_ref[0, h0 + 1, r0:r0 + c, :]], axis=1),
                beta=jnp.where(left, beta0, beta1), beta_w=wide(beta0, beta1),
                egc_w=wide(jnp.exp(gcol0), jnp.exp(gcol1)),
                dec_w=wide(jnp.exp(tot0 - gcol0), jnp.exp(tot1 - gcol1)),
                etot_w=jnp.concatenate([jnp.broadcast_to(jnp.exp(tot0), (1, dh)),
                                        jnp.broadcast_to(jnp.exp(tot1), (1, dh))], axis=1),
                decay=jnp.where(incl, jnp.exp(jnp.where(incl, gcol - grow, 0.0)), 0.0)))
    return pairs


def _gdn_local_stages(chains, sub_f, sub_b, c, row, col, lower, upper, nh2, hi, refs):
    chains.extend(_gdn_chunk_chains(sub_f, sub_b, c, row, col, lower, upper, nh2, hi, *refs))
    for w in chains:
        w["gram"] = lax.dot_general(w["kq"], w["k_diag"], _NT, preferred_element_type=F32)
    yield
    ms = [jnp.where(w["strict"], w["beta"] * w["gram"][:c] * w["decay"], 0.0) for w in chains]
    left = col < c
    colh = col & (c - 1)

    def diag2(x):
        return jnp.concatenate([jnp.where(left, x, 0.0), jnp.where(left, 0.0, x)], axis=0).astype(BF16)

    t_offs = None
    k = 1
    while k < c:
        same = (row ^ colh) < 2 * k
        joins = {True: same & ((row & k) != 0) & ((colh & k) == 0),
                 False: same & ((colh & k) != 0) & ((row & k) == 0)}
        parts = [jnp.where(joins[w["lower"]], m, 0.0) for m, w in zip(ms, chains)]
        if t_offs is None:
            t_offs = [-a for a in parts]
        else:
            zs = [a + jnp.dot(t.astype(BF16), diag2(a), preferred_element_type=F32)
                  for a, t in zip(parts, t_offs)]
            yield
            t_offs = [t - z - jnp.dot(z.astype(BF16), diag2(t), preferred_element_type=F32)
                      for t, z in zip(t_offs, zs)]
            yield
        k *= 2
    eye = jnp.where(row == colh, 1.0, 0.0)
    for w, t in zip(chains, t_offs):
        w["tmat"] = (t + eye).astype(BF16)


def _gdn_state_stages(chains, c, s_ref):
    dh = GDN_HEAD_DIM

    def diag2(x):
        z = jnp.zeros((x.shape[0], dh), x.dtype)
        return jnp.concatenate([jnp.concatenate([x[:, :dh], z], axis=1),
                                jnp.concatenate([z, x[:, dh:]], axis=1)], axis=0)

    for w in chains:
        w["state"] = s_ref[w["idx"]]
        w["ks"] = jnp.dot(w["kq"], diag2(w["state"].astype(BF16)), preferred_element_type=F32)
    yield
    for w in chains:
        resid = (w["v"].astype(F32) - w["ks"][:c] * w["egc_w"]) * w["beta_w"]
        w["v_new"] = jnp.dot(w["tmat"], diag2(resid.astype(BF16)), preferred_element_type=F32)
        w["qs"] = w["ks"][c:] * w["egc_w"]
    yield
    for w in chains:
        attn = (w["gram"][c:] * w["decay"]).astype(BF16)
        o = w["qs"] + jnp.dot(attn, diag2(w["v_new"].astype(BF16)), preferred_element_type=F32)
        for m in range(2):
            w["o_ref"][0, w["h0"] + m, w["r0"]:w["r0"] + c, :] = o[:, m * dh:(m + 1) * dh].astype(w["o_ref"].dtype)
    yield
    for w in chains:
        v_dec = diag2((w["v_new"] * w["dec_w"]).astype(BF16))
        s_ref[w["idx"]] = (w["state"] * w["etot_w"]
                           + lax.dot_general(w["k_rows"], v_dec, _TN, preferred_element_type=F32))
    yield


def _gdn_scan(q, k, v, gb, gbt):
    b, nh, t, dh = q.shape
    nt = t // TOK_TILE
    nh2 = 2 * GDN_HEADS

    def bwd(s):
        return jnp.where(s == 0, 0, nt - s)

    head_f = pl.BlockSpec((1, nh, TOK_TILE, dh), lambda i, s: (i, 0, s, 0))
    head_b = pl.BlockSpec((1, nh, TOK_TILE, dh), lambda i, s: (i, 0, bwd(s), 0))
    out_shape = jax.ShapeDtypeStruct((b, nh, t, dh), BF16)
    return pl.pallas_call(
        _gdn_scan_body,
        grid=(b, nt),
        in_specs=[head_f, head_f, head_f, head_b, head_b, head_b,
                  pl.BlockSpec((1, TOK_TILE, 2 * nh2), lambda i, s: (i, s, 0)),
                  pl.BlockSpec((1, TOK_TILE, 2 * nh2), lambda i, s: (i, bwd(s), 0)),
                  pl.BlockSpec((1, 2 * nh2, TOK_TILE), lambda i, s: (i, 0, s)),
                  pl.BlockSpec((1, 2 * nh2, TOK_TILE), lambda i, s: (i, 0, bwd(s)))],
        out_specs=[head_f, head_b],
        out_shape=[out_shape, out_shape],
        scratch_shapes=[pltpu.VMEM((nh2 // 2, dh, 2 * dh), F32)],
        compiler_params=_params(("arbitrary", "arbitrary")),
        name="gdn_chunk_scan",
    )(q, k, v, q, k, v, gb, gb, gbt, gbt)


def _rope_tables(n_lat):
    rows = n_lat // GRID_W
    row_ids = jnp.repeat(jnp.arange(rows, dtype=F32), GRID_W)[:n_lat]
    col_ids = jnp.tile(jnp.arange(GRID_W, dtype=F32), rows)[:n_lat]
    axis_dim = HEAD_DIM // 2
    inv_freq = ROPE_THETA ** (-jnp.arange(0, axis_dim, 2, dtype=F32) / axis_dim)
    ang_r = row_ids[:, None] * inv_freq
    ang_c = col_ids[:, None] * inv_freq
    ang = jnp.concatenate([ang_r, ang_r, ang_c, ang_c], axis=-1)
    cos = jnp.concatenate([jnp.ones((CTX_LEN, HEAD_DIM), F32), jnp.cos(ang)], axis=0)
    sin = jnp.concatenate([jnp.zeros((CTX_LEN, HEAD_DIM), F32), jnp.sin(ang)], axis=0)
    sign = jnp.tile(jnp.repeat(jnp.array([-1.0, 1.0], F32), HEAD_DIM // 4), 2)
    return cos.T, (sin * sign).T


def kernel(x, c, ctx, c_ctx, ada_w, ada_b, norm_g, attn_w_in, attn_w_out, diff_lambda, diff_subln_g,
           gqa_qk_g, gdn_w_in, gdn_conv_w, gdn_a_log, gdn_dt_bias, gdn_norm_g, gdn_w_out,
           ffn_w_gate_up, ffn_w_down):
    b, n_lat, d = x.shape
    depth = ada_w.shape[0]
    assert d == D_MODEL and ctx.shape[1] == CTX_LEN and n_lat % TOK_TILE == 0
    rows = -(-(b + 1) // V7X_SUBLANES) * V7X_SUBLANES
    cc = jnp.concatenate([c, c_ctx[None], jnp.zeros((rows - b - 1, d), F32)], axis=0)
    mods = _modulation(cc, ada_w, ada_b)
    cos_t, sin_t = _rope_tables(n_lat)
    xa = None

    for l in range(depth):
        i = l // 2
        last = l == depth - 1
        ml = mods[l]
        modl = jnp.stack([jnp.broadcast_to(ml[b].reshape(1, 6, d), (b, 6, d)),
                          ml[:b].reshape(b, 6, d)], axis=1)
        g = norm_g[l]
        wgu = ffn_w_gate_up[l].astype(BF16)
        wd = ffn_w_down[l].astype(BF16)
        if l % 2 == 0:
            lam_init = 0.8 - 0.6 * math.exp(-0.3 * l)
            w = attn_w_in[i]
            o_dv = 2 * DIFF_QK_W
            o_gq = o_dv + DIFF_V_W
            o_gk = o_gq + GQA_Q_W
            o_gv = o_gk + GQA_KV_W
            w_t = jnp.concatenate([w[:, :DIFF_QK_W], w[:, o_gq:o_gk],
                                   w[:, DIFF_QK_W:o_dv], w[:, o_gk:o_gv],
                                   w[:, o_dv:o_gq], w[:, o_gv:]],
                                  axis=1).T.astype(BF16)
            qkg = gqa_qk_g[i].reshape(2, HEAD_DIM, 1)
            if l == 0:
                qt, k, vt, xa = _attn_proj(x, modl, g[0:1], w_t, cos_t, sin_t, qkg, ctx=ctx)
            else:
                qt, k, vt = _attn_proj(xa, modl, g[0:1], w_t, cos_t, sin_t, qkg)
            o = _attention(qt, k, vt, diff_lambda[i], diff_subln_g[i].reshape(DIFF_V_DIM, 1), lam_init)
            xa = _post_ffn(xa, o, modl, g, attn_w_out[i].astype(BF16), wgu, wd, gdn=False, latent_only=last)
        else:
            w = gdn_w_in[i]
            wab = w[:, 4 * GDN_W:]
            q, k, v, sz, gb, gbt = _gdn_proj(
                xa, modl, g[0:1], w[:, :3 * GDN_W].astype(BF16), w[:, 3 * GDN_W:4 * GDN_W].astype(BF16),
                wab.astype(BF16), wab.T.astype(BF16), gdn_conv_w[i], gdn_a_log[i], gdn_dt_bias[i])
            of, ob = _gdn_scan(q, k, v, gb, gbt)
            xa = _post_ffn(xa, (of, ob, sz, gdn_norm_g[i].reshape(1, GDN_HEAD_DIM)), modl, g,
                           gdn_w_out[i].astype(BF16), wgu, wd, gdn=True, latent_only=last)
    return xa
```

```python
import functools
import math

import jax
import jax.numpy as jnp
from jax import lax
from jax.experimental import pallas as pl
from jax.experimental.pallas import tpu as pltpu

F32 = jnp.float32
BF16 = jnp.bfloat16

D_MODEL = 1024
CTX_LEN = 256
GRID_W = 64
RMS_EPS = 1e-6
L2_EPS = 1e-6
ROPE_THETA = 10000.0

HEAD_DIM = 64
ATTN_SCALE = HEAD_DIM ** -0.5
DIFF_HEADS = 4
DIFF_V_DIM = 2 * HEAD_DIM
GQA_Q_HEADS = 8
GQA_KV_HEADS = 2
GQA_REP = GQA_Q_HEADS // GQA_KV_HEADS
DIFF_QK_W = DIFF_HEADS * 2 * HEAD_DIM
DIFF_V_W = DIFF_HEADS * DIFF_V_DIM
GQA_Q_W = GQA_Q_HEADS * HEAD_DIM
GQA_KV_W = GQA_KV_HEADS * HEAD_DIM
Q_W = DIFF_QK_W + GQA_Q_W
K_W = DIFF_QK_W + GQA_KV_W
V_W = DIFF_V_W + GQA_KV_W
SUM_ROWS = 16
VT_ROWS = V_W + (DIFF_HEADS + GQA_KV_HEADS) * SUM_ROWS
LOG2_E = math.log2(math.e)
N_MAPS = 2 * DIFF_HEADS + GQA_Q_HEADS
QK_AHEAD = 6

GDN_HEADS = 8
GDN_HEAD_DIM = 128
GDN_W = GDN_HEADS * GDN_HEAD_DIM
GDN_CONV_K = 4
GDN_CONV_LEFT = 2
GDN_CHUNK = 64
SCAN_LOCAL_STAGES = 1 + 2 * (GDN_CHUNK.bit_length() - 2)
SCAN_STATE_STAGES = 4
LOCAL_STAGES_PER_STATE_STAGE = -(-SCAN_LOCAL_STAGES // SCAN_STATE_STAGES)
SCAN_LOCAL_AHEAD = 1
V7X_SUBLANES = 8
CONV_HALO = V7X_SUBLANES

FFN_HIDDEN = 2816
FFN_CHUNK = 256
FFN_GROUP = 2
PROJ_GROUP = 2

TOK_TILE = CTX_LEN
KV_CHUNK = TOK_TILE
V7X_VMEM_BYTES = 64 * 1024 * 1024
VMEM_LIMIT = V7X_VMEM_BYTES * 7 // 8

_NT = (((1,), (1,)), ((), ()))
_TN = (((0,), (0,)), ((), ()))


def _rms(x, g):
    return x * lax.rsqrt(jnp.mean(x * x, axis=-1, keepdims=True) + RMS_EPS) * g


def _silu(x):
    return x * jax.nn.sigmoid(x)


def _params(sem):
    return pltpu.CompilerParams(dimension_semantics=sem, vmem_limit_bytes=VMEM_LIMIT)


def _const_spec(shape):
    n = len(shape)
    return pl.BlockSpec(shape, lambda *_: (0,) * n)


def _mod_body(c_ref, w_ref, b_ref, o_ref):
    sc = _silu(c_ref[...])
    o_ref[0] = jnp.dot(sc, w_ref[0], precision=lax.Precision.HIGHEST,
                       preferred_element_type=F32) + b_ref[0]


def _modulation(cc, ada_w, ada_b):
    depth, d, w6 = ada_w.shape
    rows = cc.shape[0]
    nblk = w6 // d
    return pl.pallas_call(
        _mod_body,
        grid=(depth, nblk),
        in_specs=[pl.BlockSpec((rows, d), lambda l, j: (0, 0)),
                  pl.BlockSpec((1, d, d), lambda l, j: (l, 0, j)),
                  pl.BlockSpec((1, 1, d), lambda l, j: (l, 0, j))],
        out_specs=pl.BlockSpec((1, rows, d), lambda l, j: (l, 0, j)),
        out_shape=jax.ShapeDtypeStruct((depth, rows, w6), F32),
        compiler_params=_params(("parallel", "parallel")),
        name="adaln_mod",
    )(cc, ada_w, ada_b.reshape(depth, 1, w6))


def _attn_proj_body(group, nt, assemble, *refs):
    per = 5 if assemble else 4
    tiles = [refs[per * m:per * (m + 1)] for m in range(group)]
    g_ref, w_ref, qkg_ref = refs[per * group:per * group + 3]
    outs = refs[per * group + 3:]
    hs = []
    for m, tile in enumerate(tiles):
        if assemble:
            j = (pl.program_id(0) * group + m) % nt
            x = jnp.where(j == 0, tile[0][0], tile[1][0])
            outs[3][m] = x
        else:
            x = tile[0][0]
        mod = tile[-3][0, 0]
        hs.append((_rms(x, g_ref[...]) * (1.0 + mod[1:2]) + mod[0:1]).astype(BF16))
    for m, (tile, h) in enumerate(zip(tiles, hs)):
        _attn_proj_heads(m, h, w_ref, tile[-2], tile[-1], qkg_ref, *outs[:3])


def _attn_proj_heads(m, h, w_ref, cos_ref, sin_ref, qkg_ref, qt_out, k_out, vt_out):
    def project(r0, rows):
        return lax.dot_general(w_ref[r0:r0 + rows, :], h, _NT, preferred_element_type=F32)

    pk = project(Q_W, K_W)
    pq = project(0, Q_W)
    pv = project(Q_W + K_W, V_W)
    cos = cos_ref[...]
    sin = sin_ref[...]

    def rope(xh):
        swapped = jnp.concatenate([xh[16:32], xh[0:16], xh[48:64], xh[32:48]], axis=0)
        return xh * cos + swapped * sin

    def norm(xh, g):
        r = lax.rsqrt(jnp.mean(xh * xh, axis=0, keepdims=True) + RMS_EPS)
        return xh * r * g

    gq_g = qkg_ref[0]
    gk_g = qkg_ref[1]
    k_heads = []
    for j in range(K_W // HEAD_DIM):
        r0 = j * HEAD_DIM
        xh = pk[r0:r0 + HEAD_DIM]
        if r0 >= DIFF_QK_W:
            xh = norm(xh, gk_g)
        k_heads.append(rope(xh))
    k_out[m] = jnp.concatenate(k_heads, axis=0).T.astype(BF16)
    for j in range(Q_W // HEAD_DIM):
        r0 = j * HEAD_DIM
        xh = pq[r0:r0 + HEAD_DIM]
        if r0 >= DIFF_QK_W:
            xh = norm(xh, gq_g)
        qt_out[m, r0:r0 + HEAD_DIM, :] = (rope(xh) * (ATTN_SCALE * LOG2_E)).astype(BF16)
    ones = jnp.ones((SUM_ROWS, pv.shape[1]), F32)
    v_blocks = []
    for r0, width in ([(hd * DIFF_V_DIM, DIFF_V_DIM) for hd in range(DIFF_HEADS)]
                      + [(DIFF_V_W + grp * HEAD_DIM, HEAD_DIM) for grp in range(GQA_KV_HEADS)]):
        v_blocks += [pv[r0:r0 + width], ones]
    vt_out[m] = jnp.concatenate(v_blocks, axis=0).astype(BF16)


def _attn_proj(xa, modl, g0, w_t, cos_t, sin_t, qkg, ctx=None):
    assemble = ctx is not None
    b, t, d = xa.shape
    if assemble:
        t += CTX_LEN
    nt = t // TOK_TILE
    group = PROJ_GROUP
    assert (b * nt) % group == 0
    steps = b * nt // group

    def at(m, fn):
        def index_map(p):
            n = p * group + m
            return fn(n // nt, n % nt)
        return index_map

    in_specs, args = [], []
    for m in range(group):
        if assemble:
            in_specs += [pl.BlockSpec((1, TOK_TILE, d), at(m, lambda i, j: (i, 0, 0))),
                         pl.BlockSpec((1, TOK_TILE, d), at(m, lambda i, j: (i, jnp.maximum(j - 1, 0), 0)))]
            args += [ctx, xa]
        else:
            in_specs.append(pl.BlockSpec((1, TOK_TILE, d), at(m, lambda i, j: (i, j, 0))))
            args.append(xa)
        in_specs += [pl.BlockSpec((1, 1, 6, d), at(m, lambda i, j: (i, jnp.minimum(j, 1), 0, 0))),
                     pl.BlockSpec((HEAD_DIM, TOK_TILE), at(m, lambda i, j: (0, j))),
                     pl.BlockSpec((HEAD_DIM, TOK_TILE), at(m, lambda i, j: (0, j)))]
        args += [modl, cos_t, sin_t]
    in_specs += [_const_spec((1, d)), _const_spec(w_t.shape), _const_spec((2, HEAD_DIM, 1))]
    args += [g0, w_t, qkg]
    out_specs = [pl.BlockSpec((group, Q_W, TOK_TILE), lambda p: (p, 0, 0)),
                 pl.BlockSpec((group, TOK_TILE, K_W), lambda p: (p, 0, 0)),
                 pl.BlockSpec((group, VT_ROWS, TOK_TILE), lambda p: (p, 0, 0))]
    out_shape = [jax.ShapeDtypeStruct((b * nt, Q_W, TOK_TILE), BF16),
                 jax.ShapeDtypeStruct((b * nt, TOK_TILE, K_W), BF16),
                 jax.ShapeDtypeStruct((b * nt, VT_ROWS, TOK_TILE), BF16)]
    if assemble:
        out_specs.append(pl.BlockSpec((group, TOK_TILE, d), lambda p: (p, 0, 0)))
        out_shape.append(jax.ShapeDtypeStruct((b * nt, TOK_TILE, d), F32))
    outs = pl.pallas_call(
        functools.partial(_attn_proj_body, group, nt, assemble),
        grid=(steps,),
        in_specs=in_specs,
        out_specs=out_specs,
        out_shape=out_shape,
        compiler_params=_params(("parallel",)),
        name="attn_in_proj",
    )(*args)
    res = (outs[0].reshape(b, nt, Q_W, TOK_TILE), outs[1].reshape(b, t, K_W),
           outs[2].reshape(b, nt, VT_ROWS, TOK_TILE))
    return res + (outs[3].reshape(b, t, d),) if assemble else res


def _attn_body(lam_init, qt_ref, k_ref, vt_ref, lam_ref, subln_ref, o_ref, qpad_ref, m_ref, acc_ref):
    qi = pl.program_id(1)
    slab = 2 * HEAD_DIM
    diff_blk = DIFF_V_DIM + SUM_ROWS
    gqa_blk = HEAD_DIM + SUM_ROWS
    maps = []
    for hd in range(DIFF_HEADS):
        for mm in range(2):
            maps.append((2 * hd + mm, hd, hd * diff_blk, diff_blk))
    for hq in range(GQA_Q_HEADS):
        grp = hq // GQA_REP
        maps.append((2 * DIFF_HEADS + hq, DIFF_HEADS, DIFF_HEADS * diff_blk + grp * gqa_blk, gqa_blk))

    zeros = jnp.zeros((HEAD_DIM, qt_ref.shape[3]), BF16)
    for i, (qh, _, _, _) in enumerate(maps):
        half = (qh % 2) if qh < 2 * DIFF_HEADS else (qh - 2 * DIFF_HEADS) // GQA_REP
        qh_t = qt_ref[0, 0, qh * HEAD_DIM:(qh + 1) * HEAD_DIM, :]
        qpad_ref[i] = jnp.concatenate([qh_t, zeros] if half == 0 else [zeros, qh_t], axis=0)
    def scores(c, i):
        ks = maps[i][1]
        return jnp.dot(k_ref[0, c * KV_CHUNK:(c + 1) * KV_CHUNK, ks * slab:(ks + 1) * slab], qpad_ref[i],
                       preferred_element_type=F32)

    def run(n_chunks):
        items = [(c, i) for c in range(n_chunks) for i in range(N_MAPS)]
        pending = [scores(*items[n]) for n in range(QK_AHEAD)]
        for n, (c, i) in enumerate(items):
            s = pending.pop(0)
            if n + QK_AHEAD < len(items):
                pending.append(scores(*items[n + QK_AHEAD]))
            _, _, v_row0, v_w = maps[i]
            m_new = jnp.max(s, axis=0, keepdims=True)
            if c > 0:
                m_old = m_ref[i]
                m_new = jnp.maximum(m_old, m_new)
                alpha = jnp.exp2(m_old - m_new)
            m_ref[i] = m_new
            p = jnp.exp2(s - m_new)
            pv = jnp.dot(vt_ref[0, c, v_row0:v_row0 + v_w, :], p.astype(BF16), preferred_element_type=F32)
            acc_ref[i, 0:v_w, :] = pv if c == 0 else alpha * acc_ref[i, 0:v_w, :] + pv

    @pl.when(qi == 0)
    def _():
        run(CTX_LEN // KV_CHUNK)

    @pl.when(qi > 0)
    def _():
        run(vt_ref.shape[1])

    lv = lam_ref[...]
    lam = (jnp.exp(jnp.sum(lv[0:1] * lv[1:2], axis=-1, keepdims=True))
           - jnp.exp(jnp.sum(lv[2:3] * lv[3:4], axis=-1, keepdims=True)) + lam_init)
    def normalized(i, width):
        return acc_ref[i, 0:width, :] / acc_ref[i, width:width + 1, :]

    for hd in range(DIFF_HEADS):
        od = normalized(2 * hd, DIFF_V_DIM) - lam * normalized(2 * hd + 1, DIFF_V_DIM)
        od = od * lax.rsqrt(jnp.mean(od * od, axis=0, keepdims=True) + RMS_EPS) * subln_ref[...]
        o_ref[0, :, hd * DIFF_V_DIM:(hd + 1) * DIFF_V_DIM] = (od * (1.0 - lam_init)).T.astype(o_ref.dtype)
    for pair in range(GQA_Q_HEADS // 2):
        i0 = 2 * DIFF_HEADS + 2 * pair
        og = jnp.concatenate([normalized(i0, HEAD_DIM), normalized(i0 + 1, HEAD_DIM)], axis=0)
        c0 = DIFF_V_W + pair * slab
        o_ref[0, :, c0:c0 + slab] = og.T.astype(o_ref.dtype)


def _attention(qt, k, vt, lam_vec, subln_g, lam_init):
    b, t, _ = k.shape
    nq = t // TOK_TILE
    return pl.pallas_call(
        functools.partial(_attn_body, lam_init),
        grid=(b, nq),
        in_specs=[pl.BlockSpec((1, 1, Q_W, TOK_TILE), lambda i, j: (i, j, 0, 0)),
                  pl.BlockSpec((1, t, K_W), lambda i, j: (i, 0, 0)),
                  pl.BlockSpec((1,) + vt.shape[1:], lambda i, j: (i, 0, 0, 0)),
                  _const_spec((4, HEAD_DIM)),
                  _const_spec((DIFF_V_DIM, 1))],
        out_specs=pl.BlockSpec((1, TOK_TILE, D_MODEL), lambda i, j: (i, j, 0)),
        out_shape=jax.ShapeDtypeStruct((b, t, D_MODEL), BF16),
        scratch_shapes=[pltpu.VMEM((N_MAPS, 2 * HEAD_DIM, TOK_TILE), BF16),
                        pltpu.VMEM((N_MAPS, 1, TOK_TILE), F32),
                        pltpu.VMEM((N_MAPS, DIFF_V_DIM + SUM_ROWS, TOK_TILE), F32)],
        compiler_params=_params(("parallel", "parallel")),
        name="diff_gqa_attention",
    )(qt, k, vt, lam_vec, subln_g)


def _post_body(gdn, group, *refs):
    per = 5 if gdn else 3
    tiles = [refs[per * m:per * (m + 1)] for m in range(group)]
    rest = refs[per * group:]
    if gdn:
        ng_ref, rest = rest[0], rest[1:]
    g_ref, wo_ref, wgu_ref, wd_ref, xo_ref, a_ref = rest
    g = g_ref[...]

    def mixer_out(tile):
        if not gdn:
            return tile[1][0]
        _, of_ref, ob_ref, sz_ref, _ = tile
        parts = []
        for hd in range(GDN_HEADS):
            o = of_ref[0, hd].astype(F32) + ob_ref[0, hd].astype(F32)
            parts.append(_rms(o, ng_ref[...]))
        return (jnp.concatenate(parts, axis=-1) * sz_ref[0].astype(F32)).astype(BF16)

    ys = [jnp.dot(mixer_out(tile), wo_ref[...], preferred_element_type=F32) for tile in tiles]
    for m, (tile, y) in enumerate(zip(tiles, ys)):
        mod = tile[-1][0, 0]
        x = tile[0][0] + mod[2:3] * _rms(y, g[1:2])
        h = (_rms(x, g[2:3]) * (1.0 + mod[4:5]) + mod[3:4]).astype(BF16)
        for c in range(FFN_HIDDEN // FFN_CHUNK):
            c0 = c * FFN_CHUNK
            gate = jnp.dot(h, wgu_ref[:, c0:c0 + FFN_CHUNK], preferred_element_type=F32)
            up = jnp.dot(h, wgu_ref[:, FFN_HIDDEN + c0:FFN_HIDDEN + c0 + FFN_CHUNK],
                         preferred_element_type=F32)
            a_ref[m, :, c0:c0 + FFN_CHUNK] = (_silu(gate) * up).astype(BF16)
        ff = jnp.dot(a_ref[m], wd_ref[...], preferred_element_type=F32)
        xo_ref[0, m * TOK_TILE:(m + 1) * TOK_TILE, :] = x + mod[5:6] * _rms(ff, g[3:4])


def _post_ffn(xa, mixer_out, modl, g, wo, wgu, wd, gdn, latent_only):
    b, t, d = xa.shape
    skip = CTX_LEN // TOK_TILE if latent_only else 0
    nt = t // TOK_TILE - skip
    group = FFN_GROUP
    assert (b * nt) % group == 0

    def at(m, fn):
        def index_map(p):
            n = p * group + m
            return fn(n // nt, n % nt + skip)
        return index_map

    in_specs, args = [], []
    for m in range(group):
        tile = pl.BlockSpec((1, TOK_TILE, d), at(m, lambda i, j: (i, j, 0)))
        if gdn:
            of, ob, sz, ng = mixer_out
            head_tile = pl.BlockSpec((1, GDN_HEADS, TOK_TILE, GDN_HEAD_DIM), at(m, lambda i, j: (i, 0, j, 0)))
            in_specs += [tile, head_tile, head_tile, tile]
            args += [xa, of, ob, sz]
        else:
            in_specs += [tile, tile]
            args += [xa, mixer_out]
        in_specs.append(pl.BlockSpec((1, 1, 6, d), at(m, lambda i, j: (i, jnp.minimum(j, 1), 0, 0))))
        args.append(modl)
    if gdn:
        in_specs.append(_const_spec((1, GDN_HEAD_DIM)))
        args.append(ng)
    in_specs += [_const_spec((4, d)), _const_spec(wo.shape), _const_spec(wgu.shape), _const_spec(wd.shape)]
    args += [g, wo, wgu, wd]
    rows = group * TOK_TILE
    out = pl.pallas_call(
        functools.partial(_post_body, gdn, group),
        grid=(b * nt // group,),
        in_specs=in_specs,
        out_specs=pl.BlockSpec((1, rows, d), lambda p: (p, 0, 0)),
        out_shape=jax.ShapeDtypeStruct((b * nt // group, rows, d), F32),
        scratch_shapes=[pltpu.VMEM((group, TOK_TILE, FFN_HIDDEN), BF16)],
        compiler_params=_params(("parallel",)),
        name="out_proj_ffn_gdn" if gdn else "out_proj_ffn_attn",
    )(*args)
    return out.reshape(b, nt * TOK_TILE, d)


def _gdn_proj_body(xp_ref, x_ref, xn_ref, mod_ref, g_ref, wqkv_ref, wz_ref, wab_ref, wabt_ref,
                   conv_ref, alog_ref, dtb_ref, alogt_ref, dtbt_ref,
                   q_out, k_out, v_out, sz_out, gb_out, gbt_out, xs_ref, ys_ref):
    j = pl.program_id(1)
    nt = pl.num_programs(1)
    mod = mod_ref[0, 0]
    g = g_ref[...]
    lanes = GDN_HEAD_DIM
    rows = TOK_TILE + 2 * CONV_HALO
    pitch = rows // 8

    def prep(xv):
        return _rms(xv, g) * (1.0 + mod[1:2]) + mod[0:1]

    prev_ok = jnp.where(j >= 2, 1.0, 0.0)
    next_ok = jnp.where(jnp.logical_and(j >= 1, j < nt - 1), 1.0, 0.0)
    h_f32 = prep(x_ref[0])
    h_main = h_f32.astype(BF16)
    z = jnp.dot(h_main, wz_ref[...], preferred_element_type=F32)
    h_cat = jnp.concatenate([prep(xp_ref[0]) * prev_ok, h_f32, prep(xn_ref[0]) * next_ok], axis=0)
    n_slab = h_cat.shape[1] // lanes
    for s in range(n_slab):
        xs_ref[s] = h_cat[:, s * lanes:(s + 1) * lanes]
    h_perm = jnp.concatenate(
        [jnp.concatenate([xs_ref[s, pl.ds(a, 8, stride=pitch), :] for s in range(n_slab)], axis=1)
         for a in range(pitch)], axis=0).astype(BF16)
    sections = [jnp.dot(h_perm, wqkv_ref[:, s * GDN_W:(s + 1) * GDN_W], preferred_element_type=F32)
                for s in range(3)]
    for sec, (p, out) in enumerate(zip(sections, (q_out, k_out, v_out))):
        cw = conv_ref[:, sec * GDN_W:(sec + 1) * GDN_W]

        def taps(groups):
            acc = groups[0] * cw[0:1]
            for tap in range(1, GDN_CONV_K):
                acc = acc + groups[tap] * cw[tap:tap + 1]
            return acc

        def grp(a):
            return p[8 * a:8 * (a + 1)]

        below = [pltpu.roll(grp(pitch - 2), 1, 0), pltpu.roll(grp(pitch - 1), 1, 0)]
        above = pltpu.roll(grp(0), 7, 0)
        mid = taps([p[8 * t:8 * (t + pitch - 3)] for t in range(GDN_CONV_K)])
        conv = jnp.concatenate([taps([below[0], below[1], grp(0), grp(1)]),
                                taps([below[1], grp(0), grp(1), grp(2)]),
                                mid,
                                taps([grp(pitch - 3), grp(pitch - 2), grp(pitch - 1), above])], axis=0)
        act = _silu(conv)
        for hd in range(GDN_HEADS):
            val = act[:, hd * lanes:(hd + 1) * lanes]
            if sec < 2:
                val = val * lax.rsqrt(jnp.sum(val * val, axis=-1, keepdims=True) + L2_EPS)
            if sec == 0:
                val = val * (GDN_HEAD_DIM ** -0.5)
            slab = sec * GDN_HEADS + hd
            for a in range(pitch):
                ys_ref[slab, pl.ds(a, 8, stride=pitch), :] = val[8 * a:8 * (a + 1)]
            out[0, hd] = ys_ref[slab, CONV_HALO:CONV_HALO + TOK_TILE, :].astype(BF16)
    sz_out[0] = _silu(z).astype(BF16)

    nh2 = 2 * GDN_HEADS
    ab = jnp.dot(h_main, wab_ref[...], preferred_element_type=F32)
    gdec = -jnp.exp(alog_ref[...]) * jax.nn.softplus(ab[:, :nh2] + dtb_ref[...])
    gb_out[0] = jnp.concatenate([gdec, jax.nn.sigmoid(ab[:, nh2:])], axis=-1)
    abt = lax.dot_general(wabt_ref[...], h_main, _NT, preferred_element_type=F32)
    gdec_t = -jnp.exp(alogt_ref[...]) * jax.nn.softplus(abt[:nh2] + dtbt_ref[...])
    gbt_out[0] = jnp.concatenate([gdec_t, jax.nn.sigmoid(abt[nh2:])], axis=0)


def _gdn_proj(xa, modl, g0, wqkv, wz, wab, wab_t, conv_w, a_log, dt_bias):
    b, t, d = xa.shape
    nt = t // TOK_TILE
    per = TOK_TILE // CONV_HALO
    last = t // CONV_HALO - 1
    nh2 = 2 * GDN_HEADS
    head_out = pl.BlockSpec((1, GDN_HEADS, TOK_TILE, GDN_HEAD_DIM), lambda i, j: (i, 0, j, 0))
    head_shape = jax.ShapeDtypeStruct((b, GDN_HEADS, t, GDN_HEAD_DIM), BF16)
    return pl.pallas_call(
        _gdn_proj_body,
        grid=(b, nt),
        in_specs=[pl.BlockSpec((1, CONV_HALO, d), lambda i, j: (i, jnp.maximum(j * per - 1, 0), 0)),
                  pl.BlockSpec((1, TOK_TILE, d), lambda i, j: (i, j, 0)),
                  pl.BlockSpec((1, CONV_HALO, d), lambda i, j: (i, jnp.minimum((j + 1) * per, last), 0)),
                  pl.BlockSpec((1, 1, 6, d), lambda i, j: (i, jnp.minimum(j, 1), 0, 0)),
                  _const_spec((1, d)),
                  _const_spec(wqkv.shape), _const_spec(wz.shape), _const_spec(wab.shape),
                  _const_spec(wab_t.shape), _const_spec(conv_w.shape),
                  _const_spec((1, nh2)), _const_spec((1, nh2)),
                  _const_spec((nh2, 1)), _const_spec((nh2, 1))],
        out_specs=[head_out, head_out, head_out,
                   pl.BlockSpec((1, TOK_TILE, d), lambda i, j: (i, j, 0)),
                   pl.BlockSpec((1, TOK_TILE, 2 * nh2), lambda i, j: (i, j, 0)),
                   pl.BlockSpec((1, 2 * nh2, TOK_TILE), lambda i, j: (i, 0, j))],
        out_shape=[head_shape, head_shape, head_shape,
                   jax.ShapeDtypeStruct((b, t, d), BF16),
                   jax.ShapeDtypeStruct((b, t, 2 * nh2), F32),
                   jax.ShapeDtypeStruct((b, 2 * nh2, t), F32)],
        scratch_shapes=[pltpu.VMEM((d // GDN_HEAD_DIM, TOK_TILE + 2 * CONV_HALO, GDN_HEAD_DIM), F32),
                        pltpu.VMEM((3 * GDN_HEADS, TOK_TILE + 2 * CONV_HALO, GDN_HEAD_DIM), F32)],
        compiler_params=_params(("parallel", "parallel")),
        name="gdn_in_proj",
    )(xa, xa, xa, modl, g0, wqkv, wz, wab, wab_t, conv_w,
      a_log.reshape(1, nh2), dt_bias.reshape(1, nh2), a_log.reshape(nh2, 1), dt_bias.reshape(nh2, 1))


def _gdn_scan_body(qf, kf, vf, qb, kb, vb, gbf, gbb, gtf, gtb, of, ob, s_ref):
    @pl.when(pl.program_id(1) == 0)
    def _():
        s_ref[...] = jnp.zeros_like(s_ref)

    c = GDN_CHUNK
    row = lax.broadcasted_iota(jnp.int32, (c, c), 0)
    col = lax.broadcasted_iota(jnp.int32, (c, c), 1)
    lower = jnp.where(row >= col, 1.0, 0.0)
    upper = jnp.where(row <= col, 1.0, 0.0)
    row = lax.broadcasted_iota(jnp.int32, (c, 2 * c), 0)
    col = lax.broadcasted_iota(jnp.int32, (c, 2 * c), 1)
    nh2 = 2 * GDN_HEADS
    hi = lax.Precision.HIGHEST
    n_sub = qf.shape[2] // c
    refs = (qf, kf, vf, qb, kb, vb, gbf, gbb, gtf, gtb, of, ob)
    chunks = [[] for _ in range(n_sub)]
    local = [_gdn_local_stages(chunks[sub], sub, n_sub - 1 - sub, c, row, col, lower, upper, nh2, hi, refs)
             for sub in range(n_sub)]
    carried = [_gdn_state_stages(chunks[sub], c, s_ref) for sub in range(n_sub)]
    for first in local[:SCAN_LOCAL_AHEAD]:
        for _ in first:
            pass
    for sub in range(n_sub):
        ahead = local[sub + SCAN_LOCAL_AHEAD] if sub + SCAN_LOCAL_AHEAD < n_sub else iter(())
        for _ in carried[sub]:
            for _ in range(LOCAL_STAGES_PER_STATE_STAGE):
                next(ahead, None)
        for _ in ahead:
            pass


def _gdn_chunk_chains(sub_f, sub_b, c, row, col, lower, upper, nh2, hi,
                      qf, kf, vf, qb, kb, vb, gbf, gbb, gtf, gtb, of, ob):
    dh = GDN_HEAD_DIM
    left = col < c
    zeros = jnp.zeros((c, dh), BF16)

    def wide(x0, x1):
        return jnp.concatenate([jnp.broadcast_to(x0, (c, dh)), jnp.broadcast_to(x1, (c, dh))], axis=1)

    pairs = []
    for direction, (q_ref, k_ref, v_ref, gb_ref, gt_ref, o_ref, sub) in enumerate(
            ((qf, kf, vf, gbf, gtf, of, sub_f), (qb, kb, vb, gbb, gtb, ob, sub_b))):
        r0 = sub * c
        gb = gb_ref[0, r0:r0 + c, :]
        gt = gt_ref[0, :, r0:r0 + c]
        tri_c, tri_r = (lower, upper) if direction == 0 else (upper, lower)
        gc = jnp.dot(tri_c, gb[:, :nh2], precision=hi, preferred_element_type=F32)
        gr = jnp.dot(gt[:nh2], tri_r, precision=hi, preferred_element_type=F32)
        incl = (row >= (col & (c - 1))) if direction == 0 else (row <= (col & (c - 1)))
        strict = (row > (col & (c - 1))) if direction == 0 else (row < (col & (c - 1)))
        for pr in range(GDN_HEADS // 2):
            h0 = 2 * pr
            ch = direction * GDN_HEADS + h0
            gcol0, gcol1 = gc[:, ch:ch + 1], gc[:, ch + 1:ch + 2]
            beta0, beta1 = gb[:, nh2 + ch:nh2 + ch + 1], gb[:, nh2 + ch + 1:nh2 + ch + 2]
            gcol = jnp.where(left, gcol0, gcol1)
            grow = jnp.concatenate([gr[ch:ch + 1, :], gr[ch + 1:ch + 2, :]], axis=1)
            edge = c - 1 if direction == 0 else 0
            tot0, tot1 = gcol0[edge:edge + 1], gcol1[edge:edge + 1]
            k0, k1 = k_ref[0, h0, r0:r0 + c, :], k_ref[0, h0 + 1, r0:r0 + c, :]
            q0, q1 = q_ref[0, h0, r0:r0 + c, :], q_ref[0, h0 + 1, r0:r0 + c, :]
            pairs.append(dict(
                idx=direction * (GDN_HEADS // 2) + pr, h0=h0, lower=direction == 0, strict=strict,
                o_ref=o_ref, r0=r0, left=left,
                kq=jnp.concatenate([jnp.concatenate([k0, q0], axis=0),
                                    jnp.concatenate([k1, q1], axis=0)], axis=1),
                k_diag=jnp.concatenate([jnp.concatenate([k0, zeros], axis=1),
                                        jnp.concatenate([zeros, k1], axis=1)], axis=0),
                k_rows=jnp.concatenate([k0, k1], axis=0),
                v=jnp.concatenate([v_ref[0, h0, r0:r0 + c, :], v_ref[0, h0 + 1, r0:r0 + c, :]], axis=1),
                beta=jnp.where(left, beta0, beta1), beta_w=wide(beta0, beta1),
                egc_w=wide(jnp.exp(gcol0), jnp.exp(gcol1)),
                dec_w=wide(jnp.exp(tot0 - gcol0), jnp.exp(tot1 - gcol1)),
                etot_w=jnp.concatenate([jnp.broadcast_to(jnp.exp(tot0), (1, dh)),
                                        jnp.broadcast_to(jnp.exp(tot1), (1, dh))], axis=1),
                decay=jnp.where(incl, jnp.exp(jnp.where(incl, gcol - grow, 0.0)), 0.0)))
    return pairs


def _gdn_local_stages(chains, sub_f, sub_b, c, row, col, lower, upper, nh2, hi, refs):
    chains.extend(_gdn_chunk_chains(sub_f, sub_b, c, row, col, lower, upper, nh2, hi, *refs))
    for w in chains:
        w["gram"] = lax.dot_general(w["kq"], w["k_diag"], _NT, preferred_element_type=F32)
    yield
    ms = [jnp.where(w["strict"], w["beta"] * w["gram"][:c] * w["decay"], 0.0) for w in chains]
    left = col < c
    colh = col & (c - 1)

    def diag2(x):
        return jnp.concatenate([jnp.where(left, x, 0.0), jnp.where(left, 0.0, x)], axis=0).astype(BF16)

    t_offs = None
    k = 1
    while k < c:
        same = (row ^ colh) < 2 * k
        joins = {True: same & ((row & k) != 0) & ((colh & k) == 0),
                 False: same & ((colh & k) != 0) & ((row & k) == 0)}
        parts = [jnp.where(joins[w["lower"]], m, 0.0) for m, w in zip(ms, chains)]
        if t_offs is None:
            t_offs = [-a for a in parts]
        else:
            zs = [a + jnp.dot(t.astype(BF16), diag2(a), preferred_element_type=F32)
                  for a, t in zip(parts, t_offs)]
            yield
            t_offs = [t - z - jnp.dot(z.astype(BF16), diag2(t), preferred_element_type=F32)
                      for t, z in zip(t_offs, zs)]
            yield
        k *= 2
    eye = jnp.where(row == colh, 1.0, 0.0)
    for w, t in zip(chains, t_offs):
        w["tmat"] = (t + eye).astype(BF16)


def _gdn_state_stages(chains, c, s_ref):
    dh = GDN_HEAD_DIM

    def diag2(x):
        z = jnp.zeros((x.shape[0], dh), x.dtype)
        return jnp.concatenate([jnp.concatenate([x[:, :dh], z], axis=1),
                                jnp.concatenate([z, x[:, dh:]], axis=1)], axis=0)

    for w in chains:
        w["state"] = s_ref[w["idx"]]
        w["ks"] = jnp.dot(w["kq"], diag2(w["state"].astype(BF16)), preferred_element_type=F32)
    yield
    for w in chains:
        resid = (w["v"].astype(F32) - w["ks"][:c] * w["egc_w"]) * w["beta_w"]
        w["v_new"] = jnp.dot(w["tmat"], diag2(resid.astype(BF16)), preferred_element_type=F32)
        w["qs"] = w["ks"][c:] * w["egc_w"]
    yield
    for w in chains:
        attn = (w["gram"][c:] * w["decay"]).astype(BF16)
        o = w["qs"] + jnp.dot(attn, diag2(w["v_new"].astype(BF16)), preferred_element_type=F32)
        for m in range(2):
            w["o_ref"][0, w["h0"] + m, w["r0"]:w["r0"] + c, :] = o[:, m * dh:(m + 1) * dh].astype(w["o_ref"].dtype)
    yield
    for w in chains:
        v_dec = diag2((w["v_new"] * w["dec_w"]).astype(BF16))
        s_ref[w["idx"]] = (w["state"] * w["etot_w"]
                           + lax.dot_general(w["k_rows"], v_dec, _TN, preferred_element_type=F32))
    yield


def _gdn_scan(q, k, v, gb, gbt):
    b, nh, t, dh = q.shape
    nt = t // TOK_TILE
    nh2 = 2 * GDN_HEADS

    def bwd(s):
        return jnp.where(s == 0, 0, nt - s)

    head_f = pl.BlockSpec((1, nh, TOK_TILE, dh), lambda i, s: (i, 0, s, 0))
    head_b = pl.BlockSpec((1, nh, TOK_TILE, dh), lambda i, s: (i, 0, bwd(s), 0))
    out_shape = jax.ShapeDtypeStruct((b, nh, t, dh), BF16)
    return pl.pallas_call(
        _gdn_scan_body,
        grid=(b, nt),
        in_specs=[head_f, head_f, head_f, head_b, head_b, head_b,
                  pl.BlockSpec((1, TOK_TILE, 2 * nh2), lambda i, s: (i, s, 0)),
                  pl.BlockSpec((1, TOK_TILE, 2 * nh2), lambda i, s: (i, bwd(s), 0)),
                  pl.BlockSpec((1, 2 * nh2, TOK_TILE), lambda i, s: (i, 0, s)),
                  pl.BlockSpec((1, 2 * nh2, TOK_TILE), lambda i, s: (i, 0, bwd(s)))],
        out_specs=[head_f, head_b],
        out_shape=[out_shape, out_shape],
        scratch_shapes=[pltpu.VMEM((nh2 // 2, dh, 2 * dh), F32)],
        compiler_params=_params(("arbitrary", "arbitrary")),
        name="gdn_chunk_scan",
    )(q, k, v, q, k, v, gb, gb, gbt, gbt)


def _rope_tables(n_lat):
    rows = n_lat // GRID_W
    row_ids = jnp.repeat(jnp.arange(rows, dtype=F32), GRID_W)[:n_lat]
    col_ids = jnp.tile(jnp.arange(GRID_W, dtype=F32), rows)[:n_lat]
    axis_dim = HEAD_DIM // 2
    inv_freq = ROPE_THETA ** (-jnp.arange(0, axis_dim, 2, dtype=F32) / axis_dim)
    ang_r = row_ids[:, None] * inv_freq
    ang_c = col_ids[:, None] * inv_freq
    ang = jnp.concatenate([ang_r, ang_r, ang_c, ang_c], axis=-1)
    cos = jnp.concatenate([jnp.ones((CTX_LEN, HEAD_DIM), F32), jnp.cos(ang)], axis=0)
    sin = jnp.concatenate([jnp.zeros((CTX_LEN, HEAD_DIM), F32), jnp.sin(ang)], axis=0)
    sign = jnp.tile(jnp.repeat(jnp.array([-1.0, 1.0], F32), HEAD_DIM // 4), 2)
    return cos.T, (sin * sign).T


def kernel(x, c, ctx, c_ctx, ada_w, ada_b, norm_g, attn_w_in, attn_w_out, diff_lambda, diff_subln_g,
           gqa_qk_g, gdn_w_in, gdn_conv_w, gdn_a_log, gdn_dt_bias, gdn_norm_g, gdn_w_out,
           ffn_w_gate_up, ffn_w_down):
    b, n_lat, d = x.shape
    depth = ada_w.shape[0]
    assert d == D_MODEL and ctx.shape[1] == CTX_LEN and n_lat % TOK_TILE == 0
    rows = -(-(b + 1) // V7X_SUBLANES) * V7X_SUBLANES
    cc = jnp.concatenate([c, c_ctx[None], jnp.zeros((rows - b - 1, d), F32)], axis=0)
    mods = _modulation(cc, ada_w, ada_b)
    cos_t, sin_t = _rope_tables(n_lat)
    xa = None

    for l in range(depth):
        i = l // 2
        last = l == depth - 1
        ml = mods[l]
        modl = jnp.stack([jnp.broadcast_to(ml[b].reshape(1, 6, d), (b, 6, d)),
                          ml[:b].reshape(b, 6, d)], axis=1)
        g = norm_g[l]
        wgu = ffn_w_gate_up[l].astype(BF16)
        wd = ffn_w_down[l].astype(BF16)
        if l % 2 == 0:
            lam_init = 0.8 - 0.6 * math.exp(-0.3 * l)
            w = attn_w_in[i]
            o_dv = 2 * DIFF_QK_W
            o_gq = o_dv + DIFF_V_W
            o_gk = o_gq + GQA_Q_W
            o_gv = o_gk + GQA_KV_W
            w_t = jnp.concatenate([w[:, :DIFF_QK_W], w[:, o_gq:o_gk],
                                   w[:, DIFF_QK_W:o_dv], w[:, o_gk:o_gv],
                                   w[:, o_dv:o_gq], w[:, o_gv:]],
                                  axis=1).T.astype(BF16)
            qkg = gqa_qk_g[i].reshape(2, HEAD_DIM, 1)
            if l == 0:
                qt, k, vt, xa = _attn_proj(x, modl, g[0:1], w_t, cos_t, sin_t, qkg, ctx=ctx)
            else:
                qt, k, vt = _attn_proj(xa, modl, g[0:1], w_t, cos_t, sin_t, qkg)
            o = _attention(qt, k, vt, diff_lambda[i], diff_subln_g[i].reshape(DIFF_V_DIM, 1), lam_init)
            xa = _post_ffn(xa, o, modl, g, attn_w_out[i].astype(BF16), wgu, wd, gdn=False, latent_only=last)
        else:
            w = gdn_w_in[i]
            wab = w[:, 4 * GDN_W:]
            q, k, v, sz, gb, gbt = _gdn_proj(
                xa, modl, g[0:1], w[:, :3 * GDN_W].astype(BF16), w[:, 3 * GDN_W:4 * GDN_W].astype(BF16),
                wab.astype(BF16), wab.T.astype(BF16), gdn_conv_w[i], gdn_a_log[i], gdn_dt_bias[i])
            of, ob = _gdn_scan(q, k, v, gb, gbt)
            xa = _post_ffn(xa, (of, ob, sz, gdn_norm_g[i].reshape(1, GDN_HEAD_DIM)), modl, g,
                           gdn_w_out[i].astype(BF16), wgu, wd, gdn=True, latent_only=last)
    return xa
```

```python
import functools
import math

import jax
import jax.numpy as jnp
from jax import lax
from jax.experimental import pallas as pl
from jax.experimental.pallas import tpu as pltpu

F32 = jnp.float32
BF16 = jnp.bfloat16

D_MODEL = 1024
CTX_LEN = 256
GRID_W = 64
RMS_EPS = 1e-6
L2_EPS = 1e-6
ROPE_THETA = 10000.0

HEAD_DIM = 64
ATTN_SCALE = HEAD_DIM ** -0.5
DIFF_HEADS = 4
DIFF_V_DIM = 2 * HEAD_DIM
GQA_Q_HEADS = 8
GQA_KV_HEADS = 2
GQA_REP = GQA_Q_HEADS // GQA_KV_HEADS
DIFF_QK_W = DIFF_HEADS * 2 * HEAD_DIM
DIFF_V_W = DIFF_HEADS * DIFF_V_DIM
GQA_Q_W = GQA_Q_HEADS * HEAD_DIM
GQA_KV_W = GQA_KV_HEADS * HEAD_DIM
Q_W = DIFF_QK_W + GQA_Q_W
K_W = DIFF_QK_W + GQA_KV_W
V_W = DIFF_V_W + GQA_KV_W
SUM_ROWS = 16
VT_ROWS = V_W + (DIFF_HEADS + GQA_KV_HEADS) * SUM_ROWS
LOG2_E = math.log2(math.e)
N_MAPS = 2 * DIFF_HEADS + GQA_Q_HEADS
QK_AHEAD = 6

GDN_HEADS = 8
GDN_HEAD_DIM = 128
GDN_W = GDN_HEADS * GDN_HEAD_DIM
GDN_CONV_K = 4
GDN_CONV_LEFT = 2
GDN_CHUNK = 64
SCAN_LOCAL_STAGES = 1 + 2 * (GDN_CHUNK.bit_length() - 2)
SCAN_STATE_STAGES = 4
LOCAL_STAGES_PER_STATE_STAGE = -(-SCAN_LOCAL_STAGES // SCAN_STATE_STAGES)
SCAN_LOCAL_AHEAD = 1
V7X_SUBLANES = 8
CONV_HALO = V7X_SUBLANES

FFN_HIDDEN = 2816
FFN_CHUNK = 256
FFN_GROUP = 2
PROJ_GROUP = 2

TOK_TILE = CTX_LEN
KV_CHUNK = TOK_TILE
V7X_VMEM_BYTES = 64 * 1024 * 1024
VMEM_LIMIT = V7X_VMEM_BYTES * 7 // 8

_NT = (((1,), (1,)), ((), ()))
_TN = (((0,), (0,)), ((), ()))


def _rms(x, g):
    return x * lax.rsqrt(jnp.mean(x * x, axis=-1, keepdims=True) + RMS_EPS) * g


def _silu(x):
    return x * jax.nn.sigmoid(x)


def _params(sem):
    return pltpu.CompilerParams(dimension_semantics=sem, vmem_limit_bytes=VMEM_LIMIT)


def _const_spec(shape):
    n = len(shape)
    return pl.BlockSpec(shape, lambda *_: (0,) * n)


def _mod_body(c_ref, w_ref, b_ref, o_ref):
    sc = _silu(c_ref[...])
    o_ref[0] = jnp.dot(sc, w_ref[0], precision=lax.Precision.HIGHEST,
                       preferred_element_type=F32) + b_ref[0]


def _modulation(cc, ada_w, ada_b):
    depth, d, w6 = ada_w.shape
    rows = cc.shape[0]
    nblk = w6 // d
    return pl.pallas_call(
        _mod_body,
        grid=(depth, nblk),
        in_specs=[pl.BlockSpec((rows, d), lambda l, j: (0, 0)),
                  pl.BlockSpec((1, d, d), lambda l, j: (l, 0, j)),
                  pl.BlockSpec((1, 1, d), lambda l, j: (l, 0, j))],
        out_specs=pl.BlockSpec((1, rows, d), lambda l, j: (l, 0, j)),
        out_shape=jax.ShapeDtypeStruct((depth, rows, w6), F32),
        compiler_params=_params(("parallel", "parallel")),
        name="adaln_mod",
    )(cc, ada_w, ada_b.reshape(depth, 1, w6))


def _attn_proj_body(group, nt, assemble, *refs):
    per = 5 if assemble else 4
    tiles = [refs[per * m:per * (m + 1)] for m in range(group)]
    g_ref, w_ref, qkg_ref = refs[per * group:per * group + 3]
    outs = refs[per * group + 3:]
    hs = []
    for m, tile in enumerate(tiles):
        if assemble:
            j = (pl.program_id(0) * group + m) % nt
            x = jnp.where(j == 0, tile[0][0], tile[1][0])
            outs[3][m] = x
        else:
            x = tile[0][0]
        mod = tile[-3][0, 0]
        hs.append((_rms(x, g_ref[...]) * (1.0 + mod[1:2]) + mod[0:1]).astype(BF16))
    for m, (tile, h) in enumerate(zip(tiles, hs)):
        _attn_proj_heads(m, h, w_ref, tile[-2], tile[-1], qkg_ref, *outs[:3])


def _attn_proj_heads(m, h, w_ref, cos_ref, sin_ref, qkg_ref, qt_out, k_out, vt_out):
    def project(r0, rows):
        return lax.dot_general(w_ref[r0:r0 + rows, :], h, _NT, preferred_element_type=F32)

    pk = project(Q_W, K_W)
    pq = project(0, Q_W)
    pv = project(Q_W + K_W, V_W)
    cos = cos_ref[...]
    sin = sin_ref[...]

    def rope(xh):
        swapped = jnp.concatenate([xh[16:32], xh[0:16], xh[48:64], xh[32:48]], axis=0)
        return xh * cos + swapped * sin

    def norm(xh, g):
        r = lax.rsqrt(jnp.mean(xh * xh, axis=0, keepdims=True) + RMS_EPS)
        return xh * r * g

    gq_g = qkg_ref[0]
    gk_g = qkg_ref[1]
    k_heads = []
    for j in range(K_W // HEAD_DIM):
        r0 = j * HEAD_DIM
        xh = pk[r0:r0 + HEAD_DIM]
        if r0 >= DIFF_QK_W:
            xh = norm(xh, gk_g)
        k_heads.append(rope(xh))
    k_out[m] = jnp.concatenate(k_heads, axis=0).T.astype(BF16)
    for j in range(Q_W // HEAD_DIM):
        r0 = j * HEAD_DIM
        xh = pq[r0:r0 + HEAD_DIM]
        if r0 >= DIFF_QK_W:
            xh = norm(xh, gq_g)
        qt_out[m, r0:r0 + HEAD_DIM, :] = (rope(xh) * (ATTN_SCALE * LOG2_E)).astype(BF16)
    ones = jnp.ones((SUM_ROWS, pv.shape[1]), F32)
    v_blocks = []
    for r0, width in ([(hd * DIFF_V_DIM, DIFF_V_DIM) for hd in range(DIFF_HEADS)]
                      + [(DIFF_V_W + grp * HEAD_DIM, HEAD_DIM) for grp in range(GQA_KV_HEADS)]):
        v_blocks += [pv[r0:r0 + width], ones]
    vt_out[m] = jnp.concatenate(v_blocks, axis=0).astype(BF16)


def _attn_proj(xa, modl, g0, w_t, cos_t, sin_t, qkg, ctx=None):
    assemble = ctx is not None
    b, t, d = xa.shape
    if assemble:
        t += CTX_LEN
    nt = t // TOK_TILE
    group = PROJ_GROUP
    assert (b * nt) % group == 0
    steps = b * nt // group

    def at(m, fn):
        def index_map(p):
            n = p * group + m
            return fn(n // nt, n % nt)
        return index_map

    in_specs, args = [], []
    for m in range(group):
        if assemble:
            in_specs += [pl.BlockSpec((1, TOK_TILE, d), at(m, lambda i, j: (i, 0, 0))),
                         pl.BlockSpec((1, TOK_TILE, d), at(m, lambda i, j: (i, jnp.maximum(j - 1, 0), 0)))]
            args += [ctx, xa]
        else:
            in_specs.append(pl.BlockSpec((1, TOK_TILE, d), at(m, lambda i, j: (i, j, 0))))
            args.append(xa)
        in_specs += [pl.BlockSpec((1, 1, 6, d), at(m, lambda i, j: (i, jnp.minimum(j, 1), 0, 0))),
                     pl.BlockSpec((HEAD_DIM, TOK_TILE), at(m, lambda i, j: (0, j))),
                     pl.BlockSpec((HEAD_DIM, TOK_TILE), at(m, lambda i, j: (0, j)))]
        args += [modl, cos_t, sin_t]
    in_specs += [_const_spec((1, d)), _const_spec(w_t.shape), _const_spec((2, HEAD_DIM, 1))]
    args += [g0, w_t, qkg]
    out_specs = [pl.BlockSpec((group, Q_W, TOK_TILE), lambda p: (p, 0, 0)),
                 pl.BlockSpec((group, TOK_TILE, K_W), lambda p: (p, 0, 0)),
                 pl.BlockSpec((group, VT_ROWS, TOK_TILE), lambda p: (p, 0, 0))]
    out_shape = [jax.ShapeDtypeStruct((b * nt, Q_W, TOK_TILE), BF16),
                 jax.ShapeDtypeStruct((b * nt, TOK_TILE, K_W), BF16),
                 jax.ShapeDtypeStruct((b * nt, VT_ROWS, TOK_TILE), BF16)]
    if assemble:
        out_specs.append(pl.BlockSpec((group, TOK_TILE, d), lambda p: (p, 0, 0)))
        out_shape.append(jax.ShapeDtypeStruct((b * nt, TOK_TILE, d), F32))
    outs = pl.pallas_call(
        functools.partial(_attn_proj_body, group, nt, assemble),
        grid=(steps,),
        in_specs=in_specs,
        out_specs=out_specs,
        out_shape=out_shape,
        compiler_params=_params(("parallel",)),
        name="attn_in_proj",
    )(*args)
    res = (outs[0].reshape(b, nt, Q_W, TOK_TILE), outs[1].reshape(b, t, K_W),
           outs[2].reshape(b, nt, VT_ROWS, TOK_TILE))
    return res + (outs[3].reshape(b, t, d),) if assemble else res


def _attn_body(lam_init, qt_ref, k_ref, vt_ref, lam_ref, subln_ref, o_ref, qpad_ref, m_ref, acc_ref):
    qi = pl.program_id(1)
    slab = 2 * HEAD_DIM
    diff_blk = DIFF_V_DIM + SUM_ROWS
    gqa_blk = HEAD_DIM + SUM_ROWS
    maps = []
    for hd in range(DIFF_HEADS):
        for mm in range(2):
            maps.append((2 * hd + mm, hd, hd * diff_blk, diff_blk))
    for hq in range(GQA_Q_HEADS):
        grp = hq // GQA_REP
        maps.append((2 * DIFF_HEADS + hq, DIFF_HEADS, DIFF_HEADS * diff_blk + grp * gqa_blk, gqa_blk))

    zeros = jnp.zeros((HEAD_DIM, qt_ref.shape[3]), BF16)
    for i, (qh, _, _, _) in enumerate(maps):
        half = (qh % 2) if qh < 2 * DIFF_HEADS else (qh - 2 * DIFF_HEADS) // GQA_REP
        qh_t = qt_ref[0, 0, qh * HEAD_DIM:(qh + 1) * HEAD_DIM, :]
        qpad_ref[i] = jnp.concatenate([qh_t, zeros] if half == 0 else [zeros, qh_t], axis=0)
    def scores(c, i):
        ks = maps[i][1]
        return jnp.dot(k_ref[0, c * KV_CHUNK:(c + 1) * KV_CHUNK, ks * slab:(ks + 1) * slab], qpad_ref[i],
                       preferred_element_type=F32)

    def run(n_chunks):
        items = [(c, i) for c in range(n_chunks) for i in range(N_MAPS)]
        pending = [scores(*items[n]) for n in range(QK_AHEAD)]
        for n, (c, i) in enumerate(items):
            s = pending.pop(0)
            if n + QK_AHEAD < len(items):
                pending.append(scores(*items[n + QK_AHEAD]))
            _, _, v_row0, v_w = maps[i]
            m_new = jnp.max(s, axis=0, keepdims=True)
            if c > 0:
                m_old = m_ref[i]
                m_new = jnp.maximum(m_old, m_new)
                alpha = jnp.exp2(m_old - m_new)
            m_ref[i] = m_new
            p = jnp.exp2(s - m_new)
            pv = jnp.dot(vt_ref[0, c, v_row0:v_row0 + v_w, :], p.astype(BF16), preferred_element_type=F32)
            acc_ref[i, 0:v_w, :] = pv if c == 0 else alpha * acc_ref[i, 0:v_w, :] + pv

    @pl.when(qi == 0)
    def _():
        run(CTX_LEN // KV_CHUNK)

    @pl.when(qi > 0)
    def _():
        run(vt_ref.shape[1])

    lv = lam_ref[...]
    lam = (jnp.exp(jnp.sum(lv[0:1] * lv[1:2], axis=-1, keepdims=True))
           - jnp.exp(jnp.sum(lv[2:3] * lv[3:4], axis=-1, keepdims=True)) + lam_init)
    def normalized(i, width):
        return acc_ref[i, 0:width, :] / acc_ref[i, width:width + 1, :]

    for hd in range(DIFF_HEADS):
        od = normalized(2 * hd, DIFF_V_DIM) - lam * normalized(2 * hd + 1, DIFF_V_DIM)
        od = od * lax.rsqrt(jnp.mean(od * od, axis=0, keepdims=True) + RMS_EPS) * subln_ref[...]
        o_ref[0, 0, hd * DIFF_V_DIM:(hd + 1) * DIFF_V_DIM, :] = (od * (1.0 - lam_init)).astype(o_ref.dtype)
    for hq in range(GQA_Q_HEADS):
        r0 = DIFF_V_W + hq * HEAD_DIM
        o_ref[0, 0, r0:r0 + HEAD_DIM, :] = normalized(2 * DIFF_HEADS + hq, HEAD_DIM).astype(o_ref.dtype)


def _attention(qt, k, vt, lam_vec, subln_g, lam_init):
    b, t, _ = k.shape
    nq = t // TOK_TILE
    return pl.pallas_call(
        functools.partial(_attn_body, lam_init),
        grid=(b, nq),
        in_specs=[pl.BlockSpec((1, 1, Q_W, TOK_TILE), lambda i, j: (i, j, 0, 0)),
                  pl.BlockSpec((1, t, K_W), lambda i, j: (i, 0, 0)),
                  pl.BlockSpec((1,) + vt.shape[1:], lambda i, j: (i, 0, 0, 0)),
                  _const_spec((4, HEAD_DIM)),
                  _const_spec((DIFF_V_DIM, 1))],
        out_specs=pl.BlockSpec((1, 1, D_MODEL, TOK_TILE), lambda i, j: (i, j, 0, 0)),
        out_shape=jax.ShapeDtypeStruct((b, nq, D_MODEL, TOK_TILE), BF16),
        scratch_shapes=[pltpu.VMEM((N_MAPS, 2 * HEAD_DIM, TOK_TILE), BF16),
                        pltpu.VMEM((N_MAPS, 1, TOK_TILE), F32),
                        pltpu.VMEM((N_MAPS, DIFF_V_DIM + SUM_ROWS, TOK_TILE), F32)],
        compiler_params=_params(("parallel", "parallel")),
        name="diff_gqa_attention",
    )(qt, k, vt, lam_vec, subln_g)


def _post_body(gdn, group, *refs):
    per = 5 if gdn else 3
    tiles = [refs[per * m:per * (m + 1)] for m in range(group)]
    rest = refs[per * group:]
    if gdn:
        ng_ref, rest = rest[0], rest[1:]
    g_ref, wo_ref, wgu_ref, wd_ref, xo_ref, a_ref = rest
    g = g_ref[...]

    def mixer_out(tile):
        if not gdn:
            return tile[1][0, 0].astype(F32).T.astype(BF16)
        _, of_ref, ob_ref, sz_ref, _ = tile
        parts = []
        for hd in range(GDN_HEADS):
            o = of_ref[0, hd].astype(F32) + ob_ref[0, hd].astype(F32)
            parts.append(_rms(o, ng_ref[...]))
        return (jnp.concatenate(parts, axis=-1) * sz_ref[0].astype(F32)).astype(BF16)

    ys = [jnp.dot(mixer_out(tile), wo_ref[...], preferred_element_type=F32) for tile in tiles]
    for m, (tile, y) in enumerate(zip(tiles, ys)):
        mod = tile[-1][0, 0]
        x = tile[0][0] + mod[2:3] * _rms(y, g[1:2])
        h = (_rms(x, g[2:3]) * (1.0 + mod[4:5]) + mod[3:4]).astype(BF16)
        for c in range(FFN_HIDDEN // FFN_CHUNK):
            c0 = c * FFN_CHUNK
            gate = jnp.dot(h, wgu_ref[:, c0:c0 + FFN_CHUNK], preferred_element_type=F32)
            up = jnp.dot(h, wgu_ref[:, FFN_HIDDEN + c0:FFN_HIDDEN + c0 + FFN_CHUNK],
                         preferred_element_type=F32)
            a_ref[m, :, c0:c0 + FFN_CHUNK] = (_silu(gate) * up).astype(BF16)
        ff = jnp.dot(a_ref[m], wd_ref[...], preferred_element_type=F32)
        xo_ref[0, m * TOK_TILE:(m + 1) * TOK_TILE, :] = x + mod[5:6] * _rms(ff, g[3:4])


def _post_ffn(xa, mixer_out, modl, g, wo, wgu, wd, gdn, latent_only):
    b, t, d = xa.shape
    skip = CTX_LEN // TOK_TILE if latent_only else 0
    nt = t // TOK_TILE - skip
    group = FFN_GROUP
    assert (b * nt) % group == 0

    def at(m, fn):
        def index_map(p):
            n = p * group + m
            return fn(n // nt, n % nt + skip)
        return index_map

    in_specs, args = [], []
    for m in range(group):
        tile = pl.BlockSpec((1, TOK_TILE, d), at(m, lambda i, j: (i, j, 0)))
        if gdn:
            of, ob, sz, ng = mixer_out
            head_tile = pl.BlockSpec((1, GDN_HEADS, TOK_TILE, GDN_HEAD_DIM), at(m, lambda i, j: (i, 0, j, 0)))
            in_specs += [tile, head_tile, head_tile, tile]
            args += [xa, of, ob, sz]
        else:
            in_specs += [tile, pl.BlockSpec((1, 1, d, TOK_TILE), at(m, lambda i, j: (i, j, 0, 0)))]
            args += [xa, mixer_out]
        in_specs.append(pl.BlockSpec((1, 1, 6, d), at(m, lambda i, j: (i, jnp.minimum(j, 1), 0, 0))))
        args.append(modl)
    if gdn:
        in_specs.append(_const_spec((1, GDN_HEAD_DIM)))
        args.append(ng)
    in_specs += [_const_spec((4, d)), _const_spec(wo.shape), _const_spec(wgu.shape), _const_spec(wd.shape)]
    args += [g, wo, wgu, wd]
    rows = group * TOK_TILE
    out = pl.pallas_call(
        functools.partial(_post_body, gdn, group),
        grid=(b * nt // group,),
        in_specs=in_specs,
        out_specs=pl.BlockSpec((1, rows, d), lambda p: (p, 0, 0)),
        out_shape=jax.ShapeDtypeStruct((b * nt // group, rows, d), F32),
        scratch_shapes=[pltpu.VMEM((group, TOK_TILE, FFN_HIDDEN), BF16)],
        compiler_params=_params(("parallel",)),
        name="out_proj_ffn_gdn" if gdn else "out_proj_ffn_attn",
    )(*args)
    return out.reshape(b, nt * TOK_TILE, d)


def _gdn_proj_body(xp_ref, x_ref, xn_ref, mod_ref, g_ref, wqkv_ref, wz_ref, wab_ref, wabt_ref,
                   conv_ref, alog_ref, dtb_ref, alogt_ref, dtbt_ref,
                   q_out, k_out, v_out, sz_out, gb_out, gbt_out, xs_ref, ys_ref):
    j = pl.program_id(1)
    nt = pl.num_programs(1)
    mod = mod_ref[0, 0]
    g = g_ref[...]
    lanes = GDN_HEAD_DIM
    rows = TOK_TILE + 2 * CONV_HALO
    pitch = rows // 8

    def prep(xv):
        return _rms(xv, g) * (1.0 + mod[1:2]) + mod[0:1]

    prev_ok = jnp.where(j >= 2, 1.0, 0.0)
    next_ok = jnp.where(jnp.logical_and(j >= 1, j < nt - 1), 1.0, 0.0)
    h_f32 = prep(x_ref[0])
    h_main = h_f32.astype(BF16)
    z = jnp.dot(h_main, wz_ref[...], preferred_element_type=F32)
    h_cat = jnp.concatenate([prep(xp_ref[0]) * prev_ok, h_f32, prep(xn_ref[0]) * next_ok], axis=0)
    n_slab = h_cat.shape[1] // lanes
    for s in range(n_slab):
        xs_ref[s] = h_cat[:, s * lanes:(s + 1) * lanes]
    h_perm = jnp.concatenate(
        [jnp.concatenate([xs_ref[s, pl.ds(a, 8, stride=pitch), :] for s in range(n_slab)], axis=1)
         for a in range(pitch)], axis=0).astype(BF16)
    sections = [jnp.dot(h_perm, wqkv_ref[:, s * GDN_W:(s + 1) * GDN_W], preferred_element_type=F32)
                for s in range(3)]
    for sec, (p, out) in enumerate(zip(sections, (q_out, k_out, v_out))):
        cw = conv_ref[:, sec * GDN_W:(sec + 1) * GDN_W]

        def taps(groups):
            acc = groups[0] * cw[0:1]
            for tap in range(1, GDN_CONV_K):
                acc = acc + groups[tap] * cw[tap:tap + 1]
            return acc

        def grp(a):
            return p[8 * a:8 * (a + 1)]

        below = [pltpu.roll(grp(pitch - 2), 1, 0), pltpu.roll(grp(pitch - 1), 1, 0)]
        above = pltpu.roll(grp(0), 7, 0)
        mid = taps([p[8 * t:8 * (t + pitch - 3)] for t in range(GDN_CONV_K)])
        conv = jnp.concatenate([taps([below[0], below[1], grp(0), grp(1)]),
                                taps([below[1], grp(0), grp(1), grp(2)]),
                                mid,
                                taps([grp(pitch - 3), grp(pitch - 2), grp(pitch - 1), above])], axis=0)
        act = _silu(conv)
        for hd in range(GDN_HEADS):
            val = act[:, hd * lanes:(hd + 1) * lanes]
            if sec < 2:
                val = val * lax.rsqrt(jnp.sum(val * val, axis=-1, keepdims=True) + L2_EPS)
            if sec == 0:
                val = val * (GDN_HEAD_DIM ** -0.5)
            slab = sec * GDN_HEADS + hd
            for a in range(pitch):
                ys_ref[slab, pl.ds(a, 8, stride=pitch), :] = val[8 * a:8 * (a + 1)]
            out[0, hd] = ys_ref[slab, CONV_HALO:CONV_HALO + TOK_TILE, :].astype(BF16)
    sz_out[0] = _silu(z).astype(BF16)

    nh2 = 2 * GDN_HEADS
    ab = jnp.dot(h_main, wab_ref[...], preferred_element_type=F32)
    gdec = -jnp.exp(alog_ref[...]) * jax.nn.softplus(ab[:, :nh2] + dtb_ref[...])
    gb_out[0] = jnp.concatenate([gdec, jax.nn.sigmoid(ab[:, nh2:])], axis=-1)
    abt = lax.dot_general(wabt_ref[...], h_main, _NT, preferred_element_type=F32)
    gdec_t = -jnp.exp(alogt_ref[...]) * jax.nn.softplus(abt[:nh2] + dtbt_ref[...])
    gbt_out[0] = jnp.concatenate([gdec_t, jax.nn.sigmoid(abt[nh2:])], axis=0)


def _gdn_proj(xa, modl, g0, wqkv, wz, wab, wab_t, conv_w, a_log, dt_bias):
    b, t, d = xa.shape
    nt = t // TOK_TILE
    per = TOK_TILE // CONV_HALO
    last = t // CONV_HALO - 1
    nh2 = 2 * GDN_HEADS
    head_out = pl.BlockSpec((1, GDN_HEADS, TOK_TILE, GDN_HEAD_DIM), lambda i, j: (i, 0, j, 0))
    head_shape = jax.ShapeDtypeStruct((b, GDN_HEADS, t, GDN_HEAD_DIM), BF16)
    return pl.pallas_call(
        _gdn_proj_body,
        grid=(b, nt),
        in_specs=[pl.BlockSpec((1, CONV_HALO, d), lambda i, j: (i, jnp.maximum(j * per - 1, 0), 0)),
                  pl.BlockSpec((1, TOK_TILE, d), lambda i, j: (i, j, 0)),
                  pl.BlockSpec((1, CONV_HALO, d), lambda i, j: (i, jnp.minimum((j + 1) * per, last), 0)),
                  pl.BlockSpec((1, 1, 6, d), lambda i, j: (i, jnp.minimum(j, 1), 0, 0)),
                  _const_spec((1, d)),
                  _const_spec(wqkv.shape), _const_spec(wz.shape), _const_spec(wab.shape),
                  _const_spec(wab_t.shape), _const_spec(conv_w.shape),
                  _const_spec((1, nh2)), _const_spec((1, nh2)),
                  _const_spec((nh2, 1)), _const_spec((nh2, 1))],
        out_specs=[head_out, head_out, head_out,
                   pl.BlockSpec((1, TOK_TILE, d), lambda i, j: (i, j, 0)),
                   pl.BlockSpec((1, TOK_TILE, 2 * nh2), lambda i, j: (i, j, 0)),
                   pl.BlockSpec((1, 2 * nh2, TOK_TILE), lambda i, j: (i, 0, j))],
        out_shape=[head_shape, head_shape, head_shape,
                   jax.ShapeDtypeStruct((b, t, d), BF16),
                   jax.ShapeDtypeStruct((b, t, 2 * nh2), F32),
                   jax.ShapeDtypeStruct((b, 2 * nh2, t), F32)],
        scratch_shapes=[pltpu.VMEM((d // GDN_HEAD_DIM, TOK_TILE + 2 * CONV_HALO, GDN_HEAD_DIM), F32),
                        pltpu.VMEM((3 * GDN_HEADS, TOK_TILE + 2 * CONV_HALO, GDN_HEAD_DIM), F32)],
        compiler_params=_params(("parallel", "parallel")),
        name="gdn_in_proj",
    )(xa, xa, xa, modl, g0, wqkv, wz, wab, wab_t, conv_w,
      a_log.reshape(1, nh2), dt_bias.reshape(1, nh2), a_log.reshape(nh2, 1), dt_bias.reshape(nh2, 1))


def _gdn_scan_body(qf, kf, vf, qb, kb, vb, gbf, gbb, gtf, gtb, of, ob, s_ref):
    @pl.when(pl.program_id(1) == 0)
    def _():
        s_ref[...] = jnp.zeros_like(s_ref)

    c = GDN_CHUNK
    row = lax.broadcasted_iota(jnp.int32, (c, c), 0)
    col = lax.broadcasted_iota(jnp.int32, (c, c), 1)
    lower = jnp.where(row >= col, 1.0, 0.0)
    upper = jnp.where(row <= col, 1.0, 0.0)
    row = lax.broadcasted_iota(jnp.int32, (c, 2 * c), 0)
    col = lax.broadcasted_iota(jnp.int32, (c, 2 * c), 1)
    nh2 = 2 * GDN_HEADS
    hi = lax.Precision.HIGHEST
    n_sub = qf.shape[2] // c
    refs = (qf, kf, vf, qb, kb, vb, gbf, gbb, gtf, gtb, of, ob)
    chunks = [[] for _ in range(n_sub)]
    local = [_gdn_local_stages(chunks[sub], sub, n_sub - 1 - sub, c, row, col, lower, upper, nh2, hi, refs)
             for sub in range(n_sub)]
    carried = [_gdn_state_stages(chunks[sub], c, s_ref) for sub in range(n_sub)]
    for first in local[:SCAN_LOCAL_AHEAD]:
        for _ in first:
            pass
    for sub in range(n_sub):
        ahead = local[sub + SCAN_LOCAL_AHEAD] if sub + SCAN_LOCAL_AHEAD < n_sub else iter(())
        for _ in carried[sub]:
            for _ in range(LOCAL_STAGES_PER_STATE_STAGE):
                next(ahead, None)
        for _ in ahead:
            pass


def _gdn_chunk_chains(sub_f, sub_b, c, row, col, lower, upper, nh2, hi,
                      qf, kf, vf, qb, kb, vb, gbf, gbb, gtf, gtb, of, ob):
    dh = GDN_HEAD_DIM
    left = col < c
    zeros = jnp.zeros((c, dh), BF16)

    def wide(x0, x1):
        return jnp.concatenate([jnp.broadcast_to(x0, (c, dh)), jnp.broadcast_to(x1, (c, dh))], axis=1)

    pairs = []
    for direction, (q_ref, k_ref, v_ref, gb_ref, gt_ref, o_ref, sub) in enumerate(
            ((qf, kf, vf, gbf, gtf, of, sub_f), (qb, kb, vb, gbb, gtb, ob, sub_b))):
        r0 = sub * c
        gb = gb_ref[0, r0:r0 + c, :]
        gt = gt_ref[0, :, r0:r0 + c]
        tri_c, tri_r = (lower, upper) if direction == 0 else (upper, lower)
        gc = jnp.dot(tri_c, gb[:, :nh2], precision=hi, preferred_element_type=F32)
        gr = jnp.dot(gt[:nh2], tri_r, precision=hi, preferred_element_type=F32)
        incl = (row >= (col & (c - 1))) if direction == 0 else (row <= (col & (c - 1)))
        strict = (row > (col & (c - 1))) if direction == 0 else (row < (col & (c - 1)))
        for pr in range(GDN_HEADS // 2):
            h0 = 2 * pr
            ch = direction * GDN_HEADS + h0
            gcol0, gcol1 = gc[:, ch:ch + 1], gc[:, ch + 1:ch + 2]
            beta0, beta1 = gb[:, nh2 + ch:nh2 + ch + 1], gb[:, nh2 + ch + 1:nh2 + ch + 2]
            gcol = jnp.where(left, gcol0, gcol1)
            grow = jnp.concatenate([gr[ch:ch + 1, :], gr[ch + 1:ch + 2, :]], axis=1)
            edge = c - 1 if direction == 0 else 0
            tot0, tot1 = gcol0[edge:edge + 1], gcol1[edge:edge + 1]
            k0, k1 = k_ref[0, h0, r0:r0 + c, :], k_ref[0, h0 + 1, r0:r0 + c, :]
            q0, q1 = q_ref[0, h0, r0:r0 + c, :], q_ref[0, h0 + 1, r0:r0 + c, :]
            pairs.append(dict(
                idx=direction * (GDN_HEADS // 2) + pr, h0=h0, lower=direction == 0, strict=strict,
                o_ref=o_ref, r0=r0, left=left,
                kq=jnp.concatenate([jnp.concatenate([k0, q0], axis=0),
                                    jnp.concatenate([k1, q1], axis=0)], axis=1),
                k_diag=jnp.concatenate([jnp.concatenate([k0, zeros], axis=1),
                                        jnp.concatenate([zeros, k1], axis=1)], axis=0),
                k_rows=jnp.concatenate([k0, k1], axis=0),
                v=jnp.concatenate([v_ref[0, h0, r0:r0 + c, :], v_ref[0, h0 + 1, r0:r0 + c, :]], axis=1),
                beta=jnp.where(left, beta0, beta1), beta_w=wide(beta0, beta1),
                egc_w=wide(jnp.exp(gcol0), jnp.exp(gcol1)),
                dec_w=wide(jnp.exp(tot0 - gcol0), jnp.exp(tot1 - gcol1)),
                etot_w=jnp.concatenate([jnp.broadcast_to(jnp.exp(tot0), (1, dh)),
                                        jnp.broadcast_to(jnp.exp(tot1), (1, dh))], axis=1),
                decay=jnp.where(incl, jnp.exp(jnp.where(incl, gcol - grow, 0.0)), 0.0)))
    return pairs


def _gdn_local_stages(chains, sub_f, sub_b, c, row, col, lower, upper, nh2, hi, refs):
    chains.extend(_gdn_chunk_chains(sub_f, sub_b, c, row, col, lower, upper, nh2, hi, *refs))
    for w in chains:
        w["gram"] = lax.dot_general(w["kq"], w["k_diag"], _NT, preferred_element_type=F32)
    yield
    ms = [jnp.where(w["strict"], w["beta"] * w["gram"][:c] * w["decay"], 0.0) for w in chains]
    left = col < c
    colh = col & (c - 1)

    def diag2(x):
        return jnp.concatenate([jnp.where(left, x, 0.0), jnp.where(left, 0.0, x)], axis=0).astype(BF16)

    t_offs = None
    k = 1
    while k < c:
        same = (row ^ colh) < 2 * k
        joins = {True: same & ((row & k) != 0) & ((colh & k) == 0),
                 False: same & ((colh & k) != 0) & ((row & k) == 0)}
        parts = [jnp.where(joins[w["lower"]], m, 0.0) for m, w in zip(ms, chains)]
        if t_offs is None:
            t_offs = [-a for a in parts]
        else:
            zs = [a + jnp.dot(t.astype(BF16), diag2(a), preferred_element_type=F32)
                  for a, t in zip(parts, t_offs)]
            yield
            t_offs = [t - z - jnp.dot(z.astype(BF16), diag2(t), preferred_element_type=F32)
                      for t, z in zip(t_offs, zs)]
            yield
        k *= 2
    eye = jnp.where(row == colh, 1.0, 0.0)
    for w, t in zip(chains, t_offs):
        w["tmat"] = (t + eye).astype(BF16)


def _gdn_state_stages(chains, c, s_ref):
    dh = GDN_HEAD_DIM

    def diag2(x):
        z = jnp.zeros((x.shape[0], dh), x.dtype)
        return jnp.concatenate([jnp.concatenate([x[:, :dh], z], axis=1),
                                jnp.concatenate([z, x[:, dh:]], axis=1)], axis=0)

    for w in chains:
        w["state"] = s_ref[w["idx"]]
        w["ks"] = jnp.dot(w["kq"], diag2(w["state"].astype(BF16)), preferred_element_type=F32)
    yield
    for w in chains:
        resid = (w["v"].astype(F32) - w["ks"][:c] * w["egc_w"]) * w["beta_w"]
        w["v_new"] = jnp.dot(w["tmat"], diag2(resid.astype(BF16)), preferred_element_type=F32)
        w["qs"] = w["ks"][c:] * w["egc_w"]
    yield
    for w in chains:
        attn = (w["gram"][c:] * w["decay"]).astype(BF16)
        o = w["qs"] + jnp.dot(attn, diag2(w["v_new"].astype(BF16)), preferred_element_type=F32)
        for m in range(2):
            w["o_ref"][0, w["h0"] + m, w["r0"]:w["r0"] + c, :] = o[:, m * dh:(m + 1) * dh].astype(w["o_ref"].dtype)
    yield
    for w in chains:
        v_dec = diag2((w["v_new"] * w["dec_w"]).astype(BF16))
        s_ref[w["idx"]] = (w["state"] * w["etot_w"]
                           + lax.dot_general(w["k_rows"], v_dec, _TN, preferred_element_type=F32))
    yield


def _gdn_scan(q, k, v, gb, gbt):
    b, nh, t, dh = q.shape
    nt = t // TOK_TILE
    nh2 = 2 * GDN_HEADS

    def bwd(s):
        return jnp.where(s == 0, 0, nt - s)

    head_f = pl.BlockSpec((1, nh, TOK_TILE, dh), lambda i, s: (i, 0, s, 0))
    head_b = pl.BlockSpec((1, nh, TOK_TILE, dh), lambda i, s: (i, 0, bwd(s), 0))
    out_shape = jax.ShapeDtypeStruct((b, nh, t, dh), BF16)
    return pl.pallas_call(
        _gdn_scan_body,
        grid=(b, nt),
        in_specs=[head_f, head_f, head_f, head_b, head_b, head_b,
                  pl.BlockSpec((1, TOK_TILE, 2 * nh2), lambda i, s: (i, s, 0)),
                  pl.BlockSpec((1, TOK_TILE, 2 * nh2), lambda i, s: (i, bwd(s), 0)),
                  pl.BlockSpec((1, 2 * nh2, TOK_TILE), lambda i, s: (i, 0, s)),
                  pl.BlockSpec((1, 2 * nh2, TOK_TILE), lambda i, s: (i, 0, bwd(s)))],
        out_specs=[head_f, head_b],
        out_shape=[out_shape, out_shape],
        scratch_shapes=[pltpu.VMEM((nh2 // 2, dh, 2 * dh), F32)],
        compiler_params=_params(("arbitrary", "arbitrary")),
        name="gdn_chunk_scan",
    )(q, k, v, q, k, v, gb, gb, gbt, gbt)


def _rope_tables(n_lat):
    rows = n_lat // GRID_W
    row_ids = jnp.repeat(jnp.arange(rows, dtype=F32), GRID_W)[:n_lat]
    col_ids = jnp.tile(jnp.arange(GRID_W, dtype=F32), rows)[:n_lat]
    axis_dim = HEAD_DIM // 2
    inv_freq = ROPE_THETA ** (-jnp.arange(0, axis_dim, 2, dtype=F32) / axis_dim)
    ang_r = row_ids[:, None] * inv_freq
    ang_c = col_ids[:, None] * inv_freq
    ang = jnp.concatenate([ang_r, ang_r, ang_c, ang_c], axis=-1)
    cos = jnp.concatenate([jnp.ones((CTX_LEN, HEAD_DIM), F32), jnp.cos(ang)], axis=0)
    sin = jnp.concatenate([jnp.zeros((CTX_LEN, HEAD_DIM), F32), jnp.sin(ang)], axis=0)
    sign = jnp.tile(jnp.repeat(jnp.array([-1.0, 1.0], F32), HEAD_DIM // 4), 2)
    return cos.T, (sin * sign).T


def kernel(x, c, ctx, c_ctx, ada_w, ada_b, norm_g, attn_w_in, attn_w_out, diff_lambda, diff_subln_g,
           gqa_qk_g, gdn_w_in, gdn_conv_w, gdn_a_log, gdn_dt_bias, gdn_norm_g, gdn_w_out,
           ffn_w_gate_up, ffn_w_down):
    b, n_lat, d = x.shape
    depth = ada_w.shape[0]
    assert d == D_MODEL and ctx.shape[1] == CTX_LEN and n_lat % TOK_TILE == 0
    rows = -(-(b + 1) // V7X_SUBLANES) * V7X_SUBLANES
    cc = jnp.concatenate([c, c_ctx[None], jnp.zeros((rows - b - 1, d), F32)], axis=0)
    mods = _modulation(cc, ada_w, ada_b)
    cos_t, sin_t = _rope_tables(n_lat)
    xa = None

    for l in range(depth):
        i = l // 2
        last = l == depth - 1
        ml = mods[l]
        modl = jnp.stack([jnp.broadcast_to(ml[b].reshape(1, 6, d), (b, 6, d)),
                          ml[:b].reshape(b, 6, d)], axis=1)
        g = norm_g[l]
        wgu = ffn_w_gate_up[l].astype(BF16)
        wd = ffn_w_down[l].astype(BF16)
        if l % 2 == 0:
            lam_init = 0.8 - 0.6 * math.exp(-0.3 * l)
            w = attn_w_in[i]
            o_dv = 2 * DIFF_QK_W
            o_gq = o_dv + DIFF_V_W
            o_gk = o_gq + GQA_Q_W
            o_gv = o_gk + GQA_KV_W
            w_t = jnp.concatenate([w[:, :DIFF_QK_W], w[:, o_gq:o_gk],
                                   w[:, DIFF_QK_W:o_dv], w[:, o_gk:o_gv],
                                   w[:, o_dv:o_gq], w[:, o_gv:]],
                                  axis=1).T.astype(BF16)
            qkg = gqa_qk_g[i].reshape(2, HEAD_DIM, 1)
            if l == 0:
                qt, k, vt, xa = _attn_proj(x, modl, g[0:1], w_t, cos_t, sin_t, qkg, ctx=ctx)
            else:
                qt, k, vt = _attn_proj(xa, modl, g[0:1], w_t, cos_t, sin_t, qkg)
            o = _attention(qt, k, vt, diff_lambda[i], diff_subln_g[i].reshape(DIFF_V_DIM, 1), lam_init)
            xa = _post_ffn(xa, o, modl, g, attn_w_out[i].astype(BF16), wgu, wd, gdn=False, latent_only=last)
        else:
            w = gdn_w_in[i]
            wab = w[:, 4 * GDN_W:]
            q, k, v, sz, gb, gbt = _gdn_proj(
                xa, modl, g[0:1], w[:, :3 * GDN_W].astype(BF16), w[:, 3 * GDN_W:4 * GDN_W].astype(BF16),
                wab.astype(BF16), wab.T.astype(BF16), gdn_conv_w[i], gdn_a_log[i], gdn_dt_bias[i])
            of, ob = _gdn_scan(q, k, v, gb, gbt)
            xa = _post_ffn(xa, (of, ob, sz, gdn_norm_g[i].reshape(1, GDN_HEAD_DIM)), modl, g,
                           gdn_w_out[i].astype(BF16), wgu, wd, gdn=True, latent_only=last)
    return xa
```

```python
import functools
import math

import jax
import jax.numpy as jnp
from jax import lax
from jax.experimental import pallas as pl
from jax.experimental.pallas import tpu as pltpu

F32 = jnp.float32
BF16 = jnp.bfloat16

D_MODEL = 1024
CTX_LEN = 256
GRID_W = 64
RMS_EPS = 1e-6
L2_EPS = 1e-6
ROPE_THETA = 10000.0

HEAD_DIM = 64
ATTN_SCALE = HEAD_DIM ** -0.5
DIFF_HEADS = 4
DIFF_V_DIM = 2 * HEAD_DIM
GQA_Q_HEADS = 8
GQA_KV_HEADS = 2
GQA_REP = GQA_Q_HEADS // GQA_KV_HEADS
DIFF_QK_W = DIFF_HEADS * 2 * HEAD_DIM
DIFF_V_W = DIFF_HEADS * DIFF_V_DIM
GQA_Q_W = GQA_Q_HEADS * HEAD_DIM
GQA_KV_W = GQA_KV_HEADS * HEAD_DIM
Q_W = DIFF_QK_W + GQA_Q_W
K_W = DIFF_QK_W + GQA_KV_W
V_W = DIFF_V_W + GQA_KV_W
SUM_ROWS = 16
VT_ROWS = V_W + (DIFF_HEADS + GQA_KV_HEADS) * SUM_ROWS
LOG2_E = math.log2(math.e)
N_MAPS = 2 * DIFF_HEADS + GQA_Q_HEADS
QK_AHEAD = 5

GDN_HEADS = 8
GDN_HEAD_DIM = 128
GDN_W = GDN_HEADS * GDN_HEAD_DIM
GDN_CONV_K = 4
GDN_CONV_LEFT = 2
GDN_CHUNK = 64
SCAN_LOCAL_STAGES = 1 + 2 * (GDN_CHUNK.bit_length() - 2)
SCAN_STATE_STAGES = 4
LOCAL_STAGES_PER_STATE_STAGE = -(-SCAN_LOCAL_STAGES // SCAN_STATE_STAGES)
SCAN_LOCAL_AHEAD = 1
V7X_SUBLANES = 8
CONV_HALO = V7X_SUBLANES

FFN_HIDDEN = 2816
FFN_CHUNK = 256
FFN_GROUP = 2
PROJ_GROUP = 2

TOK_TILE = CTX_LEN
KV_CHUNK = TOK_TILE
V7X_VMEM_BYTES = 64 * 1024 * 1024
VMEM_LIMIT = V7X_VMEM_BYTES * 7 // 8

_NT = (((1,), (1,)), ((), ()))
_TN = (((0,), (0,)), ((), ()))


def _rms(x, g):
    return x * lax.rsqrt(jnp.mean(x * x, axis=-1, keepdims=True) + RMS_EPS) * g


def _silu(x):
    return x * jax.nn.sigmoid(x)


def _params(sem):
    return pltpu.CompilerParams(dimension_semantics=sem, vmem_limit_bytes=VMEM_LIMIT)


def _const_spec(shape):
    n = len(shape)
    return pl.BlockSpec(shape, lambda *_: (0,) * n)


def _mod_body(c_ref, w_ref, b_ref, o_ref):
    sc = _silu(c_ref[...])
    o_ref[0] = jnp.dot(sc, w_ref[0], precision=lax.Precision.HIGHEST,
                       preferred_element_type=F32) + b_ref[0]


def _modulation(cc, ada_w, ada_b):
    depth, d, w6 = ada_w.shape
    rows = cc.shape[0]
    nblk = w6 // d
    return pl.pallas_call(
        _mod_body,
        grid=(depth, nblk),
        in_specs=[pl.BlockSpec((rows, d), lambda l, j: (0, 0)),
                  pl.BlockSpec((1, d, d), lambda l, j: (l, 0, j)),
                  pl.BlockSpec((1, 1, d), lambda l, j: (l, 0, j))],
        out_specs=pl.BlockSpec((1, rows, d), lambda l, j: (l, 0, j)),
        out_shape=jax.ShapeDtypeStruct((depth, rows, w6), F32),
        compiler_params=_params(("parallel", "parallel")),
        name="adaln_mod",
    )(cc, ada_w, ada_b.reshape(depth, 1, w6))


def _attn_proj_body(group, nt, assemble, *refs):
    per = 5 if assemble else 4
    tiles = [refs[per * m:per * (m + 1)] for m in range(group)]
    g_ref, w_ref, qkg_ref = refs[per * group:per * group + 3]
    outs = refs[per * group + 3:]
    hs = []
    for m, tile in enumerate(tiles):
        if assemble:
            j = (pl.program_id(0) * group + m) % nt
            x = jnp.where(j == 0, tile[0][0], tile[1][0])
            outs[3][m] = x
        else:
            x = tile[0][0]
        mod = tile[-3][0, 0]
        hs.append((_rms(x, g_ref[...]) * (1.0 + mod[1:2]) + mod[0:1]).astype(BF16))
    for m, (tile, h) in enumerate(zip(tiles, hs)):
        _attn_proj_heads(m, h, w_ref, tile[-2], tile[-1], qkg_ref, *outs[:3])


def _attn_proj_heads(m, h, w_ref, cos_ref, sin_ref, qkg_ref, qt_out, k_out, vt_out):
    def project(r0, rows):
        return lax.dot_general(w_ref[r0:r0 + rows, :], h, _NT, preferred_element_type=F32)

    pk = project(Q_W, K_W)
    pq = project(0, Q_W)
    pv = project(Q_W + K_W, V_W)
    cos = cos_ref[...]
    sin = sin_ref[...]

    def rope(xh):
        swapped = jnp.concatenate([xh[16:32], xh[0:16], xh[48:64], xh[32:48]], axis=0)
        return xh * cos + swapped * sin

    def norm(xh, g):
        r = lax.rsqrt(jnp.mean(xh * xh, axis=0, keepdims=True) + RMS_EPS)
        return xh * r * g

    gq_g = qkg_ref[0]
    gk_g = qkg_ref[1]
    k_heads = []
    for j in range(K_W // HEAD_DIM):
        r0 = j * HEAD_DIM
        xh = pk[r0:r0 + HEAD_DIM]
        if r0 >= DIFF_QK_W:
            xh = norm(xh, gk_g)
        k_heads.append(rope(xh))
    k_out[m] = jnp.concatenate(k_heads, axis=0).T.astype(BF16)
    for j in range(Q_W // HEAD_DIM):
        r0 = j * HEAD_DIM
        xh = pq[r0:r0 + HEAD_DIM]
        if r0 >= DIFF_QK_W:
            xh = norm(xh, gq_g)
        qt_out[m, r0:r0 + HEAD_DIM, :] = (rope(xh) * (ATTN_SCALE * LOG2_E)).astype(BF16)
    ones = jnp.ones((SUM_ROWS, pv.shape[1]), F32)
    v_blocks = []
    for r0, width in ([(hd * DIFF_V_DIM, DIFF_V_DIM) for hd in range(DIFF_HEADS)]
                      + [(DIFF_V_W + grp * HEAD_DIM, HEAD_DIM) for grp in range(GQA_KV_HEADS)]):
        v_blocks += [pv[r0:r0 + width], ones]
    vt_out[m] = jnp.concatenate(v_blocks, axis=0).astype(BF16)


def _attn_proj(xa, modl, g0, w_t, cos_t, sin_t, qkg, ctx=None):
    assemble = ctx is not None
    b, t, d = xa.shape
    if assemble:
        t += CTX_LEN
    nt = t // TOK_TILE
    group = PROJ_GROUP
    assert (b * nt) % group == 0
    steps = b * nt // group

    def at(m, fn):
        def index_map(p):
            n = p * group + m
            return fn(n // nt, n % nt)
        return index_map

    in_specs, args = [], []
    for m in range(group):
        if assemble:
            in_specs += [pl.BlockSpec((1, TOK_TILE, d), at(m, lambda i, j: (i, 0, 0))),
                         pl.BlockSpec((1, TOK_TILE, d), at(m, lambda i, j: (i, jnp.maximum(j - 1, 0), 0)))]
            args += [ctx, xa]
        else:
            in_specs.append(pl.BlockSpec((1, TOK_TILE, d), at(m, lambda i, j: (i, j, 0))))
            args.append(xa)
        in_specs += [pl.BlockSpec((1, 1, 6, d), at(m, lambda i, j: (i, jnp.minimum(j, 1), 0, 0))),
                     pl.BlockSpec((HEAD_DIM, TOK_TILE), at(m, lambda i, j: (0, j))),
                     pl.BlockSpec((HEAD_DIM, TOK_TILE), at(m, lambda i, j: (0, j)))]
        args += [modl, cos_t, sin_t]
    in_specs += [_const_spec((1, d)), _const_spec(w_t.shape), _const_spec((2, HEAD_DIM, 1))]
    args += [g0, w_t, qkg]
    out_specs = [pl.BlockSpec((group, Q_W, TOK_TILE), lambda p: (p, 0, 0)),
                 pl.BlockSpec((group, TOK_TILE, K_W), lambda p: (p, 0, 0)),
                 pl.BlockSpec((group, VT_ROWS, TOK_TILE), lambda p: (p, 0, 0))]
    out_shape = [jax.ShapeDtypeStruct((b * nt, Q_W, TOK_TILE), BF16),
                 jax.ShapeDtypeStruct((b * nt, TOK_TILE, K_W), BF16),
                 jax.ShapeDtypeStruct((b * nt, VT_ROWS, TOK_TILE), BF16)]
    if assemble:
        out_specs.append(pl.BlockSpec((group, TOK_TILE, d), lambda p: (p, 0, 0)))
        out_shape.append(jax.ShapeDtypeStruct((b * nt, TOK_TILE, d), F32))
    outs = pl.pallas_call(
        functools.partial(_attn_proj_body, group, nt, assemble),
        grid=(steps,),
        in_specs=in_specs,
        out_specs=out_specs,
        out_shape=out_shape,
        compiler_params=_params(("parallel",)),
        name="attn_in_proj",
    )(*args)
    res = (outs[0].reshape(b, nt, Q_W, TOK_TILE), outs[1].reshape(b, t, K_W),
           outs[2].reshape(b, nt, VT_ROWS, TOK_TILE))
    return res + (outs[3].reshape(b, t, d),) if assemble else res


def _attn_body(lam_init, qt_ref, k_ref, vt_ref, lam_ref, subln_ref, o_ref, qpad_ref, m_ref, acc_ref):
    qi = pl.program_id(1)
    slab = 2 * HEAD_DIM
    diff_blk = DIFF_V_DIM + SUM_ROWS
    gqa_blk = HEAD_DIM + SUM_ROWS
    maps = []
    for hd in range(DIFF_HEADS):
        for mm in range(2):
            maps.append((2 * hd + mm, hd, hd * diff_blk, diff_blk))
    for hq in range(GQA_Q_HEADS):
        grp = hq // GQA_REP
        maps.append((2 * DIFF_HEADS + hq, DIFF_HEADS, DIFF_HEADS * diff_blk + grp * gqa_blk, gqa_blk))

    zeros = jnp.zeros((HEAD_DIM, qt_ref.shape[3]), BF16)
    for i, (qh, _, _, _) in enumerate(maps):
        half = (qh % 2) if qh < 2 * DIFF_HEADS else (qh - 2 * DIFF_HEADS) // GQA_REP
        qh_t = qt_ref[0, 0, qh * HEAD_DIM:(qh + 1) * HEAD_DIM, :]
        qpad_ref[i] = jnp.concatenate([qh_t, zeros] if half == 0 else [zeros, qh_t], axis=0)
    def scores(c, i):
        ks = maps[i][1]
        return jnp.dot(k_ref[0, c * KV_CHUNK:(c + 1) * KV_CHUNK, ks * slab:(ks + 1) * slab], qpad_ref[i],
                       preferred_element_type=F32)

    def run(n_chunks):
        items = [(c, i) for c in range(n_chunks) for i in range(N_MAPS)]
        pending = [scores(*items[n]) for n in range(QK_AHEAD)]
        for n, (c, i) in enumerate(items):
            s = pending.pop(0)
            if n + QK_AHEAD < len(items):
                pending.append(scores(*items[n + QK_AHEAD]))
            _, _, v_row0, v_w = maps[i]
            m_new = jnp.max(s, axis=0, keepdims=True)
            if c > 0:
                m_old = m_ref[i]
                m_new = jnp.maximum(m_old, m_new)
                alpha = jnp.exp2(m_old - m_new)
            m_ref[i] = m_new
            p = jnp.exp2(s - m_new)
            pv = jnp.dot(vt_ref[0, c, v_row0:v_row0 + v_w, :], p.astype(BF16), preferred_element_type=F32)
            acc_ref[i, 0:v_w, :] = pv if c == 0 else alpha * acc_ref[i, 0:v_w, :] + pv

    @pl.when(qi == 0)
    def _():
        run(CTX_LEN // KV_CHUNK)

    @pl.when(qi > 0)
    def _():
        run(vt_ref.shape[1])

    lv = lam_ref[...]
    lam = (jnp.exp(jnp.sum(lv[0:1] * lv[1:2], axis=-1, keepdims=True))
           - jnp.exp(jnp.sum(lv[2:3] * lv[3:4], axis=-1, keepdims=True)) + lam_init)
    def normalized(i, width):
        return acc_ref[i, 0:width, :] / acc_ref[i, width:width + 1, :]

    for hd in range(DIFF_HEADS):
        od = normalized(2 * hd, DIFF_V_DIM) - lam * normalized(2 * hd + 1, DIFF_V_DIM)
        od = od * lax.rsqrt(jnp.mean(od * od, axis=0, keepdims=True) + RMS_EPS) * subln_ref[...]
        o_ref[0, 0, hd * DIFF_V_DIM:(hd + 1) * DIFF_V_DIM, :] = (od * (1.0 - lam_init)).astype(o_ref.dtype)
    for hq in range(GQA_Q_HEADS):
        r0 = DIFF_V_W + hq * HEAD_DIM
        o_ref[0, 0, r0:r0 + HEAD_DIM, :] = normalized(2 * DIFF_HEADS + hq, HEAD_DIM).astype(o_ref.dtype)


def _attention(qt, k, vt, lam_vec, subln_g, lam_init):
    b, t, _ = k.shape
    nq = t // TOK_TILE
    return pl.pallas_call(
        functools.partial(_attn_body, lam_init),
        grid=(b, nq),
        in_specs=[pl.BlockSpec((1, 1, Q_W, TOK_TILE), lambda i, j: (i, j, 0, 0)),
                  pl.BlockSpec((1, t, K_W), lambda i, j: (i, 0, 0)),
                  pl.BlockSpec((1,) + vt.shape[1:], lambda i, j: (i, 0, 0, 0)),
                  _const_spec((4, HEAD_DIM)),
                  _const_spec((DIFF_V_DIM, 1))],
        out_specs=pl.BlockSpec((1, 1, D_MODEL, TOK_TILE), lambda i, j: (i, j, 0, 0)),
        out_shape=jax.ShapeDtypeStruct((b, nq, D_MODEL, TOK_TILE), BF16),
        scratch_shapes=[pltpu.VMEM((N_MAPS, 2 * HEAD_DIM, TOK_TILE), BF16),
                        pltpu.VMEM((N_MAPS, 1, TOK_TILE), F32),
                        pltpu.VMEM((N_MAPS, DIFF_V_DIM + SUM_ROWS, TOK_TILE), F32)],
        compiler_params=_params(("parallel", "parallel")),
        name="diff_gqa_attention",
    )(qt, k, vt, lam_vec, subln_g)


def _post_body(gdn, group, *refs):
    per = 5 if gdn else 3
    tiles = [refs[per * m:per * (m + 1)] for m in range(group)]
    rest = refs[per * group:]
    if gdn:
        ng_ref, rest = rest[0], rest[1:]
    g_ref, wo_ref, wgu_ref, wd_ref, xo_ref, a_ref = rest
    g = g_ref[...]

    def mixer_out(tile):
        if not gdn:
            return tile[1][0, 0].astype(F32).T.astype(BF16)
        _, of_ref, ob_ref, sz_ref, _ = tile
        parts = []
        for hd in range(GDN_HEADS):
            o = of_ref[0, hd].astype(F32) + ob_ref[0, hd].astype(F32)
            parts.append(_rms(o, ng_ref[...]))
        return (jnp.concatenate(parts, axis=-1) * sz_ref[0].astype(F32)).astype(BF16)

    ys = [jnp.dot(mixer_out(tile), wo_ref[...], preferred_element_type=F32) for tile in tiles]
    for m, (tile, y) in enumerate(zip(tiles, ys)):
        mod = tile[-1][0, 0]
        x = tile[0][0] + mod[2:3] * _rms(y, g[1:2])
        h = (_rms(x, g[2:3]) * (1.0 + mod[4:5]) + mod[3:4]).astype(BF16)
        for c in range(FFN_HIDDEN // FFN_CHUNK):
            c0 = c * FFN_CHUNK
            gate = jnp.dot(h, wgu_ref[:, c0:c0 + FFN_CHUNK], preferred_element_type=F32)
            up = jnp.dot(h, wgu_ref[:, FFN_HIDDEN + c0:FFN_HIDDEN + c0 + FFN_CHUNK],
                         preferred_element_type=F32)
            a_ref[m, :, c0:c0 + FFN_CHUNK] = (_silu(gate) * up).astype(BF16)
        ff = jnp.dot(a_ref[m], wd_ref[...], preferred_element_type=F32)
        xo_ref[0, m * TOK_TILE:(m + 1) * TOK_TILE, :] = x + mod[5:6] * _rms(ff, g[3:4])


def _post_ffn(xa, mixer_out, modl, g, wo, wgu, wd, gdn, latent_only):
    b, t, d = xa.shape
    skip = CTX_LEN // TOK_TILE if latent_only else 0
    nt = t // TOK_TILE - skip
    group = FFN_GROUP
    assert (b * nt) % group == 0

    def at(m, fn):
        def index_map(p):
            n = p * group + m
            return fn(n // nt, n % nt + skip)
        return index_map

    in_specs, args = [], []
    for m in range(group):
        tile = pl.BlockSpec((1, TOK_TILE, d), at(m, lambda i, j: (i, j, 0)))
        if gdn:
            of, ob, sz, ng = mixer_out
            head_tile = pl.BlockSpec((1, GDN_HEADS, TOK_TILE, GDN_HEAD_DIM), at(m, lambda i, j: (i, 0, j, 0)))
            in_specs += [tile, head_tile, head_tile, tile]
            args += [xa, of, ob, sz]
        else:
            in_specs += [tile, pl.BlockSpec((1, 1, d, TOK_TILE), at(m, lambda i, j: (i, j, 0, 0)))]
            args += [xa, mixer_out]
        in_specs.append(pl.BlockSpec((1, 1, 6, d), at(m, lambda i, j: (i, jnp.minimum(j, 1), 0, 0))))
        args.append(modl)
    if gdn:
        in_specs.append(_const_spec((1, GDN_HEAD_DIM)))
        args.append(ng)
    in_specs += [_const_spec((4, d)), _const_spec(wo.shape), _const_spec(wgu.shape), _const_spec(wd.shape)]
    args += [g, wo, wgu, wd]
    rows = group * TOK_TILE
    out = pl.pallas_call(
        functools.partial(_post_body, gdn, group),
        grid=(b * nt // group,),
        in_specs=in_specs,
        out_specs=pl.BlockSpec((1, rows, d), lambda p: (p, 0, 0)),
        out_shape=jax.ShapeDtypeStruct((b * nt // group, rows, d), F32),
        scratch_shapes=[pltpu.VMEM((group, TOK_TILE, FFN_HIDDEN), BF16)],
        compiler_params=_params(("parallel",)),
        name="out_proj_ffn_gdn" if gdn else "out_proj_ffn_attn",
    )(*args)
    return out.reshape(b, nt * TOK_TILE, d)


def _gdn_proj_body(xp_ref, x_ref, xn_ref, mod_ref, g_ref, wqkv_ref, wz_ref, wab_ref, wabt_ref,
                   conv_ref, alog_ref, dtb_ref, alogt_ref, dtbt_ref,
                   q_out, k_out, v_out, sz_out, gb_out, gbt_out, xs_ref, ys_ref):
    j = pl.program_id(1)
    nt = pl.num_programs(1)
    mod = mod_ref[0, 0]
    g = g_ref[...]
    lanes = GDN_HEAD_DIM
    rows = TOK_TILE + 2 * CONV_HALO
    pitch = rows // 8

    def prep(xv):
        return _rms(xv, g) * (1.0 + mod[1:2]) + mod[0:1]

    prev_ok = jnp.where(j >= 2, 1.0, 0.0)
    next_ok = jnp.where(jnp.logical_and(j >= 1, j < nt - 1), 1.0, 0.0)
    h_f32 = prep(x_ref[0])
    h_main = h_f32.astype(BF16)
    z = jnp.dot(h_main, wz_ref[...], preferred_element_type=F32)
    h_cat = jnp.concatenate([prep(xp_ref[0]) * prev_ok, h_f32, prep(xn_ref[0]) * next_ok], axis=0)
    n_slab = h_cat.shape[1] // lanes
    for s in range(n_slab):
        xs_ref[s] = h_cat[:, s * lanes:(s + 1) * lanes]
    h_perm = jnp.concatenate(
        [jnp.concatenate([xs_ref[s, pl.ds(a, 8, stride=pitch), :] for s in range(n_slab)], axis=1)
         for a in range(pitch)], axis=0).astype(BF16)
    sections = [jnp.dot(h_perm, wqkv_ref[:, s * GDN_W:(s + 1) * GDN_W], preferred_element_type=F32)
                for s in range(3)]
    for sec, (p, out) in enumerate(zip(sections, (q_out, k_out, v_out))):
        cw = conv_ref[:, sec * GDN_W:(sec + 1) * GDN_W]

        def taps(groups):
            acc = groups[0] * cw[0:1]
            for tap in range(1, GDN_CONV_K):
                acc = acc + groups[tap] * cw[tap:tap + 1]
            return acc

        def grp(a):
            return p[8 * a:8 * (a + 1)]

        below = [pltpu.roll(grp(pitch - 2), 1, 0), pltpu.roll(grp(pitch - 1), 1, 0)]
        above = pltpu.roll(grp(0), 7, 0)
        mid = taps([p[8 * t:8 * (t + pitch - 3)] for t in range(GDN_CONV_K)])
        conv = jnp.concatenate([taps([below[0], below[1], grp(0), grp(1)]),
                                taps([below[1], grp(0), grp(1), grp(2)]),
                                mid,
                                taps([grp(pitch - 3), grp(pitch - 2), grp(pitch - 1), above])], axis=0)
        act = _silu(conv)
        for hd in range(GDN_HEADS):
            val = act[:, hd * lanes:(hd + 1) * lanes]
            if sec < 2:
                val = val * lax.rsqrt(jnp.sum(val * val, axis=-1, keepdims=True) + L2_EPS)
            if sec == 0:
                val = val * (GDN_HEAD_DIM ** -0.5)
            slab = sec * GDN_HEADS + hd
            for a in range(pitch):
                ys_ref[slab, pl.ds(a, 8, stride=pitch), :] = val[8 * a:8 * (a + 1)]
            out[0, hd] = ys_ref[slab, CONV_HALO:CONV_HALO + TOK_TILE, :].astype(BF16)
    sz_out[0] = _silu(z).astype(BF16)

    nh2 = 2 * GDN_HEADS
    ab = jnp.dot(h_main, wab_ref[...], preferred_element_type=F32)
    gdec = -jnp.exp(alog_ref[...]) * jax.nn.softplus(ab[:, :nh2] + dtb_ref[...])
    gb_out[0] = jnp.concatenate([gdec, jax.nn.sigmoid(ab[:, nh2:])], axis=-1)
    abt = lax.dot_general(wabt_ref[...], h_main, _NT, preferred_element_type=F32)
    gdec_t = -jnp.exp(alogt_ref[...]) * jax.nn.softplus(abt[:nh2] + dtbt_ref[...])
    gbt_out[0] = jnp.concatenate([gdec_t, jax.nn.sigmoid(abt[nh2:])], axis=0)


def _gdn_proj(xa, modl, g0, wqkv, wz, wab, wab_t, conv_w, a_log, dt_bias):
    b, t, d = xa.shape
    nt = t // TOK_TILE
    per = TOK_TILE // CONV_HALO
    last = t // CONV_HALO - 1
    nh2 = 2 * GDN_HEADS
    head_out = pl.BlockSpec((1, GDN_HEADS, TOK_TILE, GDN_HEAD_DIM), lambda i, j: (i, 0, j, 0))
    head_shape = jax.ShapeDtypeStruct((b, GDN_HEADS, t, GDN_HEAD_DIM), BF16)
    return pl.pallas_call(
        _gdn_proj_body,
        grid=(b, nt),
        in_specs=[pl.BlockSpec((1, CONV_HALO, d), lambda i, j: (i, jnp.maximum(j * per - 1, 0), 0)),
                  pl.BlockSpec((1, TOK_TILE, d), lambda i, j: (i, j, 0)),
                  pl.BlockSpec((1, CONV_HALO, d), lambda i, j: (i, jnp.minimum((j + 1) * per, last), 0)),
                  pl.BlockSpec((1, 1, 6, d), lambda i, j: (i, jnp.minimum(j, 1), 0, 0)),
                  _const_spec((1, d)),
                  _const_spec(wqkv.shape), _const_spec(wz.shape), _const_spec(wab.shape),
                  _const_spec(wab_t.shape), _const_spec(conv_w.shape),
                  _const_spec((1, nh2)), _const_spec((1, nh2)),
                  _const_spec((nh2, 1)), _const_spec((nh2, 1))],
        out_specs=[head_out, head_out, head_out,
                   pl.BlockSpec((1, TOK_TILE, d), lambda i, j: (i, j, 0)),
                   pl.BlockSpec((1, TOK_TILE, 2 * nh2), lambda i, j: (i, j, 0)),
                   pl.BlockSpec((1, 2 * nh2, TOK_TILE), lambda i, j: (i, 0, j))],
        out_shape=[head_shape, head_shape, head_shape,
                   jax.ShapeDtypeStruct((b, t, d), BF16),
                   jax.ShapeDtypeStruct((b, t, 2 * nh2), F32),
                   jax.ShapeDtypeStruct((b, 2 * nh2, t), F32)],
        scratch_shapes=[pltpu.VMEM((d // GDN_HEAD_DIM, TOK_TILE + 2 * CONV_HALO, GDN_HEAD_DIM), F32),
                        pltpu.VMEM((3 * GDN_HEADS, TOK_TILE + 2 * CONV_HALO, GDN_HEAD_DIM), F32)],
        compiler_params=_params(("parallel", "parallel")),
        name="gdn_in_proj",
    )(xa, xa, xa, modl, g0, wqkv, wz, wab, wab_t, conv_w,
      a_log.reshape(1, nh2), dt_bias.reshape(1, nh2), a_log.reshape(nh2, 1), dt_bias.reshape(nh2, 1))


def _gdn_scan_body(qf, kf, vf, qb, kb, vb, gbf, gbb, gtf, gtb, of, ob, s_ref):
    @pl.when(pl.program_id(1) == 0)
    def _():
        s_ref[...] = jnp.zeros_like(s_ref)

    c = GDN_CHUNK
    row = lax.broadcasted_iota(jnp.int32, (c, c), 0)
    col = lax.broadcasted_iota(jnp.int32, (c, c), 1)
    lower = jnp.where(row >= col, 1.0, 0.0)
    upper = jnp.where(row <= col, 1.0, 0.0)
    row = lax.broadcasted_iota(jnp.int32, (c, 2 * c), 0)
    col = lax.broadcasted_iota(jnp.int32, (c, 2 * c), 1)
    nh2 = 2 * GDN_HEADS
    hi = lax.Precision.HIGHEST
    n_sub = qf.shape[2] // c
    refs = (qf, kf, vf, qb, kb, vb, gbf, gbb, gtf, gtb, of, ob)
    chunks = [[] for _ in range(n_sub)]
    local = [_gdn_local_stages(chunks[sub], sub, n_sub - 1 - sub, c, row, col, lower, upper, nh2, hi, refs)
             for sub in range(n_sub)]
    carried = [_gdn_state_stages(chunks[sub], c, s_ref) for sub in range(n_sub)]
    for first in local[:SCAN_LOCAL_AHEAD]:
        for _ in first:
            pass
    for sub in range(n_sub):
        ahead = local[sub + SCAN_LOCAL_AHEAD] if sub + SCAN_LOCAL_AHEAD < n_sub else iter(())
        for _ in carried[sub]:
            for _ in range(LOCAL_STAGES_PER_STATE_STAGE):
                next(ahead, None)
        for _ in ahead:
            pass


def _gdn_chunk_chains(sub_f, sub_b, c, row, col, lower, upper, nh2, hi,
                      qf, kf, vf, qb, kb, vb, gbf, gbb, gtf, gtb, of, ob):
    dh = GDN_HEAD_DIM
    left = col < c
    zeros = jnp.zeros((c, dh), BF16)

    def wide(x0, x1):
        return jnp.concatenate([jnp.broadcast_to(x0, (c, dh)), jnp.broadcast_to(x1, (c, dh))], axis=1)

    pairs = []
    for direction, (q_ref, k_ref, v_ref, gb_ref, gt_ref, o_ref, sub) in enumerate(
            ((qf, kf, vf, gbf, gtf, of, sub_f), (qb, kb, vb, gbb, gtb, ob, sub_b))):
        r0 = sub * c
        gb = gb_ref[0, r0:r0 + c, :]
        gt = gt_ref[0, :, r0:r0 + c]
        tri_c, tri_r = (lower, upper) if direction == 0 else (upper, lower)
        gc = jnp.dot(tri_c, gb[:, :nh2], precision=hi, preferred_element_type=F32)
        gr = jnp.dot(gt[:nh2], tri_r, precision=hi, preferred_element_type=F32)
        incl = (row >= (col & (c - 1))) if direction == 0 else (row <= (col & (c - 1)))
        strict = (row > (col & (c - 1))) if direction == 0 else (row < (col & (c - 1)))
        for pr in range(GDN_HEADS // 2):
            h0 = 2 * pr
            ch = direction * GDN_HEADS + h0
            gcol0, gcol1 = gc[:, ch:ch + 1], gc[:, ch + 1:ch + 2]
            beta0, beta1 = gb[:, nh2 + ch:nh2 + ch + 1], gb[:, nh2 + ch + 1:nh2 + ch + 2]
            gcol = jnp.where(left, gcol0, gcol1)
            grow = jnp.concatenate([gr[ch:ch + 1, :], gr[ch + 1:ch + 2, :]], axis=1)
            edge = c - 1 if direction == 0 else 0
            tot0, tot1 = gcol0[edge:edge + 1], gcol1[edge:edge + 1]
            k0, k1 = k_ref[0, h0, r0:r0 + c, :], k_ref[0, h0 + 1, r0:r0 + c, :]
            q0, q1 = q_ref[0, h0, r0:r0 + c, :], q_ref[0, h0 + 1, r0:r0 + c, :]
            pairs.append(dict(
                idx=direction * (GDN_HEADS // 2) + pr, h0=h0, lower=direction == 0, strict=strict,
                o_ref=o_ref, r0=r0, left=left,
                kq=jnp.concatenate([jnp.concatenate([k0, q0], axis=0),
                                    jnp.concatenate([k1, q1], axis=0)], axis=1),
                k_diag=jnp.concatenate([jnp.concatenate([k0, zeros], axis=1),
                                        jnp.concatenate([zeros, k1], axis=1)], axis=0),
                k_rows=jnp.concatenate([k0, k1], axis=0),
                v=jnp.concatenate([v_ref[0, h0, r0:r0 + c, :], v_ref[0, h0 + 1, r0:r0 + c, :]], axis=1),
                beta=jnp.where(left, beta0, beta1), beta_w=wide(beta0, beta1),
                egc_w=wide(jnp.exp(gcol0), jnp.exp(gcol1)),
                dec_w=wide(jnp.exp(tot0 - gcol0), jnp.exp(tot1 - gcol1)),
                etot_w=jnp.concatenate([jnp.broadcast_to(jnp.exp(tot0), (1, dh)),
                                        jnp.broadcast_to(jnp.exp(tot1), (1, dh))], axis=1),
                decay=jnp.where(incl, jnp.exp(jnp.where(incl, gcol - grow, 0.0)), 0.0)))
    return pairs


def _gdn_local_stages(chains, sub_f, sub_b, c, row, col, lower, upper, nh2, hi, refs):
    chains.extend(_gdn_chunk_chains(sub_f, sub_b, c, row, col, lower, upper, nh2, hi, *refs))
    for w in chains:
        w["gram"] = lax.dot_general(w["kq"], w["k_diag"], _NT, preferred_element_type=F32)
    yield
    ms = [jnp.where(w["strict"], w["beta"] * w["gram"][:c] * w["decay"], 0.0) for w in chains]
    left = col < c
    colh = col & (c - 1)

    def diag2(x):
        return jnp.concatenate([jnp.where(left, x, 0.0), jnp.where(left, 0.0, x)], axis=0).astype(BF16)

    t_offs = None
    k = 1
    while k < c:
        same = (row ^ colh) < 2 * k
        joins = {True: same & ((row & k) != 0) & ((colh & k) == 0),
                 False: same & ((colh & k) != 0) & ((row & k) == 0)}
        parts = [jnp.where(joins[w["lower"]], m, 0.0) for m, w in zip(ms, chains)]
        if t_offs is None:
            t_offs = [-a for a in parts]
        else:
            zs = [a + jnp.dot(t.astype(BF16), diag2(a), preferred_element_type=F32)
                  for a, t in zip(parts, t_offs)]
            yield
            t_offs = [t - z - jnp.dot(z.astype(BF16), diag2(t), preferred_element_type=F32)
                      for t, z in zip(t_offs, zs)]
            yield
        k *= 2
    eye = jnp.where(row == colh, 1.0, 0.0)
    for w, t in zip(chains, t_offs):
        w["tmat"] = (t + eye).astype(BF16)


def _gdn_state_stages(chains, c, s_ref):
    dh = GDN_HEAD_DIM

    def diag2(x):
        z = jnp.zeros((x.shape[0], dh), x.dtype)
        return jnp.concatenate([jnp.concatenate([x[:, :dh], z], axis=1),
                                jnp.concatenate([z, x[:, dh:]], axis=1)], axis=0)

    for w in chains:
        w["state"] = s_ref[w["idx"]]
        w["ks"] = jnp.dot(w["kq"], diag2(w["state"].astype(BF16)), preferred_element_type=F32)
    yield
    for w in chains:
        resid = (w["v"].astype(F32) - w["ks"][:c] * w["egc_w"]) * w["beta_w"]
        w["v_new"] = jnp.dot(w["tmat"], diag2(resid.astype(BF16)), preferred_element_type=F32)
        w["qs"] = w["ks"][c:] * w["egc_w"]
    yield
    for w in chains:
        attn = (w["gram"][c:] * w["decay"]).astype(BF16)
        o = w["qs"] + jnp.dot(attn, diag2(w["v_new"].astype(BF16)), preferred_element_type=F32)
        for m in range(2):
            w["o_ref"][0, w["h0"] + m, w["r0"]:w["r0"] + c, :] = o[:, m * dh:(m + 1) * dh].astype(w["o_ref"].dtype)
    yield
    for w in chains:
        v_dec = diag2((w["v_new"] * w["dec_w"]).astype(BF16))
        s_ref[w["idx"]] = (w["state"] * w["etot_w"]
                           + lax.dot_general(w["k_rows"], v_dec, _TN, preferred_element_type=F32))
    yield


def _gdn_scan(q, k, v, gb, gbt):
    b, nh, t, dh = q.shape
    nt = t // TOK_TILE
    nh2 = 2 * GDN_HEADS

    def bwd(s):
        return jnp.where(s == 0, 0, nt - s)

    head_f = pl.BlockSpec((1, nh, TOK_TILE, dh), lambda i, s: (i, 0, s, 0))
    head_b = pl.BlockSpec((1, nh, TOK_TILE, dh), lambda i, s: (i, 0, bwd(s), 0))
    out_shape = jax.ShapeDtypeStruct((b, nh, t, dh), BF16)
    return pl.pallas_call(
        _gdn_scan_body,
        grid=(b, nt),
        in_specs=[head_f, head_f, head_f, head_b, head_b, head_b,
                  pl.BlockSpec((1, TOK_TILE, 2 * nh2), lambda i, s: (i, s, 0)),
                  pl.BlockSpec((1, TOK_TILE, 2 * nh2), lambda i, s: (i, bwd(s), 0)),
                  pl.BlockSpec((1, 2 * nh2, TOK_TILE), lambda i, s: (i, 0, s)),
                  pl.BlockSpec((1, 2 * nh2, TOK_TILE), lambda i, s: (i, 0, bwd(s)))],
        out_specs=[head_f, head_b],
        out_shape=[out_shape, out_shape],
        scratch_shapes=[pltpu.VMEM((nh2 // 2, dh, 2 * dh), F32)],
        compiler_params=_params(("arbitrary", "arbitrary")),
        name="gdn_chunk_scan",
    )(q, k, v, q, k, v, gb, gb, gbt, gbt)


def _rope_tables(n_lat):
    rows = n_lat // GRID_W
    row_ids = jnp.repeat(jnp.arange(rows, dtype=F32), GRID_W)[:n_lat]
    col_ids = jnp.tile(jnp.arange(GRID_W, dtype=F32), rows)[:n_lat]
    axis_dim = HEAD_DIM // 2
    inv_freq = ROPE_THETA ** (-jnp.arange(0, axis_dim, 2, dtype=F32) / axis_dim)
    ang_r = row_ids[:, None] * inv_freq
    ang_c = col_ids[:, None] * inv_freq
    ang = jnp.concatenate([ang_r, ang_r, ang_c, ang_c], axis=-1)
    cos = jnp.concatenate([jnp.ones((CTX_LEN, HEAD_DIM), F32), jnp.cos(ang)], axis=0)
    sin = jnp.concatenate([jnp.zeros((CTX_LEN, HEAD_DIM), F32), jnp.sin(ang)], axis=0)
    sign = jnp.tile(jnp.repeat(jnp.array([-1.0, 1.0], F32), HEAD_DIM // 4), 2)
    return cos.T, (sin * sign).T


def kernel(x, c, ctx, c_ctx, ada_w, ada_b, norm_g, attn_w_in, attn_w_out, diff_lambda, diff_subln_g,
           gqa_qk_g, gdn_w_in, gdn_conv_w, gdn_a_log, gdn_dt_bias, gdn_norm_g, gdn_w_out,
           ffn_w_gate_up, ffn_w_down):
    b, n_lat, d = x.shape
    depth = ada_w.shape[0]
    assert d == D_MODEL and ctx.shape[1] == CTX_LEN and n_lat % TOK_TILE == 0
    rows = -(-(b + 1) // V7X_SUBLANES) * V7X_SUBLANES
    cc = jnp.concatenate([c, c_ctx[None], jnp.zeros((rows - b - 1, d), F32)], axis=0)
    mods = _modulation(cc, ada_w, ada_b)
    cos_t, sin_t = _rope_tables(n_lat)
    xa = None

    for l in range(depth):
        i = l // 2
        last = l == depth - 1
        ml = mods[l]
        modl = jnp.stack([jnp.broadcast_to(ml[b].reshape(1, 6, d), (b, 6, d)),
                          ml[:b].reshape(b, 6, d)], axis=1)
        g = norm_g[l]
        wgu = ffn_w_gate_up[l].astype(BF16)
        wd = ffn_w_down[l].astype(BF16)
        if l % 2 == 0:
            lam_init = 0.8 - 0.6 * math.exp(-0.3 * l)
            w = attn_w_in[i]
            o_dv = 2 * DIFF_QK_W
            o_gq = o_dv + DIFF_V_W
            o_gk = o_gq + GQA_Q_W
            o_gv = o_gk + GQA_KV_W
            w_t = jnp.concatenate([w[:, :DIFF_QK_W], w[:, o_gq:o_gk],
                                   w[:, DIFF_QK_W:o_dv], w[:, o_gk:o_gv],
                                   w[:, o_dv:o_gq], w[:, o_gv:]],
                                  axis=1).T.astype(BF16)
            qkg = gqa_qk_g[i].reshape(2, HEAD_DIM, 1)
            if l == 0:
                qt, k, vt, xa = _attn_proj(x, modl, g[0:1], w_t, cos_t, sin_t, qkg, ctx=ctx)
            else:
                qt, k, vt = _attn_proj(xa, modl, g[0:1], w_t, cos_t, sin_t, qkg)
            o = _attention(qt, k, vt, diff_lambda[i], diff_subln_g[i].reshape(DIFF_V_DIM, 1), lam_init)
            xa = _post_ffn(xa, o, modl, g, attn_w_out[i].astype(BF16), wgu, wd, gdn=False, latent_only=last)
        else:
            w = gdn_w_in[i]
            wab = w[:, 4 * GDN_W:]
            q, k, v, sz, gb, gbt = _gdn_proj(
                xa, modl, g[0:1], w[:, :3 * GDN_W].astype(BF16), w[:, 3 * GDN_W:4 * GDN_W].astype(BF16),
                wab.astype(BF16), wab.T.astype(BF16), gdn_conv_w[i], gdn_a_log[i], gdn_dt_bias[i])
            of, ob = _gdn_scan(q, k, v, gb, gbt)
            xa = _post_ffn(xa, (of, ob, sz, gdn_norm_g[i].reshape(1, GDN_HEAD_DIM)), modl, g,
                           gdn_w_out[i].astype(BF16), wgu, wd, gdn=True, latent_only=last)
    return xa
```

```python
import functools
import math

import jax
import jax.numpy as jnp
from jax import lax
from jax.experimental import pallas as pl
from jax.experimental.pallas import tpu as pltpu

F32 = jnp.float32
BF16 = jnp.bfloat16

D_MODEL = 1024
CTX_LEN = 256
GRID_W = 64
RMS_EPS = 1e-6
L2_EPS = 1e-6
ROPE_THETA = 10000.0

HEAD_DIM = 64
ATTN_SCALE = HEAD_DIM ** -0.5
DIFF_HEADS = 4
DIFF_V_DIM = 2 * HEAD_DIM
GQA_Q_HEADS = 8
GQA_KV_HEADS = 2
GQA_REP = GQA_Q_HEADS // GQA_KV_HEADS
DIFF_QK_W = DIFF_HEADS * 2 * HEAD_DIM
DIFF_V_W = DIFF_HEADS * DIFF_V_DIM
GQA_Q_W = GQA_Q_HEADS * HEAD_DIM
GQA_KV_W = GQA_KV_HEADS * HEAD_DIM
Q_W = DIFF_QK_W + GQA_Q_W
K_W = DIFF_QK_W + GQA_KV_W
V_W = DIFF_V_W + GQA_KV_W
SUM_ROWS = 16
VT_ROWS = V_W + (DIFF_HEADS + GQA_KV_HEADS) * SUM_ROWS
LOG2_E = math.log2(math.e)
N_MAPS = 2 * DIFF_HEADS + GQA_Q_HEADS
QK_AHEAD = 5

GDN_HEADS = 8
GDN_HEAD_DIM = 128
GDN_W = GDN_HEADS * GDN_HEAD_DIM
GDN_CONV_K = 4
GDN_CONV_LEFT = 2
GDN_CHUNK = 64
SCAN_LOCAL_STAGES = 1 + 2 * (GDN_CHUNK.bit_length() - 2)
SCAN_STATE_STAGES = 4
LOCAL_STAGES_PER_STATE_STAGE = -(-SCAN_LOCAL_STAGES // SCAN_STATE_STAGES)
SCAN_LOCAL_AHEAD = 1
V7X_SUBLANES = 8
CONV_HALO = V7X_SUBLANES

FFN_HIDDEN = 2816
FFN_CHUNK = 256
FFN_GROUP = 2
READOUT_AT_CHUNK = 2
PROJ_GROUP = 2

TOK_TILE = CTX_LEN
KV_CHUNK = TOK_TILE
V7X_VMEM_BYTES = 64 * 1024 * 1024
VMEM_LIMIT = V7X_VMEM_BYTES * 7 // 8

_NT = (((1,), (1,)), ((), ()))
_TN = (((0,), (0,)), ((), ()))


def _rms(x, g):
    return x * lax.rsqrt(jnp.mean(x * x, axis=-1, keepdims=True) + RMS_EPS) * g


def _silu(x):
    return x * jax.nn.sigmoid(x)


def _params(sem):
    return pltpu.CompilerParams(dimension_semantics=sem, vmem_limit_bytes=VMEM_LIMIT)


def _const_spec(shape):
    n = len(shape)
    return pl.BlockSpec(shape, lambda *_: (0,) * n)


def _mod_body(c_ref, w_ref, b_ref, o_ref):
    sc = _silu(c_ref[...])
    o_ref[0] = jnp.dot(sc, w_ref[0], precision=lax.Precision.HIGHEST,
                       preferred_element_type=F32) + b_ref[0]


def _modulation(cc, ada_w, ada_b):
    depth, d, w6 = ada_w.shape
    rows = cc.shape[0]
    nblk = w6 // d
    return pl.pallas_call(
        _mod_body,
        grid=(depth, nblk),
        in_specs=[pl.BlockSpec((rows, d), lambda l, j: (0, 0)),
                  pl.BlockSpec((1, d, d), lambda l, j: (l, 0, j)),
                  pl.BlockSpec((1, 1, d), lambda l, j: (l, 0, j))],
        out_specs=pl.BlockSpec((1, rows, d), lambda l, j: (l, 0, j)),
        out_shape=jax.ShapeDtypeStruct((depth, rows, w6), F32),
        compiler_params=_params(("parallel", "parallel")),
        name="adaln_mod",
    )(cc, ada_w, ada_b.reshape(depth, 1, w6))


def _attn_proj_body(group, nt, assemble, *refs):
    per = 5 if assemble else 4
    tiles = [refs[per * m:per * (m + 1)] for m in range(group)]
    g_ref, w_ref, qkg_ref = refs[per * group:per * group + 3]
    outs = refs[per * group + 3:]
    hs = []
    for m, tile in enumerate(tiles):
        if assemble:
            j = (pl.program_id(0) * group + m) % nt
            x = jnp.where(j == 0, tile[0][0], tile[1][0])
            outs[3][m] = x
        else:
            x = tile[0][0]
        mod = tile[-3][0, 0]
        hs.append((_rms(x, g_ref[...]) * (1.0 + mod[1:2]) + mod[0:1]).astype(BF16))
    for m, (tile, h) in enumerate(zip(tiles, hs)):
        _attn_proj_heads(m, h, w_ref, tile[-2], tile[-1], qkg_ref, *outs[:3])


def _attn_proj_heads(m, h, w_ref, cos_ref, sin_ref, qkg_ref, qt_out, k_out, vt_out):
    def project(r0, rows):
        return lax.dot_general(w_ref[r0:r0 + rows, :], h, _NT, preferred_element_type=F32)

    pk = project(Q_W, K_W)
    pq = project(0, Q_W)
    pv = project(Q_W + K_W, V_W)
    cos = cos_ref[...]
    sin = sin_ref[...]

    def rope(xh):
        swapped = jnp.concatenate([xh[16:32], xh[0:16], xh[48:64], xh[32:48]], axis=0)
        return xh * cos + swapped * sin

    def norm(xh, g):
        r = lax.rsqrt(jnp.mean(xh * xh, axis=0, keepdims=True) + RMS_EPS)
        return xh * r * g

    gq_g = qkg_ref[0]
    gk_g = qkg_ref[1]
    k_heads = []
    for j in range(K_W // HEAD_DIM):
        r0 = j * HEAD_DIM
        xh = pk[r0:r0 + HEAD_DIM]
        if r0 >= DIFF_QK_W:
            xh = norm(xh, gk_g)
        k_heads.append(rope(xh))
    k_out[m] = jnp.concatenate(k_heads, axis=0).T.astype(BF16)
    for j in range(Q_W // HEAD_DIM):
        r0 = j * HEAD_DIM
        xh = pq[r0:r0 + HEAD_DIM]
        if r0 >= DIFF_QK_W:
            xh = norm(xh, gq_g)
        qt_out[m, r0:r0 + HEAD_DIM, :] = (rope(xh) * (ATTN_SCALE * LOG2_E)).astype(BF16)
    ones = jnp.ones((SUM_ROWS, pv.shape[1]), F32)
    v_blocks = []
    for r0, width in ([(hd * DIFF_V_DIM, DIFF_V_DIM) for hd in range(DIFF_HEADS)]
                      + [(DIFF_V_W + grp * HEAD_DIM, HEAD_DIM) for grp in range(GQA_KV_HEADS)]):
        v_blocks += [pv[r0:r0 + width], ones]
    vt_out[m] = jnp.concatenate(v_blocks, axis=0).astype(BF16)


def _attn_proj(xa, modl, g0, w_t, cos_t, sin_t, qkg, ctx=None):
    assemble = ctx is not None
    b, t, d = xa.shape
    if assemble:
        t += CTX_LEN
    nt = t // TOK_TILE
    group = PROJ_GROUP
    assert (b * nt) % group == 0
    steps = b * nt // group

    def at(m, fn):
        def index_map(p):
            n = p * group + m
            return fn(n // nt, n % nt)
        return index_map

    in_specs, args = [], []
    for m in range(group):
        if assemble:
            in_specs += [pl.BlockSpec((1, TOK_TILE, d), at(m, lambda i, j: (i, 0, 0))),
                         pl.BlockSpec((1, TOK_TILE, d), at(m, lambda i, j: (i, jnp.maximum(j - 1, 0), 0)))]
            args += [ctx, xa]
        else:
            in_specs.append(pl.BlockSpec((1, TOK_TILE, d), at(m, lambda i, j: (i, j, 0))))
            args.append(xa)
        in_specs += [pl.BlockSpec((1, 1, 6, d), at(m, lambda i, j: (i, jnp.minimum(j, 1), 0, 0))),
                     pl.BlockSpec((HEAD_DIM, TOK_TILE), at(m, lambda i, j: (0, j))),
                     pl.BlockSpec((HEAD_DIM, TOK_TILE), at(m, lambda i, j: (0, j)))]
        args += [modl, cos_t, sin_t]
    in_specs += [_const_spec((1, d)), _const_spec(w_t.shape), _const_spec((2, HEAD_DIM, 1))]
    args += [g0, w_t, qkg]
    out_specs = [pl.BlockSpec((group, Q_W, TOK_TILE), lambda p: (p, 0, 0)),
                 pl.BlockSpec((group, TOK_TILE, K_W), lambda p: (p, 0, 0)),
                 pl.BlockSpec((group, VT_ROWS, TOK_TILE), lambda p: (p, 0, 0))]
    out_shape = [jax.ShapeDtypeStruct((b * nt, Q_W, TOK_TILE), BF16),
                 jax.ShapeDtypeStruct((b * nt, TOK_TILE, K_W), BF16),
                 jax.ShapeDtypeStruct((b * nt, VT_ROWS, TOK_TILE), BF16)]
    if assemble:
        out_specs.append(pl.BlockSpec((group, TOK_TILE, d), lambda p: (p, 0, 0)))
        out_shape.append(jax.ShapeDtypeStruct((b * nt, TOK_TILE, d), F32))
    outs = pl.pallas_call(
        functools.partial(_attn_proj_body, group, nt, assemble),
        grid=(steps,),
        in_specs=in_specs,
        out_specs=out_specs,
        out_shape=out_shape,
        compiler_params=_params(("parallel",)),
        name="attn_in_proj",
    )(*args)
    res = (outs[0].reshape(b, nt, Q_W, TOK_TILE), outs[1].reshape(b, t, K_W),
           outs[2].reshape(b, nt, VT_ROWS, TOK_TILE))
    return res + (outs[3].reshape(b, t, d),) if assemble else res


def _attn_body(lam_init, qt_ref, k_ref, vt_ref, lam_ref, subln_ref, o_ref, qpad_ref, m_ref, acc_ref):
    qi = pl.program_id(1)
    slab = 2 * HEAD_DIM
    diff_blk = DIFF_V_DIM + SUM_ROWS
    gqa_blk = HEAD_DIM + SUM_ROWS
    maps = []
    for hd in range(DIFF_HEADS):
        for mm in range(2):
            maps.append((2 * hd + mm, hd, hd * diff_blk, diff_blk))
    for hq in range(GQA_Q_HEADS):
        grp = hq // GQA_REP
        maps.append((2 * DIFF_HEADS + hq, DIFF_HEADS, DIFF_HEADS * diff_blk + grp * gqa_blk, gqa_blk))

    zeros = jnp.zeros((HEAD_DIM, qt_ref.shape[3]), BF16)
    for i, (qh, _, _, _) in enumerate(maps):
        half = (qh % 2) if qh < 2 * DIFF_HEADS else (qh - 2 * DIFF_HEADS) // GQA_REP
        qh_t = qt_ref[0, 0, qh * HEAD_DIM:(qh + 1) * HEAD_DIM, :]
        qpad_ref[i] = jnp.concatenate([qh_t, zeros] if half == 0 else [zeros, qh_t], axis=0)
    def scores(c, i):
        ks = maps[i][1]
        return jnp.dot(k_ref[0, c * KV_CHUNK:(c + 1) * KV_CHUNK, ks * slab:(ks + 1) * slab], qpad_ref[i],
                       preferred_element_type=F32)

    def run(n_chunks):
        items = [(c, i) for c in range(n_chunks) for i in range(N_MAPS)]
        pending = [scores(*items[n]) for n in range(QK_AHEAD)]
        for n, (c, i) in enumerate(items):
            s = pending.pop(0)
            if n + QK_AHEAD < len(items):
                pending.append(scores(*items[n + QK_AHEAD]))
            _, _, v_row0, v_w = maps[i]
            m_new = jnp.max(s, axis=0, keepdims=True)
            if c > 0:
                m_old = m_ref[i]
                m_new = jnp.maximum(m_old, m_new)
                alpha = jnp.exp2(m_old - m_new)
            m_ref[i] = m_new
            p = jnp.exp2(s - m_new)
            pv = jnp.dot(vt_ref[0, c, v_row0:v_row0 + v_w, :], p.astype(BF16), preferred_element_type=F32)
            acc_ref[i, 0:v_w, :] = pv if c == 0 else alpha * acc_ref[i, 0:v_w, :] + pv

    @pl.when(qi == 0)
    def _():
        run(CTX_LEN // KV_CHUNK)

    @pl.when(qi > 0)
    def _():
        run(vt_ref.shape[1])

    lv = lam_ref[...]
    lam = (jnp.exp(jnp.sum(lv[0:1] * lv[1:2], axis=-1, keepdims=True))
           - jnp.exp(jnp.sum(lv[2:3] * lv[3:4], axis=-1, keepdims=True)) + lam_init)
    def normalized(i, width):
        return acc_ref[i, 0:width, :] / acc_ref[i, width:width + 1, :]

    for hd in range(DIFF_HEADS):
        od = normalized(2 * hd, DIFF_V_DIM) - lam * normalized(2 * hd + 1, DIFF_V_DIM)
        od = od * lax.rsqrt(jnp.mean(od * od, axis=0, keepdims=True) + RMS_EPS) * subln_ref[...]
        o_ref[0, 0, hd * DIFF_V_DIM:(hd + 1) * DIFF_V_DIM, :] = (od * (1.0 - lam_init)).astype(o_ref.dtype)
    for hq in range(GQA_Q_HEADS):
        r0 = DIFF_V_W + hq * HEAD_DIM
        o_ref[0, 0, r0:r0 + HEAD_DIM, :] = normalized(2 * DIFF_HEADS + hq, HEAD_DIM).astype(o_ref.dtype)


def _attention(qt, k, vt, lam_vec, subln_g, lam_init):
    b, t, _ = k.shape
    nq = t // TOK_TILE
    return pl.pallas_call(
        functools.partial(_attn_body, lam_init),
        grid=(b, nq),
        in_specs=[pl.BlockSpec((1, 1, Q_W, TOK_TILE), lambda i, j: (i, j, 0, 0)),
                  pl.BlockSpec((1, t, K_W), lambda i, j: (i, 0, 0)),
                  pl.BlockSpec((1,) + vt.shape[1:], lambda i, j: (i, 0, 0, 0)),
                  _const_spec((4, HEAD_DIM)),
                  _const_spec((DIFF_V_DIM, 1))],
        out_specs=pl.BlockSpec((1, 1, D_MODEL, TOK_TILE), lambda i, j: (i, j, 0, 0)),
        out_shape=jax.ShapeDtypeStruct((b, nq, D_MODEL, TOK_TILE), BF16),
        scratch_shapes=[pltpu.VMEM((N_MAPS, 2 * HEAD_DIM, TOK_TILE), BF16),
                        pltpu.VMEM((N_MAPS, 1, TOK_TILE), F32),
                        pltpu.VMEM((N_MAPS, DIFF_V_DIM + SUM_ROWS, TOK_TILE), F32)],
        compiler_params=_params(("parallel", "parallel")),
        name="diff_gqa_attention",
    )(qt, k, vt, lam_vec, subln_g)


def _post_body(gdn, group, *refs):
    per = 5 if gdn else 3
    tiles = [refs[per * m:per * (m + 1)] for m in range(group)]
    rest = refs[per * group:]
    if gdn:
        ng_ref, rest = rest[0], rest[1:]
    g_ref, wo_ref, wgu_ref, wd_ref, xo_ref, a_ref = rest
    g = g_ref[...]

    def mixer_out(tile):
        if not gdn:
            return tile[1][0, 0].astype(F32).T.astype(BF16)
        _, of_ref, ob_ref, sz_ref, _ = tile
        parts = []
        for hd in range(GDN_HEADS):
            o = of_ref[0, hd].astype(F32) + ob_ref[0, hd].astype(F32)
            parts.append(_rms(o, ng_ref[...]))
        return (jnp.concatenate(parts, axis=-1) * sz_ref[0].astype(F32)).astype(BF16)

    def out_proj(tile):
        return jnp.dot(mixer_out(tile), wo_ref[...], preferred_element_type=F32)

    ys = [out_proj(tile) for tile in (tiles[:1] if gdn else tiles)]
    for m, tile in enumerate(tiles):
        y = ys[m]
        mod = tile[-1][0, 0]
        x = tile[0][0] + mod[2:3] * _rms(y, g[1:2])
        h = (_rms(x, g[2:3]) * (1.0 + mod[4:5]) + mod[3:4]).astype(BF16)
        for c in range(FFN_HIDDEN // FFN_CHUNK):
            c0 = c * FFN_CHUNK
            gate = jnp.dot(h, wgu_ref[:, c0:c0 + FFN_CHUNK], preferred_element_type=F32)
            up = jnp.dot(h, wgu_ref[:, FFN_HIDDEN + c0:FFN_HIDDEN + c0 + FFN_CHUNK],
                         preferred_element_type=F32)
            a_ref[m, :, c0:c0 + FFN_CHUNK] = (_silu(gate) * up).astype(BF16)
            if gdn and c == READOUT_AT_CHUNK and m + 1 < group:
                ys.append(out_proj(tiles[m + 1]))
        ff = jnp.dot(a_ref[m], wd_ref[...], preferred_element_type=F32)
        xo_ref[0, m * TOK_TILE:(m + 1) * TOK_TILE, :] = x + mod[5:6] * _rms(ff, g[3:4])


def _post_ffn(xa, mixer_out, modl, g, wo, wgu, wd, gdn, latent_only):
    b, t, d = xa.shape
    skip = CTX_LEN // TOK_TILE if latent_only else 0
    nt = t // TOK_TILE - skip
    group = FFN_GROUP
    assert (b * nt) % group == 0

    def at(m, fn):
        def index_map(p):
            n = p * group + m
            return fn(n // nt, n % nt + skip)
        return index_map

    in_specs, args = [], []
    for m in range(group):
        tile = pl.BlockSpec((1, TOK_TILE, d), at(m, lambda i, j: (i, j, 0)))
        if gdn:
            of, ob, sz, ng = mixer_out
            head_tile = pl.BlockSpec((1, GDN_HEADS, TOK_TILE, GDN_HEAD_DIM), at(m, lambda i, j: (i, 0, j, 0)))
            in_specs += [tile, head_tile, head_tile, tile]
            args += [xa, of, ob, sz]
        else:
            in_specs += [tile, pl.BlockSpec((1, 1, d, TOK_TILE), at(m, lambda i, j: (i, j, 0, 0)))]
            args += [xa, mixer_out]
        in_specs.append(pl.BlockSpec((1, 1, 6, d), at(m, lambda i, j: (i, jnp.minimum(j, 1), 0, 0))))
        args.append(modl)
    if gdn:
        in_specs.append(_const_spec((1, GDN_HEAD_DIM)))
        args.append(ng)
    in_specs += [_const_spec((4, d)), _const_spec(wo.shape), _const_spec(wgu.shape), _const_spec(wd.shape)]
    args += [g, wo, wgu, wd]
    rows = group * TOK_TILE
    out = pl.pallas_call(
        functools.partial(_post_body, gdn, group),
        grid=(b * nt // group,),
        in_specs=in_specs,
        out_specs=pl.BlockSpec((1, rows, d), lambda p: (p, 0, 0)),
        out_shape=jax.ShapeDtypeStruct((b * nt // group, rows, d), F32),
        scratch_shapes=[pltpu.VMEM((group, TOK_TILE, FFN_HIDDEN), BF16)],
        compiler_params=_params(("parallel",)),
        name="out_proj_ffn_gdn" if gdn else "out_proj_ffn_attn",
    )(*args)
    return out.reshape(b, nt * TOK_TILE, d)


def _gdn_proj_body(xp_ref, x_ref, xn_ref, mod_ref, g_ref, wqkv_ref, wz_ref, wab_ref, wabt_ref,
                   conv_ref, alog_ref, dtb_ref, alogt_ref, dtbt_ref,
                   q_out, k_out, v_out, sz_out, gb_out, gbt_out, xs_ref, ys_ref):
    j = pl.program_id(1)
    nt = pl.num_programs(1)
    mod = mod_ref[0, 0]
    g = g_ref[...]
    lanes = GDN_HEAD_DIM
    rows = TOK_TILE + 2 * CONV_HALO
    pitch = rows // 8

    def prep(xv):
        return _rms(xv, g) * (1.0 + mod[1:2]) + mod[0:1]

    prev_ok = jnp.where(j >= 2, 1.0, 0.0)
    next_ok = jnp.where(jnp.logical_and(j >= 1, j < nt - 1), 1.0, 0.0)
    h_f32 = prep(x_ref[0])
    h_main = h_f32.astype(BF16)
    z = jnp.dot(h_main, wz_ref[...], preferred_element_type=F32)
    h_cat = jnp.concatenate([prep(xp_ref[0]) * prev_ok, h_f32, prep(xn_ref[0]) * next_ok], axis=0)
    n_slab = h_cat.shape[1] // lanes
    for s in range(n_slab):
        xs_ref[s] = h_cat[:, s * lanes:(s + 1) * lanes]
    h_perm = jnp.concatenate(
        [jnp.concatenate([xs_ref[s, pl.ds(a, 8, stride=pitch), :] for s in range(n_slab)], axis=1)
         for a in range(pitch)], axis=0).astype(BF16)
    sections = [jnp.dot(h_perm, wqkv_ref[:, s * GDN_W:(s + 1) * GDN_W], preferred_element_type=F32)
                for s in range(3)]
    for sec, (p, out) in enumerate(zip(sections, (q_out, k_out, v_out))):
        cw = conv_ref[:, sec * GDN_W:(sec + 1) * GDN_W]

        def taps(groups):
            acc = groups[0] * cw[0:1]
            for tap in range(1, GDN_CONV_K):
                acc = acc + groups[tap] * cw[tap:tap + 1]
            return acc

        def grp(a):
            return p[8 * a:8 * (a + 1)]

        below = [pltpu.roll(grp(pitch - 2), 1, 0), pltpu.roll(grp(pitch - 1), 1, 0)]
        above = pltpu.roll(grp(0), 7, 0)
        mid = taps([p[8 * t:8 * (t + pitch - 3)] for t in range(GDN_CONV_K)])
        conv = jnp.concatenate([taps([below[0], below[1], grp(0), grp(1)]),
                                taps([below[1], grp(0), grp(1), grp(2)]),
                                mid,
                                taps([grp(pitch - 3), grp(pitch - 2), grp(pitch - 1), above])], axis=0)
        act = _silu(conv)
        for hd in range(GDN_HEADS):
            val = act[:, hd * lanes:(hd + 1) * lanes]
            if sec < 2:
                val = val * lax.rsqrt(jnp.sum(val * val, axis=-1, keepdims=True) + L2_EPS)
            if sec == 0:
                val = val * (GDN_HEAD_DIM ** -0.5)
            slab = sec * GDN_HEADS + hd
            for a in range(pitch):
                ys_ref[slab, pl.ds(a, 8, stride=pitch), :] = val[8 * a:8 * (a + 1)]
            out[0, hd] = ys_ref[slab, CONV_HALO:CONV_HALO + TOK_TILE, :].astype(BF16)
    sz_out[0] = _silu(z).astype(BF16)

    nh2 = 2 * GDN_HEADS
    ab = jnp.dot(h_main, wab_ref[...], preferred_element_type=F32)
    gdec = -jnp.exp(alog_ref[...]) * jax.nn.softplus(ab[:, :nh2] + dtb_ref[...])
    gb_out[0] = jnp.concatenate([gdec, jax.nn.sigmoid(ab[:, nh2:])], axis=-1)
    abt = lax.dot_general(wabt_ref[...], h_main, _NT, preferred_element_type=F32)
    gdec_t = -jnp.exp(alogt_ref[...]) * jax.nn.softplus(abt[:nh2] + dtbt_ref[...])
    gbt_out[0] = jnp.concatenate([gdec_t, jax.nn.sigmoid(abt[nh2:])], axis=0)


def _gdn_proj(xa, modl, g0, wqkv, wz, wab, wab_t, conv_w, a_log, dt_bias):
    b, t, d = xa.shape
    nt = t // TOK_TILE
    per = TOK_TILE // CONV_HALO
    last = t // CONV_HALO - 1
    nh2 = 2 * GDN_HEADS
    head_out = pl.BlockSpec((1, GDN_HEADS, TOK_TILE, GDN_HEAD_DIM), lambda i, j: (i, 0, j, 0))
    head_shape = jax.ShapeDtypeStruct((b, GDN_HEADS, t, GDN_HEAD_DIM), BF16)
    return pl.pallas_call(
        _gdn_proj_body,
        grid=(b, nt),
        in_specs=[pl.BlockSpec((1, CONV_HALO, d), lambda i, j: (i, jnp.maximum(j * per - 1, 0), 0)),
                  pl.BlockSpec((1, TOK_TILE, d), lambda i, j: (i, j, 0)),
                  pl.BlockSpec((1, CONV_HALO, d), lambda i, j: (i, jnp.minimum((j + 1) * per, last), 0)),
                  pl.BlockSpec((1, 1, 6, d), lambda i, j: (i, jnp.minimum(j, 1), 0, 0)),
                  _const_spec((1, d)),
                  _const_spec(wqkv.shape), _const_spec(wz.shape), _const_spec(wab.shape),
                  _const_spec(wab_t.shape), _const_spec(conv_w.shape),
                  _const_spec((1, nh2)), _const_spec((1, nh2)),
                  _const_spec((nh2, 1)), _const_spec((nh2, 1))],
        out_specs=[head_out, head_out, head_out,
                   pl.BlockSpec((1, TOK_TILE, d), lambda i, j: (i, j, 0)),
                   pl.BlockSpec((1, TOK_TILE, 2 * nh2), lambda i, j: (i, j, 0)),
                   pl.BlockSpec((1, 2 * nh2, TOK_TILE), lambda i, j: (i, 0, j))],
        out_shape=[head_shape, head_shape, head_shape,
                   jax.ShapeDtypeStruct((b, t, d), BF16),
                   jax.ShapeDtypeStruct((b, t, 2 * nh2), F32),
                   jax.ShapeDtypeStruct((b, 2 * nh2, t), F32)],
        scratch_shapes=[pltpu.VMEM((d // GDN_HEAD_DIM, TOK_TILE + 2 * CONV_HALO, GDN_HEAD_DIM), F32),
                        pltpu.VMEM((3 * GDN_HEADS, TOK_TILE + 2 * CONV_HALO, GDN_HEAD_DIM), F32)],
        compiler_params=_params(("parallel", "parallel")),
        name="gdn_in_proj",
    )(xa, xa, xa, modl, g0, wqkv, wz, wab, wab_t, conv_w,
      a_log.reshape(1, nh2), dt_bias.reshape(1, nh2), a_log.reshape(nh2, 1), dt_bias.reshape(nh2, 1))


def _gdn_scan_body(qf, kf, vf, qb, kb, vb, gbf, gbb, gtf, gtb, of, ob, s_ref):
    @pl.when(pl.program_id(1) == 0)
    def _():
        s_ref[...] = jnp.zeros_like(s_ref)

    c = GDN_CHUNK
    row = lax.broadcasted_iota(jnp.int32, (c, c), 0)
    col = lax.broadcasted_iota(jnp.int32, (c, c), 1)
    lower = jnp.where(row >= col, 1.0, 0.0)
    upper = jnp.where(row <= col, 1.0, 0.0)
    row = lax.broadcasted_iota(jnp.int32, (c, 2 * c), 0)
    col = lax.broadcasted_iota(jnp.int32, (c, 2 * c), 1)
    nh2 = 2 * GDN_HEADS
    hi = lax.Precision.HIGHEST
    n_sub = qf.shape[2] // c
    refs = (qf, kf, vf, qb, kb, vb, gbf, gbb, gtf, gtb, of, ob)
    chunks = [[] for _ in range(n_sub)]
    local = [_gdn_local_stages(chunks[sub], sub, n_sub - 1 - sub, c, row, col, lower, upper, nh2, hi, refs)
             for sub in range(n_sub)]
    carried = [_gdn_state_stages(chunks[sub], c, s_ref) for sub in range(n_sub)]
    for first in local[:SCAN_LOCAL_AHEAD]:
        for _ in first:
            pass
    for sub in range(n_sub):
        ahead = local[sub + SCAN_LOCAL_AHEAD] if sub + SCAN_LOCAL_AHEAD < n_sub else iter(())
        for _ in carried[sub]:
            for _ in range(LOCAL_STAGES_PER_STATE_STAGE):
                next(ahead, None)
        for _ in ahead:
            pass


def _gdn_chunk_chains(sub_f, sub_b, c, row, col, lower, upper, nh2, hi,
                      qf, kf, vf, qb, kb, vb, gbf, gbb, gtf, gtb, of, ob):
    dh = GDN_HEAD_DIM
    left = col < c
    zeros = jnp.zeros((c, dh), BF16)

    def wide(x0, x1):
        return jnp.concatenate([jnp.broadcast_to(x0, (c, dh)), jnp.broadcast_to(x1, (c, dh))], axis=1)

    pairs = []
    for direction, (q_ref, k_ref, v_ref, gb_ref, gt_ref, o_ref, sub) in enumerate(
            ((qf, kf, vf, gbf, gtf, of, sub_f), (qb, kb, vb, gbb, gtb, ob, sub_b))):
        r0 = sub * c
        gb = gb_ref[0, r0:r0 + c, :]
        gt = gt_ref[0, :, r0:r0 + c]
        tri_c, tri_r = (lower, upper) if direction == 0 else (upper, lower)
        gc = jnp.dot(tri_c, gb[:, :nh2], precision=hi, preferred_element_type=F32)
        gr = jnp.dot(gt[:nh2], tri_r, precision=hi, preferred_element_type=F32)
        incl = (row >= (col & (c - 1))) if direction == 0 else (row <= (col & (c - 1)))
        strict = (row > (col & (c - 1))) if direction == 0 else (row < (col & (c - 1)))
        for pr in range(GDN_HEADS // 2):
            h0 = 2 * pr
            ch = direction * GDN_HEADS + h0
            gcol0, gcol1 = gc[:, ch:ch + 1], gc[:, ch + 1:ch + 2]
            beta0, beta1 = gb[:, nh2 + ch:nh2 + ch + 1], gb[:, nh2 + ch + 1:nh2 + ch + 2]
            gcol = jnp.where(left, gcol0, gcol1)
            grow = jnp.concatenate([gr[ch:ch + 1, :], gr[ch + 1:ch + 2, :]], axis=1)
            edge = c - 1 if direction == 0 else 0
            tot0, tot1 = gcol0[edge:edge + 1], gcol1[edge:edge + 1]
            k0, k1 = k_ref[0, h0, r0:r0 + c, :], k_ref[0, h0 + 1, r0:r0 + c, :]
            q0, q1 = q_ref[0, h0, r0:r0 + c, :], q_ref[0, h0 + 1, r0:r0 + c, :]
            pairs.append(dict(
                idx=direction * (GDN_HEADS // 2) + pr, h0=h0, lower=direction == 0, strict=strict,
                o_ref=o_ref, r0=r0, left=left,
                kq=jnp.concatenate([jnp.concatenate([k0, q0], axis=0),
                                    jnp.concatenate([k1, q1], axis=0)], axis=1),
                k_diag=jnp.concatenate([jnp.concatenate([k0, zeros], axis=1),
                                        jnp.concatenate([zeros, k1], axis=1)], axis=0),
                k_rows=jnp.concatenate([k0, k1], axis=0),
                v=jnp.concatenate([v_ref[0, h0, r0:r0 + c, :], v_ref[0, h0 + 1, r0:r0 + c, :]], axis=1),
                beta=jnp.where(left, beta0, beta1), beta_w=wide(beta0, beta1),
                egc_w=wide(jnp.exp(gcol0), jnp.exp(gcol1)),
                dec_w=wide(jnp.exp(tot0 - gcol0), jnp.exp(tot1 - gcol1)),
                etot_w=jnp.concatenate([jnp.broadcast_to(jnp.exp(tot0), (1, dh)),
                                        jnp.broadcast_to(jnp.exp(tot1), (1, dh))], axis=1),
                decay=jnp.where(incl, jnp.exp(jnp.where(incl, gcol - grow, 0.0)), 0.0)))
    return pairs


def _gdn_local_stages(chains, sub_f, sub_b, c, row, col, lower, upper, nh2, hi, refs):
    chains.extend(_gdn_chunk_chains(sub_f, sub_b, c, row, col, lower, upper, nh2, hi, *refs))
    for w in chains:
        w["gram"] = lax.dot_general(w["kq"], w["k_diag"], _NT, preferred_element_type=F32)
    yield
    ms = [jnp.where(w["strict"], w["beta"] * w["gram"][:c] * w["decay"], 0.0) for w in chains]
    left = col < c
    colh = col & (c - 1)

    def diag2(x):
        return jnp.concatenate([jnp.where(left, x, 0.0), jnp.where(left, 0.0, x)], axis=0).astype(BF16)

    t_offs = None
    k = 1
    while k < c:
        same = (row ^ colh) < 2 * k
        joins = {True: same & ((row & k) != 0) & ((colh & k) == 0),
                 False: same & ((colh & k) != 0) & ((row & k) == 0)}
        parts = [jnp.where(joins[w["lower"]], m, 0.0) for m, w in zip(ms, chains)]
        if t_offs is None:
            t_offs = [-a for a in parts]
        else:
            zs = [a + jnp.dot(t.astype(BF16), diag2(a), preferred_element_type=F32)
                  for a, t in zip(parts, t_offs)]
            yield
            t_offs = [t - z - jnp.dot(z.astype(BF16), diag2(t), preferred_element_type=F32)
                      for t, z in zip(t_offs, zs)]
            yield
        k *= 2
    eye = jnp.where(row == colh, 1.0, 0.0)
    for w, t in zip(chains, t_offs):
        w["tmat"] = (t + eye).astype(BF16)


def _gdn_state_stages(chains, c, s_ref):
    dh = GDN_HEAD_DIM

    def diag2(x):
        z = jnp.zeros((x.shape[0], dh), x.dtype)
        return jnp.concatenate([jnp.concatenate([x[:, :dh], z], axis=1),
                                jnp.concatenate([z, x[:, dh:]], axis=1)], axis=0)

    for w in chains:
        w["state"] = s_ref[w["idx"]]
        w["ks"] = jnp.dot(w["kq"], diag2(w["state"].astype(BF16)), preferred_element_type=F32)
    yield
    for w in chains:
        resid = (w["v"].astype(F32) - w["ks"][:c] * w["egc_w"]) * w["beta_w"]
        w["v_new"] = jnp.dot(w["tmat"], diag2(resid.astype(BF16)), preferred_element_type=F32)
        w["qs"] = w["ks"][c:] * w["egc_w"]
    yield
    for w in chains:
        attn = (w["gram"][c:] * w["decay"]).astype(BF16)
        o = w["qs"] + jnp.dot(attn, diag2(w["v_new"].astype(BF16)), preferred_element_type=F32)
        for m in range(2):
            w["o_ref"][0, w["h0"] + m, w["r0"]:w["r0"] + c, :] = o[:, m * dh:(m + 1) * dh].astype(w["o_ref"].dtype)
    yield
    for w in chains:
        v_dec = diag2((w["v_new"] * w["dec_w"]).astype(BF16))
        s_ref[w["idx"]] = (w["state"] * w["etot_w"]
                           + lax.dot_general(w["k_rows"], v_dec, _TN, preferred_element_type=F32))
    yield


def _gdn_scan(q, k, v, gb, gbt):
    b, nh, t, dh = q.shape
    nt = t // TOK_TILE
    nh2 = 2 * GDN_HEADS

    def bwd(s):
        return jnp.where(s == 0, 0, nt - s)

    head_f = pl.BlockSpec((1, nh, TOK_TILE, dh), lambda i, s: (i, 0, s, 0))
    head_b = pl.BlockSpec((1, nh, TOK_TILE, dh), lambda i, s: (i, 0, bwd(s), 0))
    out_shape = jax.ShapeDtypeStruct((b, nh, t, dh), BF16)
    return pl.pallas_call(
        _gdn_scan_body,
        grid=(b, nt),
        in_specs=[head_f, head_f, head_f, head_b, head_b, head_b,
                  pl.BlockSpec((1, TOK_TILE, 2 * nh2), lambda i, s: (i, s, 0)),
                  pl.BlockSpec((1, TOK_TILE, 2 * nh2), lambda i, s: (i, bwd(s), 0)),
                  pl.BlockSpec((1, 2 * nh2, TOK_TILE), lambda i, s: (i, 0, s)),
                  pl.BlockSpec((1, 2 * nh2, TOK_TILE), lambda i, s: (i, 0, bwd(s)))],
        out_specs=[head_f, head_b],
        out_shape=[out_shape, out_shape],
        scratch_shapes=[pltpu.VMEM((nh2 // 2, dh, 2 * dh), F32)],
        compiler_params=_params(("arbitrary", "arbitrary")),
        name="gdn_chunk_scan",
    )(q, k, v, q, k, v, gb, gb, gbt, gbt)


def _rope_tables(n_lat):
    rows = n_lat // GRID_W
    row_ids = jnp.repeat(jnp.arange(rows, dtype=F32), GRID_W)[:n_lat]
    col_ids = jnp.tile(jnp.arange(GRID_W, dtype=F32), rows)[:n_lat]
    axis_dim = HEAD_DIM // 2
    inv_freq = ROPE_THETA ** (-jnp.arange(0, axis_dim, 2, dtype=F32) / axis_dim)
    ang_r = row_ids[:, None] * inv_freq
    ang_c = col_ids[:, None] * inv_freq
    ang = jnp.concatenate([ang_r, ang_r, ang_c, ang_c], axis=-1)
    cos = jnp.concatenate([jnp.ones((CTX_LEN, HEAD_DIM), F32), jnp.cos(ang)], axis=0)
    sin = jnp.concatenate([jnp.zeros((CTX_LEN, HEAD_DIM), F32), jnp.sin(ang)], axis=0)
    sign = jnp.tile(jnp.repeat(jnp.array([-1.0, 1.0], F32), HEAD_DIM // 4), 2)
    return cos.T, (sin * sign).T


def kernel(x, c, ctx, c_ctx, ada_w, ada_b, norm_g, attn_w_in, attn_w_out, diff_lambda, diff_subln_g,
           gqa_qk_g, gdn_w_in, gdn_conv_w, gdn_a_log, gdn_dt_bias, gdn_norm_g, gdn_w_out,
           ffn_w_gate_up, ffn_w_down):
    b, n_lat, d = x.shape
    depth = ada_w.shape[0]
    assert d == D_MODEL and ctx.shape[1] == CTX_LEN and n_lat % TOK_TILE == 0
    rows = -(-(b + 1) // V7X_SUBLANES) * V7X_SUBLANES
    cc = jnp.concatenate([c, c_ctx[None], jnp.zeros((rows - b - 1, d), F32)], axis=0)
    mods = _modulation(cc, ada_w, ada_b)
    cos_t, sin_t = _rope_tables(n_lat)
    xa = None

    for l in range(depth):
        i = l // 2
        last = l == depth - 1
        ml = mods[l]
        modl = jnp.stack([jnp.broadcast_to(ml[b].reshape(1, 6, d), (b, 6, d)),
                          ml[:b].reshape(b, 6, d)], axis=1)
        g = norm_g[l]
        wgu = ffn_w_gate_up[l].astype(BF16)
        wd = ffn_w_down[l].astype(BF16)
        if l % 2 == 0:
            lam_init = 0.8 - 0.6 * math.exp(-0.3 * l)
            w = attn_w_in[i]
            o_dv = 2 * DIFF_QK_W
            o_gq = o_dv + DIFF_V_W
            o_gk = o_gq + GQA_Q_W
            o_gv = o_gk + GQA_KV_W
            w_t = jnp.concatenate([w[:, :DIFF_QK_W], w[:, o_gq:o_gk],
                                   w[:, DIFF_QK_W:o_dv], w[:, o_gk:o_gv],
                                   w[:, o_dv:o_gq], w[:, o_gv:]],
                                  axis=1).T.astype(BF16)
            qkg = gqa_qk_g[i].reshape(2, HEAD_DIM, 1)
            if l == 0:
                qt, k, vt, xa = _attn_proj(x, modl, g[0:1], w_t, cos_t, sin_t, qkg, ctx=ctx)
            else:
                qt, k, vt = _attn_proj(xa, modl, g[0:1], w_t, cos_t, sin_t, qkg)
            o = _attention(qt, k, vt, diff_lambda[i], diff_subln_g[i].reshape(DIFF_V_DIM, 1), lam_init)
            xa = _post_ffn(xa, o, modl, g, attn_w_out[i].astype(BF16), wgu, wd, gdn=False, latent_only=last)
        else:
            w = gdn_w_in[i]
            wab = w[:, 4 * GDN_W:]
            q, k, v, sz, gb, gbt = _gdn_proj(
                xa, modl, g[0:1], w[:, :3 * GDN_W].astype(BF16), w[:, 3 * GDN_W:4 * GDN_W].astype(BF16),
                wab.astype(BF16), wab.T.astype(BF16), gdn_conv_w[i], gdn_a_log[i], gdn_dt_bias[i])
            of, ob = _gdn_scan(q, k, v, gb, gbt)
            xa = _post_ffn(xa, (of, ob, sz, gdn_norm_g[i].reshape(1, GDN_HEAD_DIM)), modl, g,
                           gdn_w_out[i].astype(BF16), wgu, wd, gdn=True, latent_only=last)
    return xa
```
